```python
import math
import jax, jax.numpy as jnp
from jax import lax
import numpy as np

D_MODEL = 1024
BATCH = 8
SEQ = 4096
DEPTH = 2

MLA_HEADS = 4
QK_NOPE_DIM = 128
QK_ROPE_DIM = 64
V_HEAD_DIM = 128
Q_LORA_RANK = 384
KV_LORA_RANK = 256
CONV_CHANNELS = D_MODEL - MLA_HEADS * V_HEAD_DIM
CONV_WIDTH = 31
MLA_IN_DIM = Q_LORA_RANK + KV_LORA_RANK + QK_ROPE_DIM + 2 * CONV_CHANNELS
DIFF_HEAD_DIM = 128
DIFF_HEADS = D_MODEL // (2 * DIFF_HEAD_DIM)
DIFF_QKV_DIM = 3 * DIFF_HEADS * 2 * DIFF_HEAD_DIM
N_EXPERTS = 32
TOP_K = 4
D_EXPERT = D_MODEL
SWIGLU_LIMIT = 7.0
SWIGLU_ALPHA = 1.702
EXPERT_BLOCK = 256
ROPE_THETA = 10000.0
Q_BLOCK = 128
DN_ALPHA = (2 * DEPTH) ** 0.25
DN_BETA = (8 * DEPTH) ** -0.25
LN_EPS = 1e-5
RMS_EPS = 1e-6

kernel_name = 'hybrid_mla_conformer_diffattn_moe'


def layer_norm(x, g, b):
    xf = x.astype(jnp.float32)
    mu = jnp.mean(xf, -1, keepdims=True)
    var = jnp.mean(jnp.square(xf - mu), -1, keepdims=True)
    y = (xf - mu) * lax.rsqrt(var + LN_EPS)
    return (y * g.astype(jnp.float32) + b.astype(jnp.float32)).astype(x.dtype)


def rms_norm(x, g):
    xf = x.astype(jnp.float32)
    y = xf * lax.rsqrt(jnp.mean(jnp.square(xf), -1, keepdims=True) + RMS_EPS)
    return (y * g.astype(jnp.float32)).astype(x.dtype)


def apply_rope(x):
    seq, dim = x.shape[1], x.shape[-1]
    inv_freq = 1.0 / (ROPE_THETA ** (jnp.arange(0, dim, 2, dtype=jnp.float32) / dim))
    ang = jnp.arange(seq, dtype=jnp.float32)[:, None] * inv_freq[None, :]
    cos = jnp.cos(ang)[:, None, :]
    sin = jnp.sin(ang)[:, None, :]
    xf = x.astype(jnp.float32)
    x1, x2 = xf[..., : dim // 2], xf[..., dim // 2:]
    return jnp.concatenate([x1 * cos - x2 * sin, x2 * cos + x1 * sin], -1).astype(x.dtype)


def causal_multi_map_attention(q, k, v, coef, scale):
    b, h, m, s, dk = q.shape
    nb = s // Q_BLOCK
    q_blocks = jnp.moveaxis(q.reshape(b, h, m, nb, Q_BLOCK, dk), 3, 0)
    k_pos = jnp.arange(s)

    def one_block(args):
        q_blk, blk = args
        scores = jnp.einsum('bhmqd,bhmkd->bhmqk', q_blk, k).astype(jnp.float32) * scale
        q_pos = blk * Q_BLOCK + jnp.arange(Q_BLOCK)
        causal = k_pos[None, :] <= q_pos[:, None]
        probs = jax.nn.softmax(jnp.where(causal, scores, -1e30), axis=-1)
        weights = jnp.einsum('bhmqk,m->bhqk', probs, coef)
        return jnp.einsum('bhqk,bhkd->bhqd', weights.astype(v.dtype), v)

    out = lax.map(one_block, (q_blocks, jnp.arange(nb)))
    return jnp.moveaxis(out, 0, 2).reshape(b, h, s, v.shape[-1])


def mla_conv_mixer(h, w_in, q_norm_g, w_q_up, kv_norm_g, w_kv_up, conv_w, conv_b,
                   conv_ln_g, conv_ln_b, w_o):
    b, s, _ = h.shape
    proj = h @ w_in
    o1 = Q_LORA_RANK
    o2 = o1 + KV_LORA_RANK
    o3 = o2 + QK_ROPE_DIM
    q_lat, kv_lat, k_rope, conv_in = proj[..., :o1], proj[..., o1:o2], proj[..., o2:o3], proj[..., o3:]
    q = (rms_norm(q_lat, q_norm_g) @ w_q_up).reshape(b, s, MLA_HEADS, QK_NOPE_DIM + QK_ROPE_DIM)
    q = jnp.concatenate([q[..., :QK_NOPE_DIM], apply_rope(q[..., QK_NOPE_DIM:])], -1)
    kv = (rms_norm(kv_lat, kv_norm_g) @ w_kv_up).reshape(b, s, MLA_HEADS, QK_NOPE_DIM + V_HEAD_DIM)
    k_nope, v = kv[..., :QK_NOPE_DIM], kv[..., QK_NOPE_DIM:]
    k_rope = jnp.broadcast_to(apply_rope(k_rope[:, :, None, :]), (b, s, MLA_HEADS, QK_ROPE_DIM))
    k = jnp.concatenate([k_nope, k_rope], -1)
    attn = causal_multi_map_attention(
        q.transpose(0, 2, 1, 3)[:, :, None], k.transpose(0, 2, 1, 3)[:, :, None],
        v.transpose(0, 2, 1, 3), jnp.ones((1,), jnp.float32),
        (QK_NOPE_DIM + QK_ROPE_DIM) ** -0.5)
    attn = attn.transpose(0, 2, 1, 3).reshape(b, s, MLA_HEADS * V_HEAD_DIM)
    a, gate = conv_in[..., :CONV_CHANNELS], conv_in[..., CONV_CHANNELS:]
    u = a * jax.nn.sigmoid(gate)
    u = lax.conv_general_dilated(
        u, conv_w[:, None, :], window_strides=(1,), padding=[(CONV_WIDTH - 1, 0)],
        dimension_numbers=('NWC', 'WIO', 'NWC'), feature_group_count=CONV_CHANNELS) + conv_b
    u = jax.nn.silu(layer_norm(u, conv_ln_g, conv_ln_b))
    return jnp.concatenate([attn, u], -1) @ w_o


def diff_attn_mixer(h, w_qkv, lambda_q1, lambda_k1, lambda_q2, lambda_k2, subln_g, w_o, lambda_init):
    b, s, _ = h.shape
    qk_w = DIFF_HEADS * 2 * DIFF_HEAD_DIM
    proj = h @ w_qkv
    q = apply_rope(proj[..., :qk_w].reshape(b, s, 2 * DIFF_HEADS, DIFF_HEAD_DIM))
    k = apply_rope(proj[..., qk_w:2 * qk_w].reshape(b, s, 2 * DIFF_HEADS, DIFF_HEAD_DIM))
    v = proj[..., 2 * qk_w:].reshape(b, s, DIFF_HEADS, 2 * DIFF_HEAD_DIM)
    q = q.reshape(b, s, DIFF_HEADS, 2, DIFF_HEAD_DIM).transpose(0, 2, 3, 1, 4)
    k = k.reshape(b, s, DIFF_HEADS, 2, DIFF_HEAD_DIM).transpose(0, 2, 3, 1, 4)
    f32 = jnp.float32
    lam = (jnp.exp(jnp.sum(lambda_q1.astype(f32) * lambda_k1.astype(f32)))
           - jnp.exp(jnp.sum(lambda_q2.astype(f32) * lambda_k2.astype(f32))) + lambda_init)
    coef = jnp.stack([jnp.ones_like(lam), -lam])
    attn = causal_multi_map_attention(q, k, v.transpose(0, 2, 1, 3), coef, DIFF_HEAD_DIM ** -0.5)
    attn = rms_norm(attn, subln_g) * (1.0 - lambda_init)
    return attn.transpose(0, 2, 1, 3).reshape(b, s, DIFF_HEADS * 2 * DIFF_HEAD_DIM) @ w_o


def moe_ffn(h, router_w, router_b, w_gu, b_gu, w_dn, b_dn):
    b, s, d = h.shape
    t = b * s
    xf = h.reshape(t, d)
    logits = (xf @ router_w + router_b).astype(jnp.float32)
    top_val, top_idx = lax.top_k(logits, TOP_K)
    gates = jax.nn.softmax(top_val, axis=-1)
    n_assign = t * TOP_K
    flat_e = top_idx.reshape(-1)
    flat_g = gates.reshape(-1)
    flat_tok = jnp.arange(n_assign, dtype=jnp.int32) // TOP_K
    order = jnp.argsort(flat_e)
    se, stok, sg = flat_e[order], flat_tok[order], flat_g[order]
    counts = jnp.bincount(flat_e, length=N_EXPERTS)
    padded = ((counts + EXPERT_BLOCK - 1) // EXPERT_BLOCK) * EXPERT_BLOCK
    pend = jnp.cumsum(padded)
    pstart = pend - padded
    ustart = jnp.cumsum(counts) - counts
    dest = pstart[se] + jnp.arange(n_assign, dtype=jnp.int32) - ustart[se]
    n_rows = ((n_assign + EXPERT_BLOCK - 1) // EXPERT_BLOCK) * EXPERT_BLOCK + N_EXPERTS * EXPERT_BLOCK
    n_blk = n_rows // EXPERT_BLOCK
    row_tok = jnp.zeros((n_rows,), jnp.int32).at[dest].set(stok)
    row_gate = jnp.zeros((n_rows,), jnp.float32).at[dest].set(sg)
    blk_start = jnp.arange(n_blk, dtype=jnp.int32) * EXPERT_BLOCK
    blk_e = jnp.minimum(jnp.searchsorted(pend, blk_start, side='right'), N_EXPERTS - 1)

    def expert_block(args):
        tok, g, e = args
        hgu = xf[tok] @ w_gu[e] + b_gu[e]
        gate = jnp.minimum(hgu[:, :D_EXPERT], SWIGLU_LIMIT)
        up = jnp.clip(hgu[:, D_EXPERT:], -SWIGLU_LIMIT, SWIGLU_LIMIT)
        act = (up + 1.0) * gate * jax.nn.sigmoid(SWIGLU_ALPHA * gate)
        y = act @ w_dn[e] + b_dn[e]
        return y * g[:, None].astype(y.dtype)

    ys = lax.map(expert_block, (row_tok.reshape(n_blk, EXPERT_BLOCK),
                                row_gate.reshape(n_blk, EXPERT_BLOCK), blk_e))
    out = jnp.zeros((t, d), h.dtype).at[row_tok].add(ys.reshape(n_rows, d).astype(h.dtype))
    return out.reshape(b, s, d)


def setup_inputs(seed: int = 0) -> dict:
    key = jax.random.key(seed)
    keys = iter(jax.random.split(key, 64))

    def nrm(shape, scale):
        return jax.random.normal(next(keys), shape, jnp.float32) * scale

    def gain(n):
        return 1.0 + nrm((n,), 0.02)

    def bias(shape):
        return nrm(shape, 0.01)

    def add_moe(p, pre):
        p[pre + 'router_w'] = nrm((D_MODEL, N_EXPERTS), D_MODEL ** -0.5)
        p[pre + 'router_b'] = bias((N_EXPERTS,))
        p[pre + 'w_gu'] = nrm((N_EXPERTS, D_MODEL, 2 * D_EXPERT), D_MODEL ** -0.5)
        p[pre + 'b_gu'] = bias((N_EXPERTS, 2 * D_EXPERT))
        p[pre + 'w_dn'] = nrm((N_EXPERTS, D_EXPERT, D_MODEL), DN_BETA * D_EXPERT ** -0.5)
        p[pre + 'b_dn'] = bias((N_EXPERTS, D_MODEL))

    p = {'x': nrm((BATCH, SEQ, D_MODEL), 1.0)}
    p['l0_w_in'] = nrm((D_MODEL, MLA_IN_DIM), D_MODEL ** -0.5)
    p['l0_q_norm_g'] = gain(Q_LORA_RANK)
    p['l0_w_q_up'] = nrm((Q_LORA_RANK, MLA_HEADS * (QK_NOPE_DIM + QK_ROPE_DIM)), Q_LORA_RANK ** -0.5)
    p['l0_kv_norm_g'] = gain(KV_LORA_RANK)
    p['l0_w_kv_up'] = nrm((KV_LORA_RANK, MLA_HEADS * (QK_NOPE_DIM + V_HEAD_DIM)), KV_LORA_RANK ** -0.5)
    p['l0_conv_w'] = nrm((CONV_WIDTH, CONV_CHANNELS), CONV_WIDTH ** -0.5)
    p['l0_conv_b'] = bias((CONV_CHANNELS,))
    p['l0_conv_ln_g'] = gain(CONV_CHANNELS)
    p['l0_conv_ln_b'] = bias((CONV_CHANNELS,))
    p['l0_w_o'] = nrm((MLA_HEADS * V_HEAD_DIM + CONV_CHANNELS, D_MODEL), DN_BETA * D_MODEL ** -0.5)
    p['l0_ln1_g'] = gain(D_MODEL)
    p['l0_ln1_b'] = bias((D_MODEL,))
    add_moe(p, 'l0_')
    p['l0_ln2_g'] = gain(D_MODEL)
    p['l0_ln2_b'] = bias((D_MODEL,))
    p['l1_w_qkv'] = nrm((D_MODEL, DIFF_QKV_DIM), D_MODEL ** -0.5)
    p['l1_lambda_q1'] = nrm((DIFF_HEAD_DIM,), 0.1)
    p['l1_lambda_k1'] = nrm((DIFF_HEAD_DIM,), 0.1)
    p['l1_lambda_q2'] = nrm((DIFF_HEAD_DIM,), 0.1)
    p['l1_lambda_k2'] = nrm((DIFF_HEAD_DIM,), 0.1)
    p['l1_subln_g'] = gain(2 * DIFF_HEAD_DIM)
    p['l1_w_o'] = nrm((DIFF_HEADS * 2 * DIFF_HEAD_DIM, D_MODEL), DN_BETA * D_MODEL ** -0.5)
    p['l1_ln1_g'] = gain(D_MODEL)
    p['l1_ln1_b'] = bias((D_MODEL,))
    add_moe(p, 'l1_')
    p['l1_ln2_g'] = gain(D_MODEL)
    p['l1_ln2_b'] = bias((D_MODEL,))
    return p


def reference(x,
              l0_w_in, l0_q_norm_g, l0_w_q_up, l0_kv_norm_g, l0_w_kv_up, l0_conv_w, l0_conv_b,
              l0_conv_ln_g, l0_conv_ln_b, l0_w_o, l0_ln1_g, l0_ln1_b,
              l0_router_w, l0_router_b, l0_w_gu, l0_b_gu, l0_w_dn, l0_b_dn, l0_ln2_g, l0_ln2_b,
              l1_w_qkv, l1_lambda_q1, l1_lambda_k1, l1_lambda_q2, l1_lambda_k2, l1_subln_g, l1_w_o,
              l1_ln1_g, l1_ln1_b,
              l1_router_w, l1_router_b, l1_w_gu, l1_b_gu, l1_w_dn, l1_b_dn, l1_ln2_g, l1_ln2_b):
    layer_params = (
        ((l0_w_in, l0_q_norm_g, l0_w_q_up, l0_kv_norm_g, l0_w_kv_up, l0_conv_w, l0_conv_b,
          l0_conv_ln_g, l0_conv_ln_b, l0_w_o),
         (l0_ln1_g, l0_ln1_b),
         (l0_router_w, l0_router_b, l0_w_gu, l0_b_gu, l0_w_dn, l0_b_dn),
         (l0_ln2_g, l0_ln2_b)),
        ((l1_w_qkv, l1_lambda_q1, l1_lambda_k1, l1_lambda_q2, l1_lambda_k2, l1_subln_g, l1_w_o),
         (l1_ln1_g, l1_ln1_b),
         (l1_router_w, l1_router_b, l1_w_gu, l1_b_gu, l1_w_dn, l1_b_dn),
         (l1_ln2_g, l1_ln2_b)),
    )
    for i in range(DEPTH):
        mixer_p, ln1, moe_p, ln2 = layer_params[i]
        if i % 2 == 0:
            mix = mla_conv_mixer(x, *mixer_p)
        else:
            mix = diff_attn_mixer(x, *mixer_p, lambda_init=0.8 - 0.6 * math.exp(-0.3 * i))
        x = layer_norm(DN_ALPHA * x + mix, *ln1)
        x = layer_norm(DN_ALPHA * x + moe_ffn(x, *moe_p), *ln2)
    return x
```

```python
import functools
import math

import jax
import jax.numpy as jnp
from jax import lax
from jax.experimental import pallas as pl
from jax.experimental.pallas import tpu as pltpu

F32 = jnp.float32
BF16 = jnp.bfloat16

D_MODEL = 1024
DEPTH = 2
MLA_HEADS = 4
QK_NOPE_DIM = 128
QK_ROPE_DIM = 64
V_HEAD_DIM = 128
Q_LORA_RANK = 384
KV_LORA_RANK = 256
CONV_CHANNELS = D_MODEL - MLA_HEADS * V_HEAD_DIM
CONV_WIDTH = 31
DIFF_HEAD_DIM = 128
DIFF_HEADS = D_MODEL // (2 * DIFF_HEAD_DIM)
N_EXPERTS = 32
TOP_K = 4
D_EXPERT = D_MODEL
SWIGLU_LIMIT = 7.0
SWIGLU_ALPHA = 1.702
ROPE_THETA = 10000.0
DN_ALPHA = (2 * DEPTH) ** 0.25
LN_EPS = 1e-5
RMS_EPS = 1e-6
MASK_VALUE = -1e30

LANES = 128
TOKEN_TILE = 512
ATTN_TILE = 512
CONV_TILE = 512
CONV_HALO = 32
CONV_CHUNK = 64
EXPERT_ROWS = 512
COMBINE_TILE = 256
VMEM_LIMIT = 56 * 1024 * 1024


def _params(sem, vmem=VMEM_LIMIT):
    return pltpu.CompilerParams(dimension_semantics=sem, vmem_limit_bytes=vmem)


def _layer_norm(r, g, b):
    mu = jnp.mean(r, axis=-1, keepdims=True)
    d = r - mu
    var = jnp.mean(d * d, axis=-1, keepdims=True)
    return d * lax.rsqrt(var + LN_EPS) * g + b


def _rms_norm(x, g):
    return x * lax.rsqrt(jnp.mean(x * x, axis=-1, keepdims=True) + RMS_EPS) * g


def _rope(x, cos, sin):
    return x * cos + pltpu.roll(x, 64, 1) * sin


def _l0_proj_kernel(x_ref, win_ref, qg_ref, wq_ref, kvg_ref, wkv_ref, cos_ref, sin_ref,
                    q_ref, k_ref, v_ref, u_ref):
    xb = x_ref[...].astype(BF16)
    proj = jnp.dot(xb, win_ref[...], preferred_element_type=F32)
    o1 = Q_LORA_RANK
    o2 = o1 + KV_LORA_RANK
    o3 = o2 + LANES
    o4 = o3 + CONV_CHANNELS
    u_ref[...] = proj[:, o3:o4] * jax.nn.sigmoid(proj[:, o4:])
    cos = cos_ref[...]
    sin = sin_ref[...]
    scale = (QK_NOPE_DIM + QK_ROPE_DIM) ** -0.5
    qn = _rms_norm(proj[:, :o1], qg_ref[...])
    qup = jnp.dot(qn.astype(BF16), wq_ref[...], preferred_element_type=F32)
    kvn = _rms_norm(proj[:, o1:o2], kvg_ref[...])
    kvup = jnp.dot(kvn.astype(BF16), wkv_ref[...], preferred_element_type=F32)
    k_rope = _rope(proj[:, o2:o3], cos, sin).astype(BF16)
    for h in range(MLA_HEADS):
        c = 2 * LANES * h
        q_ref[:, c:c + LANES] = (qup[:, c:c + LANES] * scale).astype(BF16)
        q_rope = _rope(qup[:, c + LANES:c + 2 * LANES], cos, sin)
        q_ref[:, c + LANES:c + 2 * LANES] = (q_rope * scale).astype(BF16)
        k_ref[:, c:c + LANES] = kvup[:, LANES * h:LANES * (h + 1)].astype(BF16)
        k_ref[:, c + LANES:c + 2 * LANES] = k_rope
    v_ref[...] = kvup[:, MLA_HEADS * LANES:].astype(BF16)


def _l0_proj(x2d, w_in, qg, wq, kvg, wkv, cos, sin, seq):
    t = x2d.shape[0]
    tm = TOKEN_TILE
    n_pos = seq // tm
    full = lambda a: pl.BlockSpec(a.shape, lambda i: (0,) * a.ndim)
    row = lambda w: pl.BlockSpec((tm, w), lambda i: (i, 0))
    pos = pl.BlockSpec((tm, LANES), lambda i: (i % n_pos, 0))
    hw = MLA_HEADS * 2 * LANES
    return pl.pallas_call(
        _l0_proj_kernel,
        grid=(t // tm,),
        in_specs=[row(D_MODEL), full(w_in), full(qg), full(wq), full(kvg), full(wkv), pos, pos],
        out_specs=[row(hw), row(hw), row(MLA_HEADS * V_HEAD_DIM), row(CONV_CHANNELS)],
        out_shape=[jax.ShapeDtypeStruct((t, hw), BF16), jax.ShapeDtypeStruct((t, hw), BF16),
                   jax.ShapeDtypeStruct((t, MLA_HEADS * V_HEAD_DIM), BF16),
                   jax.ShapeDtypeStruct((t, CONV_CHANNELS), F32)],
        compiler_params=_params(("parallel",)),
        name="l0_proj",
    )(x2d, w_in, qg, wq, kvg, wkv, cos, sin)


def _l1_proj_kernel(x_ref, w_ref, cos_ref, sin_ref, q_ref, k_ref, v_ref):
    xb = x_ref[...].astype(BF16)
    cos = cos_ref[...]
    sin = sin_ref[...]
    scale = DIFF_HEAD_DIM ** -0.5
    qk_w = DIFF_HEADS * 2 * DIFF_HEAD_DIM
    q = jnp.dot(xb, w_ref[:, :qk_w], preferred_element_type=F32)
    for j in range(qk_w // LANES):
        c = j * LANES
        q_ref[:, c:c + LANES] = (_rope(q[:, c:c + LANES], cos, sin) * scale).astype(BF16)
    k = jnp.dot(xb, w_ref[:, qk_w:2 * qk_w], preferred_element_type=F32)
    for j in range(qk_w // LANES):
        c = j * LANES
        k_ref[:, c:c + LANES] = _rope(k[:, c:c + LANES], cos, sin).astype(BF16)
    v_ref[...] = jnp.dot(xb, w_ref[:, 2 * qk_w:], preferred_element_type=F32).astype(BF16)


def _l1_proj(x2d, w_qkv, cos, sin, seq):
    t = x2d.shape[0]
    tm = TOKEN_TILE
    n_pos = seq // tm
    row = lambda w: pl.BlockSpec((tm, w), lambda i: (i, 0))
    pos = pl.BlockSpec((tm, LANES), lambda i: (i % n_pos, 0))
    out = jax.ShapeDtypeStruct((t, D_MODEL), BF16)
    return pl.pallas_call(
        _l1_proj_kernel,
        grid=(t // tm,),
        in_specs=[row(D_MODEL), pl.BlockSpec(w_qkv.shape, lambda i: (0, 0)), pos, pos],
        out_specs=[row(D_MODEL)] * 3,
        out_shape=[out, out, out],
        compiler_params=_params(("parallel",)),
        name="l1_proj",
    )(x2d, w_qkv, cos, sin)


def _attn_kernel(*refs, n_maps, dk, tile, lambda_init):
    if n_maps == 2:
        q_ref, k_ref, v_ref, lam_ref, g_ref, o_ref, m_sc, l_sc, acc_sc = refs
    else:
        q_ref, k_ref, v_ref, o_ref, m_sc, l_sc, acc_sc = refs
    qi = pl.program_id(2)
    m_sc[...] = jnp.full(m_sc.shape, MASK_VALUE, F32)
    l_sc[...] = jnp.zeros(l_sc.shape, F32)
    acc_sc[...] = jnp.zeros(acc_sc.shape, F32)

    def step(j, masked):
        start = pl.multiple_of(j * tile, tile)
        v = v_ref[0, pl.ds(start, tile), :]
        for m in range(n_maps):
            q = q_ref[0, :, m * dk:(m + 1) * dk]
            k = k_ref[0, pl.ds(start, tile), m * dk:(m + 1) * dk]
            s = lax.dot_general(q, k, (((1,), (1,)), ((), ())), preferred_element_type=F32)
            if masked:
                row = lax.broadcasted_iota(jnp.int32, (tile, tile), 0)
                col = lax.broadcasted_iota(jnp.int32, (tile, tile), 1)
                s = jnp.where(col <= row, s, MASK_VALUE)
            m_prev = m_sc[m]
            m_new = jnp.maximum(m_prev, jnp.max(s, axis=1, keepdims=True))
            p = jnp.exp(s - m_new)
            a = jnp.exp(m_prev - m_new)
            l_sc[m] = a * l_sc[m] + jnp.sum(p, axis=1, keepdims=True)
            acc_sc[m] = a * acc_sc[m] + jnp.dot(p.astype(BF16), v, preferred_element_type=F32)
            m_sc[m] = m_new

    def off_diagonal(j, carry):
        step(j, False)
        return carry

    lax.fori_loop(0, qi, off_diagonal, 0)
    step(qi, True)

    if n_maps == 1:
        o_ref[0] = (acc_sc[0] / l_sc[0]).astype(o_ref.dtype)
    else:
        lam_in = lam_ref[...]
        lam = (jnp.exp(jnp.sum(lam_in[0:1] * lam_in[1:2], axis=1, keepdims=True))
               - jnp.exp(jnp.sum(lam_in[2:3] * lam_in[3:4], axis=1, keepdims=True)) + lambda_init)
        a = acc_sc[0] / l_sc[0] - lam * (acc_sc[1] / l_sc[1])
        o_ref[0] = (_rms_norm(a, g_ref[...]) * (1.0 - lambda_init)).astype(o_ref.dtype)


def _attention(q, k, v, n_heads, n_maps, dk, dv, extra=(), lambda_init=0.0):
    b, s, _ = q.shape
    tile = ATTN_TILE
    qw = n_maps * dk
    extra_specs = [pl.BlockSpec(e.shape, lambda bi, h, qi: (0, 0)) for e in extra]
    kern = functools.partial(_attn_kernel, n_maps=n_maps, dk=dk, tile=tile, lambda_init=lambda_init)
    return pl.pallas_call(
        kern,
        grid=(b, n_heads, s // tile),
        in_specs=[pl.BlockSpec((1, tile, qw), lambda bi, h, qi: (bi, qi, h)),
                  pl.BlockSpec((1, s, qw), lambda bi, h, qi: (bi, 0, h)),
                  pl.BlockSpec((1, s, dv), lambda bi, h, qi: (bi, 0, h))] + extra_specs,
        out_specs=pl.BlockSpec((1, tile, dv), lambda bi, h, qi: (bi, qi, h)),
        out_shape=jax.ShapeDtypeStruct((b, s, n_heads * dv), BF16),
        scratch_shapes=[pltpu.VMEM((n_maps, tile, 1), F32), pltpu.VMEM((n_maps, tile, 1), F32),
                        pltpu.VMEM((n_maps, tile, dv), F32)],
        compiler_params=_params(("parallel", "parallel", "arbitrary")),
        name="attention_%dmap" % n_maps,
    )(q, k, v, *extra)


def _conv_kernel(u_ref, w_ref, cb_ref, g_ref, b_ref, o_ref, ext):
    ts = u_ref.shape[0]
    si = pl.program_id(1)

    @pl.when(si == 0)
    def _():
        ext[0:CONV_HALO, :] = jnp.zeros((CONV_HALO, CONV_CHANNELS), F32)

    @pl.when(si > 0)
    def _():
        ext[0:CONV_HALO, :] = ext[ts:ts + CONV_HALO, :]

    ext[CONV_HALO:CONV_HALO + ts, :] = u_ref[...]
    first = CONV_HALO - (CONV_WIDTH - 1)
    for c in range(ts // CONV_CHUNK):
        r0 = c * CONV_CHUNK
        acc = jnp.zeros((CONV_CHUNK, CONV_CHANNELS), F32)
        for j in range(CONV_WIDTH):
            acc = acc + w_ref[j:j + 1, :] * ext[r0 + first + j:r0 + first + j + CONV_CHUNK, :]
        y = _layer_norm(acc + cb_ref[...], g_ref[...], b_ref[...])
        o_ref[r0:r0 + CONV_CHUNK, :] = (y * jax.nn.sigmoid(y)).astype(o_ref.dtype)


def _conv_module(u2d, conv_w, conv_b, ln_g, ln_b, batch, seq):
    ts = CONV_TILE
    n_s = seq // ts
    vec = pl.BlockSpec((1, CONV_CHANNELS), lambda bi, si: (0, 0))
    return pl.pallas_call(
        _conv_kernel,
        grid=(batch, n_s),
        in_specs=[pl.BlockSpec((ts, CONV_CHANNELS), lambda bi, si: (bi * n_s + si, 0)),
                  pl.BlockSpec(conv_w.shape, lambda bi, si: (0, 0)), vec, vec, vec],
        out_specs=pl.BlockSpec((ts, CONV_CHANNELS), lambda bi, si: (bi * n_s + si, 0)),
        out_shape=jax.ShapeDtypeStruct((batch * seq, CONV_CHANNELS), BF16),
        scratch_shapes=[pltpu.VMEM((ts + CONV_HALO, CONV_CHANNELS), F32)],
        compiler_params=_params(("arbitrary", "arbitrary")),
        name="conv_module",
    )(u2d, conv_w, conv_b, ln_g, ln_b)


def _out_ln_kernel(*refs, n_in):
    a_refs = refs[:n_in]
    w_refs = refs[n_in:2 * n_in]
    x_ref, g_ref, b_ref, o_ref = refs[2 * n_in:]
    mix = jnp.dot(a_refs[0][...], w_refs[0][...], preferred_element_type=F32)
    for a_ref, w_ref in zip(a_refs[1:], w_refs[1:]):
        mix = mix + jnp.dot(a_ref[...], w_ref[...], preferred_element_type=F32)
    o_ref[...] = _layer_norm(DN_ALPHA * x_ref[...] + mix, g_ref[...], b_ref[...])


def _out_ln(acts, weights, x2d, g, b):
    t = x2d.shape[0]
    tm = TOKEN_TILE
    vec = pl.BlockSpec((1, D_MODEL), lambda i: (0, 0))
    return pl.pallas_call(
        functools.partial(_out_ln_kernel, n_in=len(acts)),
        grid=(t // tm,),
        in_specs=([pl.BlockSpec((tm, a.shape[1]), lambda i: (i, 0)) for a in acts]
                  + [pl.BlockSpec(w.shape, lambda i: (0, 0)) for w in weights]
                  + [pl.BlockSpec((tm, D_MODEL), lambda i: (i, 0)), vec, vec]),
        out_specs=pl.BlockSpec((tm, D_MODEL), lambda i: (i, 0)),
        out_shape=jax.ShapeDtypeStruct((t, D_MODEL), F32),
        compiler_params=_params(("parallel",)),
        name="out_proj_ln",
    )(*acts, *weights, x2d, g, b)


def _router_kernel(x_ref, rw_ref, rb_ref, tri_ref, idx_ref, gate_ref, rank_ref, cnt_ref):
    tm = x_ref.shape[0]

    @pl.when(pl.program_id(0) == 0)
    def _():
        cnt_ref[...] = jnp.zeros(cnt_ref.shape, F32)

    logits = lax.dot_general(rw_ref[...], x_ref[...], (((1,), (1,)), ((), ())),
                             precision=lax.Precision.HIGHEST,
                             preferred_element_type=F32) + rb_ref[...]
    e_iota = lax.broadcasted_iota(jnp.int32, (N_EXPERTS, tm), 0)
    vals, sels = [], []
    work = logits
    for k in range(TOP_K):
        top = jnp.max(work, axis=0, keepdims=True)
        idx = jnp.min(jnp.where(work == top, e_iota, N_EXPERTS), axis=0, keepdims=True)
        sel = e_iota == idx
        idx_ref[k:k + 1, :] = idx
        vals.append(top)
        sels.append(sel)
        work = jnp.where(sel, -jnp.inf, work)
    exps = [jnp.exp(v - vals[0]) for v in vals]
    denom = exps[0] + exps[1] + exps[2] + exps[3]
    for k in range(TOP_K):
        gate_ref[k:k + 1, :] = exps[k] / denom
    chosen = jnp.where(sels[0] | sels[1] | sels[2] | sels[3], 1.0, 0.0)
    earlier = jnp.dot(chosen.astype(BF16), tri_ref[...], preferred_element_type=F32)
    base = earlier + cnt_ref[:, 0:1]
    for k in range(TOP_K):
        rank = jnp.sum(jnp.where(sels[k], base, 0.0), axis=0, keepdims=True)
        rank_ref[k:k + 1, :] = rank.astype(jnp.int32)
    cnt_ref[...] = cnt_ref[...] + jnp.sum(chosen, axis=1, keepdims=True)


def _router(x2d, rw_t, rb, tri):
    t = x2d.shape[0]
    tm = TOKEN_TILE
    kt = pl.BlockSpec((TOP_K, tm), lambda i: (0, i))
    return pl.pallas_call(
        _router_kernel,
        grid=(t // tm,),
        in_specs=[pl.BlockSpec((tm, D_MODEL), lambda i: (i, 0)),
                  pl.BlockSpec(rw_t.shape, lambda i: (0, 0)),
                  pl.BlockSpec(rb.shape, lambda i: (0, 0)),
                  pl.BlockSpec(tri.shape, lambda i: (0, 0))],
        out_specs=[kt, kt, kt, pl.BlockSpec((N_EXPERTS, LANES), lambda i: (0, 0))],
        out_shape=[jax.ShapeDtypeStruct((TOP_K, t), jnp.int32),
                   jax.ShapeDtypeStruct((TOP_K, t), F32),
                   jax.ShapeDtypeStruct((TOP_K, t), jnp.int32),
                   jax.ShapeDtypeStruct((N_EXPERTS, LANES), F32)],
        compiler_params=_params(("arbitrary",)),
        name="router",
    )(x2d, rw_t, rb, tri)


def _dispatch_kernel(dest_ref, x_ref, xs_in, xs_out, sem):
    del xs_in
    td = x_ref.shape[0]

    def row_copy(t, k):
        return pltpu.make_async_copy(x_ref.at[pl.ds(t, 1)], xs_out.at[pl.ds(dest_ref[k, t], 1)], sem)

    def issue(t8, carry):
        for u in range(8):
            for k in range(TOP_K):
                row_copy(t8 * 8 + u, k).start()
        return carry

    lax.fori_loop(0, td // 8, issue, 0)

    def drain(t8, carry):
        for u in range(8):
            for k in range(TOP_K):
                row_copy(t8 * 8 + u, k).wait()
        return carry

    lax.fori_loop(0, td // 8, drain, 0)


def _dispatch(dest, x2d, n_rows):
    t = x2d.shape[0]
    td = TOKEN_TILE
    xs0 = jnp.zeros((n_rows, D_MODEL), F32)
    return pl.pallas_call(
        _dispatch_kernel,
        grid=(t // td,),
        in_specs=[pl.BlockSpec((TOP_K, td), lambda i: (0, i), memory_space=pltpu.SMEM),
                  pl.BlockSpec((td, D_MODEL), lambda i: (i, 0)),
                  pl.BlockSpec(memory_space=pl.ANY)],
        out_specs=pl.BlockSpec(memory_space=pl.ANY),
        out_shape=jax.ShapeDtypeStruct((n_rows, D_MODEL), F32),
        scratch_shapes=[pltpu.SemaphoreType.DMA(())],
        input_output_aliases={2: 0},
        compiler_params=_params(("arbitrary",)),
        name="moe_dispatch",
    )(dest, x2d, xs0)


def _expert_kernel(blk_e_ref, n_used_ref, xs_ref, wgu_ref, bgu_ref, wdn_ref, bdn_ref, ys_ref):
    del blk_e_ref

    @pl.when(pl.program_id(0) < n_used_ref[0])
    def _():
        h = jnp.dot(xs_ref[...].astype(BF16), wgu_ref[0], preferred_element_type=F32) + bgu_ref[0]
        gate = jnp.minimum(h[:, :D_EXPERT], SWIGLU_LIMIT)
        up = jnp.clip(h[:, D_EXPERT:], -SWIGLU_LIMIT, SWIGLU_LIMIT)
        act = (up + 1.0) * gate * jax.nn.sigmoid(SWIGLU_ALPHA * gate)
        ys_ref[...] = jnp.dot(act.astype(BF16), wdn_ref[0], preferred_element_type=F32) + bdn_ref[0]


def _experts(blk_e, n_used, xs, w_gu, b_gu, w_dn, b_dn):
    n_rows = xs.shape[0]
    rb = EXPERT_ROWS
    rows = pl.BlockSpec((rb, D_MODEL), lambda i, be, nu: (jnp.minimum(i, nu[0] - 1), 0))
    per_e = lambda a: pl.BlockSpec((1,) + a.shape[1:], lambda i, be, nu: (be[i], 0, 0))
    return pl.pallas_call(
        _expert_kernel,
        grid_spec=pltpu.PrefetchScalarGridSpec(
            num_scalar_prefetch=2,
            grid=(n_rows // rb,),
            in_specs=[rows, per_e(w_gu), per_e(b_gu), per_e(w_dn), per_e(b_dn)],
            out_specs=rows),
        out_shape=jax.ShapeDtypeStruct((n_rows, D_MODEL), F32),
        compiler_params=_params(("arbitrary",)),
        name="moe_experts",
    )(blk_e, n_used, xs, w_gu, b_gu, w_dn, b_dn)


def _combine_kernel(dest_ref, gate_ref, x_ref, ys_ref, g_ref, b_ref, o_ref, buf, sem):
    tc = x_ref.shape[0]

    def row_copy(t, k):
        return pltpu.make_async_copy(ys_ref.at[pl.ds(dest_ref[k, t], 1)], buf.at[k, pl.ds(t, 1)], sem)

    def issue(t8, carry):
        for u in range(8):
            for k in range(TOP_K):
                row_copy(t8 * 8 + u, k).start()
        return carry

    lax.fori_loop(0, tc // 8, issue, 0)

    def drain(t8, carry):
        for u in range(8):
            for k in range(TOP_K):
                row_copy(t8 * 8 + u, k).wait()
        return carry

    lax.fori_loop(0, tc // 8, drain, 0)
    gates = gate_ref[...]
    moe = gates[:, 0:1] * buf[0]
    for k in range(1, TOP_K):
        moe = moe + gates[:, k:k + 1] * buf[k]
    o_ref[...] = _layer_norm(DN_ALPHA * x_ref[...] + moe, g_ref[...], b_ref[...])


def _combine(dest, gates_tk, x2d, ys, g, b):
    t = x2d.shape[0]
    tc = COMBINE_TILE
    vec = pl.BlockSpec((1, D_MODEL), lambda i: (0, 0))
    return pl.pallas_call(
        _combine_kernel,
        grid=(t // tc,),
        in_specs=[pl.BlockSpec((TOP_K, tc), lambda i: (0, i), memory_space=pltpu.SMEM),
                  pl.BlockSpec((tc, TOP_K), lambda i: (i, 0)),
                  pl.BlockSpec((tc, D_MODEL), lambda i: (i, 0)),
                  pl.BlockSpec(memory_space=pl.ANY), vec, vec],
        out_specs=pl.BlockSpec((tc, D_MODEL), lambda i: (i, 0)),
        out_shape=jax.ShapeDtypeStruct((t, D_MODEL), F32),
        scratch_shapes=[pltpu.VMEM((TOP_K, tc, D_MODEL), F32), pltpu.SemaphoreType.DMA(())],
        compiler_params=_params(("arbitrary",)),
        name="moe_combine_ln",
    )(dest, gates_tk, x2d, ys, g, b)


def _moe_ln(x2d, router_w, router_b, w_gu, b_gu, w_dn, b_dn, ln_g, ln_b, tri):
    t = x2d.shape[0]
    rb = EXPERT_ROWS
    idx, gates, rank, cnt = _router(x2d, router_w.T, router_b.reshape(N_EXPERTS, 1), tri)
    counts = cnt[:, 0].astype(jnp.int32)
    padded = ((counts + rb - 1) // rb) * rb
    pend = jnp.cumsum(padded)
    pstart = pend - padded
    n_blk = (t * TOP_K) // rb + N_EXPERTS
    n_used = pend[-1] // rb
    blk = jnp.minimum(jnp.arange(n_blk, dtype=jnp.int32), n_used - 1)
    blk_e = jnp.sum((blk[:, None] * rb >= pend[None, :]).astype(jnp.int32), axis=1)
    blk_e = jnp.minimum(blk_e, N_EXPERTS - 1).astype(jnp.int32)
    e_ids = jnp.arange(N_EXPERTS, dtype=jnp.int32)[:, None, None]
    dest = rank + jnp.sum(jnp.where(idx[None] == e_ids, pstart[:, None, None], 0), axis=0)
    dest = dest.astype(jnp.int32)

    xs = _dispatch(dest, x2d, n_blk * rb)
    ys = _experts(blk_e, n_used.reshape(1).astype(jnp.int32), xs,
                  w_gu.astype(BF16), b_gu.reshape(N_EXPERTS, 1, -1),
                  w_dn.astype(BF16), b_dn.reshape(N_EXPERTS, 1, -1))
    return _combine(dest, gates.T, x2d, ys, ln_g.reshape(1, -1), ln_b.reshape(1, -1))


def _spread_rope_cols(w):
    half = QK_ROPE_DIM // 2
    z = jnp.zeros(w.shape[:-1] + (LANES // 2 - half,), w.dtype)
    return jnp.concatenate([w[..., :half], z, w[..., half:], z], axis=-1)


def _rope_tables(seq, dim, spread):
    inv_freq = 1.0 / (ROPE_THETA ** (jnp.arange(0, dim, 2, dtype=F32) / dim))
    ang = jnp.arange(seq, dtype=F32)[:, None] * inv_freq[None, :]
    cos, sin = jnp.cos(ang), jnp.sin(ang)
    if spread:
        z = jnp.zeros((seq, LANES // 2 - dim // 2), F32)
        return (jnp.concatenate([cos, z, cos, z], axis=1), jnp.concatenate([-sin, z, sin, z], axis=1))
    return jnp.concatenate([cos, cos], axis=1), jnp.concatenate([-sin, sin], axis=1)


def kernel(x, l0_w_in, l0_q_norm_g, l0_w_q_up, l0_kv_norm_g, l0_w_kv_up, l0_conv_w, l0_conv_b, l0_conv_ln_g, l0_conv_ln_b, l0_w_o, l0_ln1_g, l0_ln1_b, l0_router_w, l0_router_b, l0_w_gu, l0_b_gu, l0_w_dn, l0_b_dn, l0_ln2_g, l0_ln2_b, l1_w_qkv, l1_lambda_q1, l1_lambda_k1, l1_lambda_q2, l1_lambda_k2, l1_subln_g, l1_w_o, l1_ln1_g, l1_ln1_b, l1_router_w, l1_router_b, l1_w_gu, l1_b_gu, l1_w_dn, l1_b_dn, l1_ln2_g, l1_ln2_b):
    b, s, d = x.shape
    t = b * s
    x2d = x.reshape(t, d)
    row = lambda a: a.reshape(1, -1)
    tri = (jnp.arange(TOKEN_TILE)[:, None] < jnp.arange(TOKEN_TILE)[None, :]).astype(BF16)

    o1 = Q_LORA_RANK
    o2 = o1 + KV_LORA_RANK
    o3 = o2 + QK_ROPE_DIM
    w_in = jnp.concatenate([l0_w_in[:, :o2], _spread_rope_cols(l0_w_in[:, o2:o3]), l0_w_in[:, o3:]],
                           axis=1).astype(BF16)
    wq = l0_w_q_up.reshape(Q_LORA_RANK, MLA_HEADS, QK_NOPE_DIM + QK_ROPE_DIM)
    wq = jnp.concatenate([wq[..., :QK_NOPE_DIM], _spread_rope_cols(wq[..., QK_NOPE_DIM:])], axis=-1)
    wq = wq.reshape(Q_LORA_RANK, MLA_HEADS * 2 * LANES).astype(BF16)
    wkv = l0_w_kv_up.reshape(KV_LORA_RANK, MLA_HEADS, QK_NOPE_DIM + V_HEAD_DIM)
    wkv = jnp.concatenate([wkv[..., :QK_NOPE_DIM].reshape(KV_LORA_RANK, -1),
                           wkv[..., QK_NOPE_DIM:].reshape(KV_LORA_RANK, -1)], axis=1).astype(BF16)
    cos0, sin0 = _rope_tables(s, QK_ROPE_DIM, spread=True)
    q, k, v, u = _l0_proj(x2d, w_in, row(l0_q_norm_g), wq, row(l0_kv_norm_g), wkv, cos0, sin0, s)
    hw = MLA_HEADS * 2 * LANES
    attn = _attention(q.reshape(b, s, hw), k.reshape(b, s, hw), v.reshape(b, s, -1),
                      MLA_HEADS, 1, 2 * LANES, V_HEAD_DIM)
    conv_w = jnp.concatenate([l0_conv_w, jnp.zeros((1, CONV_CHANNELS), F32)], axis=0)
    uc = _conv_module(u, conv_w, row(l0_conv_b), row(l0_conv_ln_g), row(l0_conv_ln_b), b, s)
    n_attn = MLA_HEADS * V_HEAD_DIM
    w_o = l0_w_o.astype(BF16)
    x2d = _out_ln([attn.reshape(t, n_attn), uc], [w_o[:n_attn], w_o[n_attn:]], x2d,
                  row(l0_ln1_g), row(l0_ln1_b))
    x2d = _moe_ln(x2d, l0_router_w, l0_router_b, l0_w_gu, l0_b_gu, l0_w_dn, l0_b_dn,
                  l0_ln2_g, l0_ln2_b, tri)

    lambda_init = 0.8 - 0.6 * math.exp(-0.3 * 1)
    cos1, sin1 = _rope_tables(s, DIFF_HEAD_DIM, spread=False)
    q, k, v = _l1_proj(x2d, l1_w_qkv.astype(BF16), cos1, sin1, s)
    lam_in = jnp.stack([l1_lambda_q1, l1_lambda_k1, l1_lambda_q2, l1_lambda_k2]).astype(F32)
    attn = _attention(q.reshape(b, s, d), k.reshape(b, s, d), v.reshape(b, s, d),
                      DIFF_HEADS, 2, DIFF_HEAD_DIM, 2 * DIFF_HEAD_DIM,
                      extra=(lam_in, row(l1_subln_g)), lambda_init=lambda_init)
    x2d = _out_ln([attn.reshape(t, d)], [l1_w_o.astype(BF16)], x2d, row(l1_ln1_g), row(l1_ln1_b))
    x2d = _moe_ln(x2d, l1_router_w, l1_router_b, l1_w_gu, l1_b_gu, l1_w_dn, l1_b_dn,
                  l1_ln2_g, l1_ln2_b, tri)
    return x2d.reshape(b, s, d)
```

```python
import functools
import math

import jax
import jax.numpy as jnp
from jax import lax
from jax.experimental import pallas as pl
from jax.experimental.pallas import tpu as pltpu

F32 = jnp.float32
BF16 = jnp.bfloat16

D_MODEL = 1024
DEPTH = 2
MLA_HEADS = 4
QK_NOPE_DIM = 128
QK_ROPE_DIM = 64
V_HEAD_DIM = 128
Q_LORA_RANK = 384
KV_LORA_RANK = 256
CONV_CHANNELS = D_MODEL - MLA_HEADS * V_HEAD_DIM
CONV_WIDTH = 31
DIFF_HEAD_DIM = 128
DIFF_HEADS = D_MODEL // (2 * DIFF_HEAD_DIM)
N_EXPERTS = 32
TOP_K = 4
D_EXPERT = D_MODEL
SWIGLU_LIMIT = 7.0
SWIGLU_ALPHA = 1.702
ROPE_THETA = 10000.0
DN_ALPHA = (2 * DEPTH) ** 0.25
LN_EPS = 1e-5
RMS_EPS = 1e-6
MASK_VALUE = -1e30

LANES = 128
ROW_TILE = 8
TOKEN_TILE = 512
ATTN_Q_TILE = 1024
ATTN_K_TILE = 512
ATTN_CHAINS = 4
CONV_TILE = 512
CONV_HALO = 32
CONV_CHUNK = 64
EXPERT_ROWS = 512
DISPATCH_TILE = 1024
COMBINE_TILE = 256
CAST_CHUNK = 128
VMEM_LIMIT = 56 * 1024 * 1024


def _params(sem, vmem=VMEM_LIMIT):
    return pltpu.CompilerParams(dimension_semantics=sem, vmem_limit_bytes=vmem)


def _layer_norm(r, g, b):
    mu = jnp.mean(r, axis=-1, keepdims=True)
    d = r - mu
    var = jnp.mean(d * d, axis=-1, keepdims=True)
    return d * lax.rsqrt(var + LN_EPS) * g + b


def _rms_norm(x, g):
    return x * lax.rsqrt(jnp.mean(x * x, axis=-1, keepdims=True) + RMS_EPS) * g


def _rope(x, cos, sin):
    return x * cos + pltpu.roll(x, 64, 1) * sin


def _repeat_lanes(x, n):
    return x if n == 1 else jnp.concatenate([x] * n, axis=1)


def _load_rows(ref, n_rows):
    return jnp.concatenate([ref[pl.ds(j, n_rows, stride=ROW_TILE), :] for j in range(ROW_TILE)], axis=1)


def _store_rows(ref, val):
    n_rows = val.shape[0]
    for j in range(ROW_TILE):
        ref[pl.ds(j, n_rows, stride=ROW_TILE), :] = val[:, j * LANES:(j + 1) * LANES]


def _l0_proj_kernel(x_ref, win_ref, qg_ref, wq_ref, kvg_ref, wkv_ref, cos_ref, sin_ref,
                    q_ref, k_ref, v_ref, u_ref):
    tm = x_ref.shape[0]
    xb = x_ref[...].astype(BF16)
    proj = jnp.dot(xb, win_ref[...], preferred_element_type=F32)
    o1 = Q_LORA_RANK
    o2 = o1 + KV_LORA_RANK
    o3 = o2 + LANES
    o4 = o3 + CONV_CHANNELS
    u_ref[...] = proj[:, o3:o4] * jax.nn.sigmoid(proj[:, o4:])
    cos = cos_ref[...]
    sin = sin_ref[...]
    scale = (QK_NOPE_DIM + QK_ROPE_DIM) ** -0.5
    qn = _rms_norm(proj[:, :o1], qg_ref[...])
    qup = jnp.dot(qn.astype(BF16), wq_ref[...], preferred_element_type=F32)
    kvn = _rms_norm(proj[:, o1:o2], kvg_ref[...])
    kvup = jnp.dot(kvn.astype(BF16), wkv_ref[...], preferred_element_type=F32)
    k_rope = _rope(proj[:, o2:o3], cos, sin).astype(BF16)
    ones_col = jnp.where(lax.broadcasted_iota(jnp.int32, (tm, LANES), 1) == 0, 1.0, 0.0).astype(BF16)
    for h in range(MLA_HEADS):
        c = 2 * LANES * h
        q_ref[:, c:c + LANES] = (qup[:, c:c + LANES] * scale).astype(BF16)
        q_rope = _rope(qup[:, c + LANES:c + 2 * LANES], cos, sin)
        q_ref[:, c + LANES:c + 2 * LANES] = (q_rope * scale).astype(BF16)
        k_ref[:, c:c + LANES] = kvup[:, LANES * h:LANES * (h + 1)].astype(BF16)
        k_ref[:, c + LANES:c + 2 * LANES] = k_rope
        vh = MLA_HEADS * LANES + LANES * h
        v_ref[:, c:c + LANES] = kvup[:, vh:vh + LANES].astype(BF16)
        v_ref[:, c + LANES:c + 2 * LANES] = ones_col


def _l0_proj(x2d, w_in, qg, wq, kvg, wkv, cos, sin, seq):
    t = x2d.shape[0]
    tm = TOKEN_TILE
    n_pos = seq // tm
    full = lambda a: pl.BlockSpec(a.shape, lambda i: (0,) * a.ndim)
    row = lambda w: pl.BlockSpec((tm, w), lambda i: (i, 0))
    pos = pl.BlockSpec((tm, LANES), lambda i: (i % n_pos, 0))
    hw = MLA_HEADS * 2 * LANES
    wide = jax.ShapeDtypeStruct((t, hw), BF16)
    return pl.pallas_call(
        _l0_proj_kernel,
        grid=(t // tm,),
        in_specs=[row(D_MODEL), full(w_in), full(qg), full(wq), full(kvg), full(wkv), pos, pos],
        out_specs=[row(hw), row(hw), row(hw), row(CONV_CHANNELS)],
        out_shape=[wide, wide, wide, jax.ShapeDtypeStruct((t, CONV_CHANNELS), F32)],
        compiler_params=_params(("parallel",)),
        name="l0_proj",
    )(x2d, w_in, qg, wq, kvg, wkv, cos, sin)


def _l1_proj_kernel(x_ref, w_ref, cos_ref, sin_ref, q_ref, k_ref, v_ref):
    xb = x_ref[...].astype(BF16)
    cos = cos_ref[...]
    sin = sin_ref[...]
    scale = DIFF_HEAD_DIM ** -0.5
    qk_w = DIFF_HEADS * 2 * DIFF_HEAD_DIM
    q = jnp.dot(xb, w_ref[:, :qk_w], preferred_element_type=F32)
    for j in range(qk_w // LANES):
        c = j * LANES
        q_ref[:, c:c + LANES] = (_rope(q[:, c:c + LANES], cos, sin) * scale).astype(BF16)
    k = jnp.dot(xb, w_ref[:, qk_w:2 * qk_w], preferred_element_type=F32)
    for j in range(qk_w // LANES):
        c = j * LANES
        k_ref[:, c:c + LANES] = _rope(k[:, c:c + LANES], cos, sin).astype(BF16)
    v_ref[...] = jnp.dot(xb, w_ref[:, 2 * qk_w:], preferred_element_type=F32).astype(BF16)


def _l1_proj(x2d, w_qkv, cos, sin, seq):
    t = x2d.shape[0]
    tm = TOKEN_TILE
    n_pos = seq // tm
    row = lambda w: pl.BlockSpec((tm, w), lambda i: (i, 0))
    pos = pl.BlockSpec((tm, LANES), lambda i: (i % n_pos, 0))
    out = jax.ShapeDtypeStruct((t, D_MODEL), BF16)
    return pl.pallas_call(
        _l1_proj_kernel,
        grid=(t // tm,),
        in_specs=[row(D_MODEL), pl.BlockSpec(w_qkv.shape, lambda i: (0, 0)), pos, pos],
        out_specs=[row(D_MODEL)] * 3,
        out_shape=[out, out, out],
        compiler_params=_params(("parallel",)),
        name="l1_proj",
    )(x2d, w_qkv, cos, sin)


def _attn_kernel(*refs, n_maps, dk, tq, tk, n_chains, sum_col, lambda_init):
    if n_maps == 2:
        q_ref, k_ref, v_ref, lam_ref, g_ref, o_ref, m_sc, l_sc, acc_sc = refs
    else:
        q_ref, k_ref, v_ref, o_ref, m_sc, l_sc, acc_sc = refs
    qi = pl.program_id(2)
    rs = tq // n_chains
    dv = v_ref.shape[2]
    m_sc[...] = jnp.full(m_sc.shape, MASK_VALUE, F32)
    l_sc[...] = jnp.zeros(l_sc.shape, F32)
    acc_sc[...] = jnp.zeros(acc_sc.shape, F32)

    def chain(m, r, k, v, mask):
        rows = slice(r * rs, (r + 1) * rs)
        q = q_ref[0, rows, m * dk:(m + 1) * dk]
        s = lax.dot_general(q, k, (((1,), (1,)), ((), ())), preferred_element_type=F32)
        if mask is not None:
            s = jnp.where(mask, s, MASK_VALUE)
        m_prev = m_sc[m, rows, :]
        m_new = jnp.maximum(m_prev, jnp.max(s, axis=1, keepdims=True))
        p = jnp.exp(s - _repeat_lanes(m_new, s.shape[1] // LANES))
        a = jnp.exp(m_prev - m_new)
        if sum_col is None:
            l_sc[m, rows, :] = a * l_sc[m, rows, :] + jnp.sum(p, axis=1, keepdims=True)
        acc_sc[m, rows, :] = (_repeat_lanes(a, dv // LANES) * acc_sc[m, rows, :]
                              + jnp.dot(p.astype(BF16), v, preferred_element_type=F32))
        m_sc[m, rows, :] = m_new

    def off_diagonal(j, carry):
        start = pl.multiple_of(j * tk, tk)
        v = v_ref[0, pl.ds(start, tk), :]
        for m in range(n_maps):
            k = k_ref[0, pl.ds(start, tk), m * dk:(m + 1) * dk]
            for r in range(n_chains):
                chain(m, r, k, v, None)
        return carry

    lax.fori_loop(0, qi * (tq // tk), off_diagonal, 0)
    base = pl.multiple_of(qi * tq, tq)
    for r in range(n_chains):
        nk = (r + 1) * rs
        v = v_ref[0, pl.ds(base, nk), :]
        row = lax.broadcasted_iota(jnp.int32, (rs, nk), 0) + r * rs
        col = lax.broadcasted_iota(jnp.int32, (rs, nk), 1)
        mask = col <= row
        for m in range(n_maps):
            k = k_ref[0, pl.ds(base, nk), m * dk:(m + 1) * dk]
            chain(m, r, k, v, mask)

    if n_maps == 1:
        acc = acc_sc[0]
        o_ref[0] = (acc[:, :sum_col] / acc[:, sum_col:sum_col + 1]).astype(o_ref.dtype)
    else:
        lam_in = lam_ref[...]
        lam = (jnp.exp(jnp.sum(lam_in[0:1] * lam_in[1:2], axis=1, keepdims=True))
               - jnp.exp(jnp.sum(lam_in[2:3] * lam_in[3:4], axis=1, keepdims=True)) + lambda_init)
        a = acc_sc[0] / l_sc[0][:, 0:1] - lam * (acc_sc[1] / l_sc[1][:, 0:1])
        o_ref[0] = (_rms_norm(a, g_ref[...]) * (1.0 - lambda_init)).astype(o_ref.dtype)


def _attention(q, k, v, n_heads, n_maps, dk, dv_in, dv_out, extra=(), lambda_init=0.0):
    b, s, _ = q.shape
    tq, tk = ATTN_Q_TILE, ATTN_K_TILE
    qw = n_maps * dk
    sum_col = dv_out if dv_in > dv_out else None
    extra_specs = [pl.BlockSpec(e.shape, lambda bi, h, qi: (0, 0)) for e in extra]
    kern = functools.partial(_attn_kernel, n_maps=n_maps, dk=dk, tq=tq, tk=tk, n_chains=ATTN_CHAINS,
                             sum_col=sum_col, lambda_init=lambda_init)
    return pl.pallas_call(
        kern,
        grid=(b, n_heads, s // tq),
        in_specs=[pl.BlockSpec((1, tq, qw), lambda bi, h, qi: (bi, qi, h)),
                  pl.BlockSpec((1, s, qw), lambda bi, h, qi: (bi, 0, h)),
                  pl.BlockSpec((1, s, dv_in), lambda bi, h, qi: (bi, 0, h))] + extra_specs,
        out_specs=pl.BlockSpec((1, tq, dv_out), lambda bi, h, qi: (bi, qi, h)),
        out_shape=jax.ShapeDtypeStruct((b, s, n_heads * dv_out), BF16),
        scratch_shapes=[pltpu.VMEM((n_maps, tq, LANES), F32), pltpu.VMEM((n_maps, tq, LANES), F32),
                        pltpu.VMEM((n_maps, tq, dv_in), F32)],
        compiler_params=_params(("parallel", "parallel", "arbitrary")),
        name="attention_%dmap" % n_maps,
    )(q, k, v, *extra)


def _conv_kernel(u_ref, w_ref, cb_ref, g_ref, b_ref, o_ref, ext, win_sc):
    ts = u_ref.shape[0]
    si = pl.program_id(1)

    @pl.when(si == 0)
    def _():
        ext[0:CONV_HALO, :] = jnp.zeros((CONV_HALO, CONV_CHANNELS), F32)

    @pl.when(si > 0)
    def _():
        ext[0:CONV_HALO, :] = ext[ts:ts + CONV_HALO, :]

    ext[CONV_HALO:CONV_HALO + ts, :] = u_ref[...]
    first = CONV_HALO - (CONV_WIDTH - 1)
    for c in range(ts // CONV_CHUNK):
        r0 = c * CONV_CHUNK
        acc = jnp.zeros((CONV_CHUNK, CONV_CHANNELS), F32)
        for shift in range(ROW_TILE):
            offs = [first + j - shift for j in range(CONV_WIDTH) if (first + j) % ROW_TILE == shift]
            n_win = max(offs) + CONV_CHUNK
            win = win_sc.at[(c * ROW_TILE + shift) % 2]
            win[0:n_win, :] = ext[r0 + shift:r0 + shift + n_win, :]
            for off in offs:
                j = off + shift - first
                acc = acc + w_ref[j:j + 1, :] * win[off:off + CONV_CHUNK, :]
        y = _layer_norm(acc + cb_ref[...], g_ref[...], b_ref[...])
        o_ref[r0:r0 + CONV_CHUNK, :] = (y * jax.nn.sigmoid(y)).astype(o_ref.dtype)


def _conv_module(u2d, conv_w, conv_b, ln_g, ln_b, batch, seq):
    ts = CONV_TILE
    n_s = seq // ts
    vec = pl.BlockSpec((1, CONV_CHANNELS), lambda bi, si: (0, 0))
    return pl.pallas_call(
        _conv_kernel,
        grid=(batch, n_s),
        in_specs=[pl.BlockSpec((ts, CONV_CHANNELS), lambda bi, si: (bi * n_s + si, 0)),
                  pl.BlockSpec(conv_w.shape, lambda bi, si: (0, 0)), vec, vec, vec],
        out_specs=pl.BlockSpec((ts, CONV_CHANNELS), lambda bi, si: (bi * n_s + si, 0)),
        out_shape=jax.ShapeDtypeStruct((batch * seq, CONV_CHANNELS), BF16),
        scratch_shapes=[pltpu.VMEM((ts + CONV_HALO, CONV_CHANNELS), F32),
                        pltpu.VMEM((2, CONV_CHUNK + CONV_HALO, CONV_CHANNELS), F32)],
        compiler_params=_params(("arbitrary", "arbitrary")),
        name="conv_module",
    )(u2d, conv_w, conv_b, ln_g, ln_b)


def _out_ln_kernel(*refs, n_in):
    a_refs = refs[:n_in]
    w_refs = refs[n_in:2 * n_in]
    x_ref, g_ref, b_ref, o_ref, ot_ref = refs[2 * n_in:]
    mix = jnp.dot(a_refs[0][...], w_refs[0][...], preferred_element_type=F32)
    for a_ref, w_ref in zip(a_refs[1:], w_refs[1:]):
        mix = mix + jnp.dot(a_ref[...], w_ref[...], preferred_element_type=F32)
    y = _layer_norm(DN_ALPHA * x_ref[...] + mix, g_ref[...], b_ref[...])
    o_ref[...] = y
    _store_rows(ot_ref, y)


def _out_ln(acts, weights, x2d, g, b):
    t = x2d.shape[0]
    tm = TOKEN_TILE
    vec = pl.BlockSpec((1, D_MODEL), lambda i: (0, 0))
    return pl.pallas_call(
        functools.partial(_out_ln_kernel, n_in=len(acts)),
        grid=(t // tm,),
        in_specs=([pl.BlockSpec((tm, a.shape[1]), lambda i: (i, 0)) for a in acts]
                  + [pl.BlockSpec(w.shape, lambda i: (0, 0)) for w in weights]
                  + [pl.BlockSpec((tm, D_MODEL), lambda i: (i, 0)), vec, vec]),
        out_specs=[pl.BlockSpec((tm, D_MODEL), lambda i: (i, 0)),
                   pl.BlockSpec((tm * ROW_TILE, LANES), lambda i: (i, 0))],
        out_shape=[jax.ShapeDtypeStruct((t, D_MODEL), F32),
                   jax.ShapeDtypeStruct((t * ROW_TILE, LANES), F32)],
        compiler_params=_params(("parallel",)),
        name="out_proj_ln",
    )(*acts, *weights, x2d, g, b)


def _router_kernel(x_ref, rw_ref, rb_ref, tri_ref, idx_ref, gate_ref, rank_ref, cnt_ref):
    tm = x_ref.shape[0]

    @pl.when(pl.program_id(0) == 0)
    def _():
        cnt_ref[...] = jnp.zeros(cnt_ref.shape, F32)

    logits = lax.dot_general(rw_ref[...], x_ref[...], (((1,), (1,)), ((), ())),
                             precision=lax.Precision.HIGHEST,
                             preferred_element_type=F32) + rb_ref[...]
    e_iota = lax.broadcasted_iota(jnp.int32, (N_EXPERTS, tm), 0)
    vals, sels = [], []
    work = logits
    for k in range(TOP_K):
        top = jnp.max(work, axis=0, keepdims=True)
        idx = jnp.min(jnp.where(work == top, e_iota, N_EXPERTS), axis=0, keepdims=True)
        sel = e_iota == idx
        idx_ref[k:k + 1, :] = idx
        vals.append(top)
        sels.append(sel)
        work = jnp.where(sel, -jnp.inf, work)
    exps = [jnp.exp(v - vals[0]) for v in vals]
    denom = exps[0] + exps[1] + exps[2] + exps[3]
    for k in range(TOP_K):
        gate_ref[k:k + 1, :] = exps[k] / denom
    chosen = jnp.where(sels[0] | sels[1] | sels[2] | sels[3], 1.0, 0.0)
    earlier = jnp.dot(chosen.astype(BF16), tri_ref[...], preferred_element_type=F32)
    base = earlier + cnt_ref[:, 0:1]
    for k in range(TOP_K):
        rank = jnp.sum(jnp.where(sels[k], base, 0.0), axis=0, keepdims=True)
        rank_ref[k:k + 1, :] = rank.astype(jnp.int32)
    cnt_ref[...] = cnt_ref[...] + jnp.sum(chosen, axis=1, keepdims=True)


def _router(x2d, rw_t, rb, tri):
    t = x2d.shape[0]
    tm = TOKEN_TILE
    kt = pl.BlockSpec((TOP_K, tm), lambda i: (0, i))
    return pl.pallas_call(
        _router_kernel,
        grid=(t // tm,),
        in_specs=[pl.BlockSpec((tm, D_MODEL), lambda i: (i, 0)),
                  pl.BlockSpec(rw_t.shape, lambda i: (0, 0)),
                  pl.BlockSpec(rb.shape, lambda i: (0, 0)),
                  pl.BlockSpec(tri.shape, lambda i: (0, 0))],
        out_specs=[kt, kt, kt, pl.BlockSpec((N_EXPERTS, LANES), lambda i: (0, 0))],
        out_shape=[jax.ShapeDtypeStruct((TOP_K, t), jnp.int32),
                   jax.ShapeDtypeStruct((TOP_K, t), F32),
                   jax.ShapeDtypeStruct((TOP_K, t), jnp.int32),
                   jax.ShapeDtypeStruct((N_EXPERTS, LANES), F32)],
        compiler_params=_params(("arbitrary",)),
        name="router",
    )(x2d, rw_t, rb, tri)


def _dispatch_kernel(pend_ref, padded_ref, dest_ref, xt_ref, xs_ref, zeros, zsem, sem):
    td = dest_ref.shape[1]
    zrows = zeros.shape[0]

    @pl.when(pl.program_id(0) == 0)
    def _():
        zeros[...] = jnp.zeros(zeros.shape, F32)

        def zero_copy(e):
            start = pl.multiple_of(pend_ref[e] * ROW_TILE - zrows, zrows)
            return pltpu.make_async_copy(zeros, xs_ref.at[pl.ds(start, zrows)], zsem)

        def start_zero(e, carry):
            @pl.when(padded_ref[e] > 0)
            def _():
                zero_copy(e).start()
            return carry

        def wait_zero(e, carry):
            @pl.when(padded_ref[e] > 0)
            def _():
                zero_copy(e).wait()
            return carry

        lax.fori_loop(0, N_EXPERTS, start_zero, 0)
        lax.fori_loop(0, N_EXPERTS, wait_zero, 0)

    def row_copy(t8, u, k):
        src = pl.multiple_of((t8 * ROW_TILE + u) * ROW_TILE, ROW_TILE)
        dst = pl.multiple_of(dest_ref[k, t8 * ROW_TILE + u], ROW_TILE)
        return pltpu.make_async_copy(xt_ref.at[pl.ds(src, ROW_TILE)], xs_ref.at[pl.ds(dst, ROW_TILE)], sem)

    def issue(t8, carry):
        for u in range(ROW_TILE):
            for k in range(TOP_K):
                row_copy(t8, u, k).start(priority=k % 2)
        return carry

    def drain(t8, carry):
        for u in range(ROW_TILE):
            for k in range(TOP_K):
                row_copy(t8, u, k).wait()
        return carry

    lax.fori_loop(0, td // ROW_TILE, issue, 0)
    lax.fori_loop(0, td // ROW_TILE, drain, 0)


def _dispatch(pend, padded, dest8, xt, n_rows):
    t = dest8.shape[1]
    td = DISPATCH_TILE
    return pl.pallas_call(
        _dispatch_kernel,
        grid_spec=pltpu.PrefetchScalarGridSpec(
            num_scalar_prefetch=2,
            grid=(t // td,),
            in_specs=[pl.BlockSpec((TOP_K, td), lambda i, pe, pa: (0, i), memory_space=pltpu.SMEM),
                      pl.BlockSpec((td * ROW_TILE, LANES), lambda i, pe, pa: (i, 0))],
            out_specs=pl.BlockSpec(memory_space=pl.ANY),
            scratch_shapes=[pltpu.VMEM((EXPERT_ROWS * ROW_TILE, LANES), F32),
                            pltpu.SemaphoreType.DMA(()), pltpu.SemaphoreType.DMA(())]),
        out_shape=jax.ShapeDtypeStruct((n_rows * ROW_TILE, LANES), F32),
        compiler_params=_params(("arbitrary",)),
        name="moe_dispatch",
    )(pend, padded, dest8, xt)


def _expert_kernel(blk_e_ref, n_used_ref, xs_ref, wgu_ref, bgu_ref, wdn_ref, bdn_ref, ys_ref,
                   wgu_bf, wdn_bf):
    i = pl.program_id(0)
    rb = xs_ref.shape[0] // ROW_TILE
    active = i < n_used_ref[0]
    new_expert = (i == 0) | (blk_e_ref[i] != blk_e_ref[jnp.maximum(i - 1, 0)])

    @pl.when(active & new_expert)
    def _():
        def cast(c, carry):
            rows = pl.ds(pl.multiple_of(c * CAST_CHUNK, CAST_CHUNK), CAST_CHUNK)
            wgu_bf[rows, :] = wgu_ref[0, rows, :].astype(BF16)
            wdn_bf[rows, :] = wdn_ref[0, rows, :].astype(BF16)
            return carry

        lax.fori_loop(0, D_MODEL // CAST_CHUNK, cast, 0)

    @pl.when(active)
    def _():
        xb = _load_rows(xs_ref, rb).astype(BF16)
        h = jnp.dot(xb, wgu_bf[...], preferred_element_type=F32) + bgu_ref[0]
        gate = jnp.minimum(h[:, :D_EXPERT], SWIGLU_LIMIT)
        up = jnp.clip(h[:, D_EXPERT:], -SWIGLU_LIMIT, SWIGLU_LIMIT)
        act = (up + 1.0) * gate * jax.nn.sigmoid(SWIGLU_ALPHA * gate)
        y = jnp.dot(act.astype(BF16), wdn_bf[...], preferred_element_type=F32) + bdn_ref[0]
        _store_rows(ys_ref, y)


def _experts(blk_e, n_used, xs, w_gu, b_gu, w_dn, b_dn):
    rb = EXPERT_ROWS
    n_blk = xs.shape[0] // (rb * ROW_TILE)
    rows = pl.BlockSpec((rb * ROW_TILE, LANES), lambda i, be, nu: (jnp.minimum(i, nu[0] - 1), 0))
    per_e = lambda a: pl.BlockSpec((1,) + a.shape[1:], lambda i, be, nu: (be[i], 0, 0))
    return pl.pallas_call(
        _expert_kernel,
        grid_spec=pltpu.PrefetchScalarGridSpec(
            num_scalar_prefetch=2,
            grid=(n_blk,),
            in_specs=[rows, per_e(w_gu), per_e(b_gu), per_e(w_dn), per_e(b_dn)],
            out_specs=rows,
            scratch_shapes=[pltpu.VMEM(w_gu.shape[1:], BF16), pltpu.VMEM(w_dn.shape[1:], BF16)]),
        out_shape=jax.ShapeDtypeStruct(xs.shape, F32),
        compiler_params=_params(("arbitrary",)),
        name="moe_experts",
    )(blk_e, n_used, xs, w_gu, b_gu, w_dn, b_dn)


def _combine_kernel(dest_ref, next_ref, gate_ref, x_ref, ys_ref, g_ref, b_ref, o_ref, buf, sems):
    tc = x_ref.shape[0]
    i = pl.program_id(0)
    slot = i % 2

    def row_copy(d_ref, s, t8, u, k):
        src = pl.multiple_of(d_ref[k, t8 * ROW_TILE + u], ROW_TILE)
        dst = pl.multiple_of((t8 * ROW_TILE + u) * ROW_TILE, ROW_TILE)
        return pltpu.make_async_copy(ys_ref.at[pl.ds(src, ROW_TILE)], buf.at[s, k, pl.ds(dst, ROW_TILE)],
                                     sems.at[s])

    def issue(d_ref, s):
        def body(t8, carry):
            for u in range(ROW_TILE):
                for k in range(TOP_K):
                    row_copy(d_ref, s, t8, u, k).start(priority=k % 2)
            return carry

        lax.fori_loop(0, tc // ROW_TILE, body, 0)

    @pl.when(i == 0)
    def _():
        issue(dest_ref, 0)

    @pl.when(i + 1 < pl.num_programs(0))
    def _():
        issue(next_ref, 1 - slot)

    def drain(t8, carry):
        for u in range(ROW_TILE):
            for k in range(TOP_K):
                row_copy(dest_ref, slot, t8, u, k).wait()
        return carry

    lax.fori_loop(0, tc // ROW_TILE, drain, 0)
    gates = gate_ref[...]
    moe = gates[:, 0:1] * _load_rows(buf.at[slot, 0], tc)
    for k in range(1, TOP_K):
        moe = moe + gates[:, k:k + 1] * _load_rows(buf.at[slot, k], tc)
    o_ref[...] = _layer_norm(DN_ALPHA * x_ref[...] + moe, g_ref[...], b_ref[...])


def _combine(dest8, gates_tk, x2d, ys, g, b):
    t = x2d.shape[0]
    tc = COMBINE_TILE
    n = t // tc
    vec = pl.BlockSpec((1, D_MODEL), lambda i: (0, 0))
    return pl.pallas_call(
        _combine_kernel,
        grid=(n,),
        in_specs=[pl.BlockSpec((TOP_K, tc), lambda i: (0, i), memory_space=pltpu.SMEM),
                  pl.BlockSpec((TOP_K, tc), lambda i: (0, jnp.minimum(i + 1, n - 1)),
                               memory_space=pltpu.SMEM),
                  pl.BlockSpec((tc, TOP_K), lambda i: (i, 0)),
                  pl.BlockSpec((tc, D_MODEL), lambda i: (i, 0)),
                  pl.BlockSpec(memory_space=pl.ANY), vec, vec],
        out_specs=pl.BlockSpec((tc, D_MODEL), lambda i: (i, 0)),
        out_shape=jax.ShapeDtypeStruct((t, D_MODEL), F32),
        scratch_shapes=[pltpu.VMEM((2, TOP_K, tc * ROW_TILE, LANES), F32), pltpu.SemaphoreType.DMA((2,))],
        compiler_params=_params(("arbitrary",)),
        name="moe_combine_ln",
    )(dest8, dest8, gates_tk, x2d, ys, g, b)


def _moe_ln(x2d, xt, router_w, router_b, w_gu, b_gu, w_dn, b_dn, ln_g, ln_b, tri):
    t = x2d.shape[0]
    rb = EXPERT_ROWS
    idx, gates, rank, cnt = _router(x2d, router_w.T, router_b.reshape(N_EXPERTS, 1), tri)
    counts = cnt[:, 0].astype(jnp.int32)
    padded = ((counts + rb - 1) // rb) * rb
    pend = jnp.cumsum(padded).astype(jnp.int32)
    pstart = pend - padded
    n_blk = (t * TOP_K) // rb + N_EXPERTS
    n_used = pend[-1] // rb
    blk = jnp.minimum(jnp.arange(n_blk, dtype=jnp.int32), n_used - 1)
    blk_e = jnp.sum((blk[:, None] * rb >= pend[None, :]).astype(jnp.int32), axis=1)
    blk_e = jnp.minimum(blk_e, N_EXPERTS - 1).astype(jnp.int32)
    e_ids = jnp.arange(N_EXPERTS, dtype=jnp.int32)[:, None, None]
    dest = rank + jnp.sum(jnp.where(idx[None] == e_ids, pstart[:, None, None], 0), axis=0)
    dest8 = (dest * ROW_TILE).astype(jnp.int32)

    xs = _dispatch(pend, padded.astype(jnp.int32), dest8, xt, n_blk * rb)
    ys = _experts(blk_e, n_used.reshape(1).astype(jnp.int32), xs,
                  w_gu, b_gu.reshape(N_EXPERTS, 1, -1), w_dn, b_dn.reshape(N_EXPERTS, 1, -1))
    return _combine(dest8, gates.T, x2d, ys, ln_g.reshape(1, -1), ln_b.reshape(1, -1))


def _spread_rope_cols(w):
    half = QK_ROPE_DIM // 2
    z = jnp.zeros(w.shape[:-1] + (LANES // 2 - half,), w.dtype)
    return jnp.concatenate([w[..., :half], z, w[..., half:], z], axis=-1)


def _rope_tables(seq, dim, spread):
    inv_freq = 1.0 / (ROPE_THETA ** (jnp.arange(0, dim, 2, dtype=F32) / dim))
    ang = jnp.arange(seq, dtype=F32)[:, None] * inv_freq[None, :]
    cos, sin = jnp.cos(ang), jnp.sin(ang)
    if spread:
        z = jnp.zeros((seq, LANES // 2 - dim // 2), F32)
        return (jnp.concatenate([cos, z, cos, z], axis=1), jnp.concatenate([-sin, z, sin, z], axis=1))
    return jnp.concatenate([cos, cos], axis=1), jnp.concatenate([-sin, sin], axis=1)


def kernel(x, l0_w_in, l0_q_norm_g, l0_w_q_up, l0_kv_norm_g, l0_w_kv_up, l0_conv_w, l0_conv_b, l0_conv_ln_g, l0_conv_ln_b, l0_w_o, l0_ln1_g, l0_ln1_b, l0_router_w, l0_router_b, l0_w_gu, l0_b_gu, l0_w_dn, l0_b_dn, l0_ln2_g, l0_ln2_b, l1_w_qkv, l1_lambda_q1, l1_lambda_k1, l1_lambda_q2, l1_lambda_k2, l1_subln_g, l1_w_o, l1_ln1_g, l1_ln1_b, l1_router_w, l1_router_b, l1_w_gu, l1_b_gu, l1_w_dn, l1_b_dn, l1_ln2_g, l1_ln2_b):
    b, s, d = x.shape
    t = b * s
    x2d = x.reshape(t, d)
    row = lambda a: a.reshape(1, -1)
    tri = (jnp.arange(TOKEN_TILE)[:, None] < jnp.arange(TOKEN_TILE)[None, :]).astype(BF16)

    o1 = Q_LORA_RANK
    o2 = o1 + KV_LORA_RANK
    o3 = o2 + QK_ROPE_DIM
    w_in = jnp.concatenate([l0_w_in[:, :o2], _spread_rope_cols(l0_w_in[:, o2:o3]), l0_w_in[:, o3:]],
                           axis=1).astype(BF16)
    wq = l0_w_q_up.reshape(Q_LORA_RANK, MLA_HEADS, QK_NOPE_DIM + QK_ROPE_DIM)
    wq = jnp.concatenate([wq[..., :QK_NOPE_DIM], _spread_rope_cols(wq[..., QK_NOPE_DIM:])], axis=-1)
    wq = wq.reshape(Q_LORA_RANK, MLA_HEADS * 2 * LANES).astype(BF16)
    wkv = l0_w_kv_up.reshape(KV_LORA_RANK, MLA_HEADS, QK_NOPE_DIM + V_HEAD_DIM)
    wkv = jnp.concatenate([wkv[..., :QK_NOPE_DIM].reshape(KV_LORA_RANK, -1),
                           wkv[..., QK_NOPE_DIM:].reshape(KV_LORA_RANK, -1)], axis=1).astype(BF16)
    cos0, sin0 = _rope_tables(s, QK_ROPE_DIM, spread=True)
    q, k, v, u = _l0_proj(x2d, w_in, row(l0_q_norm_g), wq, row(l0_kv_norm_g), wkv, cos0, sin0, s)
    hw = MLA_HEADS * 2 * LANES
    attn = _attention(q.reshape(b, s, hw), k.reshape(b, s, hw), v.reshape(b, s, hw),
                      MLA_HEADS, 1, 2 * LANES, 2 * LANES, V_HEAD_DIM)
    conv_w = jnp.concatenate([l0_conv_w, jnp.zeros((1, CONV_CHANNELS), F32)], axis=0)
    uc = _conv_module(u, conv_w, row(l0_conv_b), row(l0_conv_ln_g), row(l0_conv_ln_b), b, s)
    n_attn = MLA_HEADS * V_HEAD_DIM
    w_o = l0_w_o.astype(BF16)
    x2d, xt = _out_ln([attn.reshape(t, n_attn), uc], [w_o[:n_attn], w_o[n_attn:]], x2d,
                      row(l0_ln1_g), row(l0_ln1_b))
    x2d = _moe_ln(x2d, xt, l0_router_w, l0_router_b, l0_w_gu, l0_b_gu, l0_w_dn, l0_b_dn,
                  l0_ln2_g, l0_ln2_b, tri)

    lambda_init = 0.8 - 0.6 * math.exp(-0.3 * 1)
    cos1, sin1 = _rope_tables(s, DIFF_HEAD_DIM, spread=False)
    q, k, v = _l1_proj(x2d, l1_w_qkv.astype(BF16), cos1, sin1, s)
    lam_in = jnp.stack([l1_lambda_q1, l1_lambda_k1, l1_lambda_q2, l1_lambda_k2]).astype(F32)
    dv = 2 * DIFF_HEAD_DIM
    attn = _attention(q.reshape(b, s, d), k.reshape(b, s, d), v.reshape(b, s, d),
                      DIFF_HEADS, 2, DIFF_HEAD_DIM, dv, dv,
                      extra=(lam_in, row(l1_subln_g)), lambda_init=lambda_init)
    x2d, xt = _out_ln([attn.reshape(t, d)], [l1_w_o.astype(BF16)], x2d, row(l1_ln1_g), row(l1_ln1_b))
    x2d = _moe_ln(x2d, xt, l1_router_w, l1_router_b, l1_w_gu, l1_b_gu, l1_w_dn, l1_b_dn,
                  l1_ln2_g, l1_ln2_b, tri)
    return x2d.reshape(b, s, d)
```

```python
import functools
import math

import jax
import jax.numpy as jnp
from jax import lax
from jax.experimental import pallas as pl
from jax.experimental.pallas import tpu as pltpu

F32 = jnp.float32
BF16 = jnp.bfloat16

D_MODEL = 1024
DEPTH = 2
MLA_HEADS = 4
QK_NOPE_DIM = 128
QK_ROPE_DIM = 64
V_HEAD_DIM = 128
Q_LORA_RANK = 384
KV_LORA_RANK = 256
CONV_CHANNELS = D_MODEL - MLA_HEADS * V_HEAD_DIM
CONV_WIDTH = 31
DIFF_HEAD_DIM = 128
DIFF_HEADS = D_MODEL // (2 * DIFF_HEAD_DIM)
N_EXPERTS = 32
TOP_K = 4
D_EXPERT = D_MODEL
SWIGLU_LIMIT = 7.0
SWIGLU_ALPHA = 1.702
ROPE_THETA = 10000.0
DN_ALPHA = (2 * DEPTH) ** 0.25
LN_EPS = 1e-5
RMS_EPS = 1e-6
MASK_VALUE = -1e30

LANES = 128
ROW_TILE = 8
TOKEN_TILE = 512
ATTN_Q_TILE = 2048
ATTN_K_TILE = 1024
ATTN_CHAINS = 8
CONV_TILE = 512
CONV_HALO = 32
CONV_CHUNK = 64
EXPERT_ROWS = 512
DISPATCH_TILE = 1024
COMBINE_TILE = 256
CAST_CHUNK = 128
VMEM_LIMIT = 56 * 1024 * 1024


def _params(sem, vmem=VMEM_LIMIT):
    return pltpu.CompilerParams(dimension_semantics=sem, vmem_limit_bytes=vmem)


def _layer_norm(r, g, b):
    mu = jnp.mean(r, axis=-1, keepdims=True)
    d = r - mu
    var = jnp.mean(d * d, axis=-1, keepdims=True)
    return d * lax.rsqrt(var + LN_EPS) * g + b


def _rms_norm(x, g):
    return x * lax.rsqrt(jnp.mean(x * x, axis=-1, keepdims=True) + RMS_EPS) * g


def _rope(x, cos, sin):
    return x * cos + pltpu.roll(x, 64, 1) * sin


def _repeat_lanes(x, n):
    return x if n == 1 else jnp.concatenate([x] * n, axis=1)


def _load_rows(ref, n_rows):
    return jnp.concatenate([ref[pl.ds(j, n_rows, stride=ROW_TILE), :] for j in range(ROW_TILE)], axis=1)


def _store_rows(ref, val):
    n_rows = val.shape[0]
    for j in range(ROW_TILE):
        ref[pl.ds(j, n_rows, stride=ROW_TILE), :] = val[:, j * LANES:(j + 1) * LANES]


def _l0_proj_kernel(x_ref, win_ref, qg_ref, wq_ref, kvg_ref, wkv_ref, cos_ref, sin_ref,
                    q_ref, k_ref, v_ref, u_ref):
    tm = x_ref.shape[0]
    xb = x_ref[...].astype(BF16)
    proj = jnp.dot(xb, win_ref[...], preferred_element_type=F32)
    o1 = Q_LORA_RANK
    o2 = o1 + KV_LORA_RANK
    o3 = o2 + LANES
    o4 = o3 + CONV_CHANNELS
    u_ref[...] = proj[:, o3:o4] * jax.nn.sigmoid(proj[:, o4:])
    cos = cos_ref[...]
    sin = sin_ref[...]
    scale = (QK_NOPE_DIM + QK_ROPE_DIM) ** -0.5
    qn = _rms_norm(proj[:, :o1], qg_ref[...])
    qup = jnp.dot(qn.astype(BF16), wq_ref[...], preferred_element_type=F32)
    kvn = _rms_norm(proj[:, o1:o2], kvg_ref[...])
    kvup = jnp.dot(kvn.astype(BF16), wkv_ref[...], preferred_element_type=F32)
    k_rope = _rope(proj[:, o2:o3], cos, sin).astype(BF16)
    ones_col = jnp.where(lax.broadcasted_iota(jnp.int32, (tm, LANES), 1) == 0, 1.0, 0.0).astype(BF16)
    for h in range(MLA_HEADS):
        c = 2 * LANES * h
        q_ref[:, c:c + LANES] = (qup[:, c:c + LANES] * scale).astype(BF16)
        q_rope = _rope(qup[:, c + LANES:c + 2 * LANES], cos, sin)
        q_ref[:, c + LANES:c + 2 * LANES] = (q_rope * scale).astype(BF16)
        k_ref[:, c:c + LANES] = kvup[:, LANES * h:LANES * (h + 1)].astype(BF16)
        k_ref[:, c + LANES:c + 2 * LANES] = k_rope
        vh = MLA_HEADS * LANES + LANES * h
        v_ref[:, c:c + LANES] = kvup[:, vh:vh + LANES].astype(BF16)
        v_ref[:, c + LANES:c + 2 * LANES] = ones_col


def _l0_proj(x2d, w_in, qg, wq, kvg, wkv, cos, sin, seq):
    t = x2d.shape[0]
    tm = TOKEN_TILE
    n_pos = seq // tm
    full = lambda a: pl.BlockSpec(a.shape, lambda i: (0,) * a.ndim)
    row = lambda w: pl.BlockSpec((tm, w), lambda i: (i, 0))
    pos = pl.BlockSpec((tm, LANES), lambda i: (i % n_pos, 0))
    hw = MLA_HEADS * 2 * LANES
    wide = jax.ShapeDtypeStruct((t, hw), BF16)
    return pl.pallas_call(
        _l0_proj_kernel,
        grid=(t // tm,),
        in_specs=[row(D_MODEL), full(w_in), full(qg), full(wq), full(kvg), full(wkv), pos, pos],
        out_specs=[row(hw), row(hw), row(hw), row(CONV_CHANNELS)],
        out_shape=[wide, wide, wide, jax.ShapeDtypeStruct((t, CONV_CHANNELS), F32)],
        compiler_params=_params(("parallel",)),
        name="l0_proj",
    )(x2d, w_in, qg, wq, kvg, wkv, cos, sin)


def _l1_proj_kernel(x_ref, w_ref, cos_ref, sin_ref, q_ref, k_ref, v_ref):
    xb = x_ref[...].astype(BF16)
    cos = cos_ref[...]
    sin = sin_ref[...]
    scale = DIFF_HEAD_DIM ** -0.5
    qk_w = DIFF_HEADS * 2 * DIFF_HEAD_DIM
    q = jnp.dot(xb, w_ref[:, :qk_w], preferred_element_type=F32)
    for j in range(qk_w // LANES):
        c = j * LANES
        q_ref[:, c:c + LANES] = (_rope(q[:, c:c + LANES], cos, sin) * scale).astype(BF16)
    k = jnp.dot(xb, w_ref[:, qk_w:2 * qk_w], preferred_element_type=F32)
    for j in range(qk_w // LANES):
        c = j * LANES
        k_ref[:, c:c + LANES] = _rope(k[:, c:c + LANES], cos, sin).astype(BF16)
    v_ref[...] = jnp.dot(xb, w_ref[:, 2 * qk_w:], preferred_element_type=F32).astype(BF16)


def _l1_proj(x2d, w_qkv, cos, sin, seq):
    t = x2d.shape[0]
    tm = TOKEN_TILE
    n_pos = seq // tm
    row = lambda w: pl.BlockSpec((tm, w), lambda i: (i, 0))
    pos = pl.BlockSpec((tm, LANES), lambda i: (i % n_pos, 0))
    out = jax.ShapeDtypeStruct((t, D_MODEL), BF16)
    return pl.pallas_call(
        _l1_proj_kernel,
        grid=(t // tm,),
        in_specs=[row(D_MODEL), pl.BlockSpec(w_qkv.shape, lambda i: (0, 0)), pos, pos],
        out_specs=[row(D_MODEL)] * 3,
        out_shape=[out, out, out],
        compiler_params=_params(("parallel",)),
        name="l1_proj",
    )(x2d, w_qkv, cos, sin)


def _attn_kernel(*refs, n_maps, dk, tq, tk, n_chains, sum_col, lambda_init):
    if n_maps == 2:
        q_ref, k_ref, v_ref, lam_ref, g_ref, o_ref, m_sc, l_sc, acc_sc = refs
    else:
        q_ref, k_ref, v_ref, o_ref, m_sc, l_sc, acc_sc = refs
    qi = pl.program_id(2)
    rs = tq // n_chains
    dv = v_ref.shape[2]
    m_sc[...] = jnp.full(m_sc.shape, MASK_VALUE, F32)
    l_sc[...] = jnp.zeros(l_sc.shape, F32)
    acc_sc[...] = jnp.zeros(acc_sc.shape, F32)

    def chain(m, r, k, v, mask):
        rows = slice(r * rs, (r + 1) * rs)
        q = q_ref[0, rows, m * dk:(m + 1) * dk]
        s = lax.dot_general(q, k, (((1,), (1,)), ((), ())), preferred_element_type=F32)
        if mask is not None:
            s = jnp.where(mask, s, MASK_VALUE)
        m_prev = m_sc[m, rows, :]
        m_new = jnp.maximum(m_prev, jnp.max(s, axis=1, keepdims=True))
        p = jnp.exp(s - _repeat_lanes(m_new, s.shape[1] // LANES))
        a = jnp.exp(m_prev - m_new)
        if sum_col is None:
            l_sc[m, rows, :] = a * l_sc[m, rows, :] + jnp.sum(p, axis=1, keepdims=True)
        acc_sc[m, rows, :] = (_repeat_lanes(a, dv // LANES) * acc_sc[m, rows, :]
                              + jnp.dot(p.astype(BF16), v, preferred_element_type=F32))
        m_sc[m, rows, :] = m_new

    def off_diagonal(j, carry):
        start = pl.multiple_of(j * tk, tk)
        v = v_ref[0, pl.ds(start, tk), :]
        for m in range(n_maps):
            k = k_ref[0, pl.ds(start, tk), m * dk:(m + 1) * dk]
            for r in range(n_chains):
                chain(m, r, k, v, None)
        return carry

    lax.fori_loop(0, qi * (tq // tk), off_diagonal, 0)
    base = pl.multiple_of(qi * tq, tq)
    for r in range(n_chains):
        nk = (r + 1) * rs
        v = v_ref[0, pl.ds(base, nk), :]
        row = lax.broadcasted_iota(jnp.int32, (rs, nk), 0) + r * rs
        col = lax.broadcasted_iota(jnp.int32, (rs, nk), 1)
        mask = col <= row
        for m in range(n_maps):
            k = k_ref[0, pl.ds(base, nk), m * dk:(m + 1) * dk]
            chain(m, r, k, v, mask)

    if n_maps == 1:
        acc = acc_sc[0]
        o_ref[0] = (acc[:, :sum_col] / acc[:, sum_col:sum_col + 1]).astype(o_ref.dtype)
    else:
        lam_in = lam_ref[...]
        lam = (jnp.exp(jnp.sum(lam_in[0:1] * lam_in[1:2], axis=1, keepdims=True))
               - jnp.exp(jnp.sum(lam_in[2:3] * lam_in[3:4], axis=1, keepdims=True)) + lambda_init)
        a = acc_sc[0] / l_sc[0][:, 0:1] - lam * (acc_sc[1] / l_sc[1][:, 0:1])
        o_ref[0] = (_rms_norm(a, g_ref[...]) * (1.0 - lambda_init)).astype(o_ref.dtype)


def _attention(q, k, v, n_heads, n_maps, dk, dv_in, dv_out, extra=(), lambda_init=0.0):
    b, s, _ = q.shape
    tq, tk = ATTN_Q_TILE, ATTN_K_TILE
    qw = n_maps * dk
    sum_col = dv_out if dv_in > dv_out else None
    extra_specs = [pl.BlockSpec(e.shape, lambda bi, h, qi: (0, 0)) for e in extra]
    kern = functools.partial(_attn_kernel, n_maps=n_maps, dk=dk, tq=tq, tk=tk, n_chains=ATTN_CHAINS,
                             sum_col=sum_col, lambda_init=lambda_init)
    return pl.pallas_call(
        kern,
        grid=(b, n_heads, s // tq),
        in_specs=[pl.BlockSpec((1, tq, qw), lambda bi, h, qi: (bi, qi, h)),
                  pl.BlockSpec((1, s, qw), lambda bi, h, qi: (bi, 0, h)),
                  pl.BlockSpec((1, s, dv_in), lambda bi, h, qi: (bi, 0, h))] + extra_specs,
        out_specs=pl.BlockSpec((1, tq, dv_out), lambda bi, h, qi: (bi, qi, h)),
        out_shape=jax.ShapeDtypeStruct((b, s, n_heads * dv_out), BF16),
        scratch_shapes=[pltpu.VMEM((n_maps, tq, LANES), F32), pltpu.VMEM((n_maps, tq, LANES), F32),
                        pltpu.VMEM((n_maps, tq, dv_in), F32)],
        compiler_params=_params(("parallel", "parallel", "arbitrary")),
        name="attention_%dmap" % n_maps,
    )(q, k, v, *extra)


def _conv_kernel(u_ref, w_ref, cb_ref, g_ref, b_ref, o_ref, ext, win_sc):
    ts = u_ref.shape[0]
    si = pl.program_id(1)

    @pl.when(si == 0)
    def _():
        ext[0:CONV_HALO, :] = jnp.zeros((CONV_HALO, CONV_CHANNELS), F32)

    @pl.when(si > 0)
    def _():
        ext[0:CONV_HALO, :] = ext[ts:ts + CONV_HALO, :]

    ext[CONV_HALO:CONV_HALO + ts, :] = u_ref[...]
    first = CONV_HALO - (CONV_WIDTH - 1)
    for c in range(ts // CONV_CHUNK):
        r0 = c * CONV_CHUNK
        acc = jnp.zeros((CONV_CHUNK, CONV_CHANNELS), F32)
        for shift in range(ROW_TILE):
            offs = [first + j - shift for j in range(CONV_WIDTH) if (first + j) % ROW_TILE == shift]
            n_win = max(offs) + CONV_CHUNK
            win = win_sc.at[(c * ROW_TILE + shift) % 2]
            win[0:n_win, :] = ext[r0 + shift:r0 + shift + n_win, :]
            for off in offs:
                j = off + shift - first
                acc = acc + w_ref[j:j + 1, :] * win[off:off + CONV_CHUNK, :]
        y = _layer_norm(acc + cb_ref[...], g_ref[...], b_ref[...])
        o_ref[r0:r0 + CONV_CHUNK, :] = (y * jax.nn.sigmoid(y)).astype(o_ref.dtype)


def _conv_module(u2d, conv_w, conv_b, ln_g, ln_b, batch, seq):
    ts = CONV_TILE
    n_s = seq // ts
    vec = pl.BlockSpec((1, CONV_CHANNELS), lambda bi, si: (0, 0))
    return pl.pallas_call(
        _conv_kernel,
        grid=(batch, n_s),
        in_specs=[pl.BlockSpec((ts, CONV_CHANNELS), lambda bi, si: (bi * n_s + si, 0)),
                  pl.BlockSpec(conv_w.shape, lambda bi, si: (0, 0)), vec, vec, vec],
        out_specs=pl.BlockSpec((ts, CONV_CHANNELS), lambda bi, si: (bi * n_s + si, 0)),
        out_shape=jax.ShapeDtypeStruct((batch * seq, CONV_CHANNELS), BF16),
        scratch_shapes=[pltpu.VMEM((ts + CONV_HALO, CONV_CHANNELS), F32),
                        pltpu.VMEM((2, CONV_CHUNK + CONV_HALO, CONV_CHANNELS), F32)],
        compiler_params=_params(("arbitrary", "arbitrary")),
        name="conv_module",
    )(u2d, conv_w, conv_b, ln_g, ln_b)


def _out_ln_kernel(*refs, n_in):
    a_refs = refs[:n_in]
    w_refs = refs[n_in:2 * n_in]
    x_ref, g_ref, b_ref, o_ref, ot_ref = refs[2 * n_in:]
    mix = jnp.dot(a_refs[0][...], w_refs[0][...], preferred_element_type=F32)
    for a_ref, w_ref in zip(a_refs[1:], w_refs[1:]):
        mix = mix + jnp.dot(a_ref[...], w_ref[...], preferred_element_type=F32)
    y = _layer_norm(DN_ALPHA * x_ref[...] + mix, g_ref[...], b_ref[...])
    o_ref[...] = y
    _store_rows(ot_ref, y)


def _out_ln(acts, weights, x2d, g, b):
    t = x2d.shape[0]
    tm = TOKEN_TILE
    vec = pl.BlockSpec((1, D_MODEL), lambda i: (0, 0))
    return pl.pallas_call(
        functools.partial(_out_ln_kernel, n_in=len(acts)),
        grid=(t // tm,),
        in_specs=([pl.BlockSpec((tm, a.shape[1]), lambda i: (i, 0)) for a in acts]
                  + [pl.BlockSpec(w.shape, lambda i: (0, 0)) for w in weights]
                  + [pl.BlockSpec((tm, D_MODEL), lambda i: (i, 0)), vec, vec]),
        out_specs=[pl.BlockSpec((tm, D_MODEL), lambda i: (i, 0)),
                   pl.BlockSpec((tm * ROW_TILE, LANES), lambda i: (i, 0))],
        out_shape=[jax.ShapeDtypeStruct((t, D_MODEL), F32),
                   jax.ShapeDtypeStruct((t * ROW_TILE, LANES), F32)],
        compiler_params=_params(("parallel",)),
        name="out_proj_ln",
    )(*acts, *weights, x2d, g, b)


def _router_kernel(x_ref, rw_ref, rb_ref, tri_ref, idx_ref, gate_ref, rank_ref, cnt_ref):
    tm = x_ref.shape[0]

    @pl.when(pl.program_id(0) == 0)
    def _():
        cnt_ref[...] = jnp.zeros(cnt_ref.shape, F32)

    logits = lax.dot_general(rw_ref[...], x_ref[...], (((1,), (1,)), ((), ())),
                             precision=lax.Precision.HIGHEST,
                             preferred_element_type=F32) + rb_ref[...]
    e_iota = lax.broadcasted_iota(jnp.int32, (N_EXPERTS, tm), 0)
    vals, sels = [], []
    work = logits
    for k in range(TOP_K):
        top = jnp.max(work, axis=0, keepdims=True)
        idx = jnp.min(jnp.where(work == top, e_iota, N_EXPERTS), axis=0, keepdims=True)
        sel = e_iota == idx
        idx_ref[k:k + 1, :] = idx
        vals.append(top)
        sels.append(sel)
        work = jnp.where(sel, -jnp.inf, work)
    exps = [jnp.exp(v - vals[0]) for v in vals]
    denom = exps[0] + exps[1] + exps[2] + exps[3]
    for k in range(TOP_K):
        gate_ref[k:k + 1, :] = exps[k] / denom
    chosen = jnp.where(sels[0] | sels[1] | sels[2] | sels[3], 1.0, 0.0)
    earlier = jnp.dot(chosen.astype(BF16), tri_ref[...], preferred_element_type=F32)
    base = earlier + cnt_ref[:, 0:1]
    for k in range(TOP_K):
        rank = jnp.sum(jnp.where(sels[k], base, 0.0), axis=0, keepdims=True)
        rank_ref[k:k + 1, :] = rank.astype(jnp.int32)
    cnt_ref[...] = cnt_ref[...] + jnp.sum(chosen, axis=1, keepdims=True)


def _router(x2d, rw_t, rb, tri):
    t = x2d.shape[0]
    tm = TOKEN_TILE
    kt = pl.BlockSpec((TOP_K, tm), lambda i: (0, i))
    return pl.pallas_call(
        _router_kernel,
        grid=(t // tm,),
        in_specs=[pl.BlockSpec((tm, D_MODEL), lambda i: (i, 0)),
                  pl.BlockSpec(rw_t.shape, lambda i: (0, 0)),
                  pl.BlockSpec(rb.shape, lambda i: (0, 0)),
                  pl.BlockSpec(tri.shape, lambda i: (0, 0))],
        out_specs=[kt, kt, kt, pl.BlockSpec((N_EXPERTS, LANES), lambda i: (0, 0))],
        out_shape=[jax.ShapeDtypeStruct((TOP_K, t), jnp.int32),
                   jax.ShapeDtypeStruct((TOP_K, t), F32),
                   jax.ShapeDtypeStruct((TOP_K, t), jnp.int32),
                   jax.ShapeDtypeStruct((N_EXPERTS, LANES), F32)],
        compiler_params=_params(("arbitrary",)),
        name="router",
    )(x2d, rw_t, rb, tri)


def _dispatch_kernel(pend_ref, padded_ref, dest_ref, xt_ref, xs_ref, zeros, zsem, sem):
    td = dest_ref.shape[1]
    zrows = zeros.shape[0]

    @pl.when(pl.program_id(0) == 0)
    def _():
        zeros[...] = jnp.zeros(zeros.shape, F32)

        def zero_copy(e):
            start = pl.multiple_of(pend_ref[e] * ROW_TILE - zrows, zrows)
            return pltpu.make_async_copy(zeros, xs_ref.at[pl.ds(start, zrows)], zsem)

        def start_zero(e, carry):
            @pl.when(padded_ref[e] > 0)
            def _():
                zero_copy(e).start()
            return carry

        def wait_zero(e, carry):
            @pl.when(padded_ref[e] > 0)
            def _():
                zero_copy(e).wait()
            return carry

        lax.fori_loop(0, N_EXPERTS, start_zero, 0)
        lax.fori_loop(0, N_EXPERTS, wait_zero, 0)

    def row_copy(t8, u, k):
        src = pl.multiple_of((t8 * ROW_TILE + u) * ROW_TILE, ROW_TILE)
        dst = pl.multiple_of(dest_ref[k, t8 * ROW_TILE + u], ROW_TILE)
        return pltpu.make_async_copy(xt_ref.at[pl.ds(src, ROW_TILE)], xs_ref.at[pl.ds(dst, ROW_TILE)], sem)

    def issue(t8, carry):
        for u in range(ROW_TILE):
            for k in range(TOP_K):
                row_copy(t8, u, k).start(priority=k % 2)
        return carry

    def drain(t8, carry):
        for u in range(ROW_TILE):
            for k in range(TOP_K):
                row_copy(t8, u, k).wait()
        return carry

    lax.fori_loop(0, td // ROW_TILE, issue, 0)
    lax.fori_loop(0, td // ROW_TILE, drain, 0)


def _dispatch(pend, padded, dest8, xt, n_rows):
    t = dest8.shape[1]
    td = DISPATCH_TILE
    return pl.pallas_call(
        _dispatch_kernel,
        grid_spec=pltpu.PrefetchScalarGridSpec(
            num_scalar_prefetch=2,
            grid=(t // td,),
            in_specs=[pl.BlockSpec((TOP_K, td), lambda i, pe, pa: (0, i), memory_space=pltpu.SMEM),
                      pl.BlockSpec((td * ROW_TILE, LANES), lambda i, pe, pa: (i, 0))],
            out_specs=pl.BlockSpec(memory_space=pl.ANY),
            scratch_shapes=[pltpu.VMEM((EXPERT_ROWS * ROW_TILE, LANES), F32),
                            pltpu.SemaphoreType.DMA(()), pltpu.SemaphoreType.DMA(())]),
        out_shape=jax.ShapeDtypeStruct((n_rows * ROW_TILE, LANES), F32),
        compiler_params=_params(("arbitrary",)),
        name="moe_dispatch",
    )(pend, padded, dest8, xt)


def _expert_kernel(blk_e_ref, grp_ref, nxt_ref, n_used_ref, xs_ref, wgu_hbm, bgu_ref, wdn_hbm, bdn_ref,
                   ys_ref, wgu_f32, wdn_f32, wgu_bf, wdn_bf, sems):
    i = pl.program_id(0)
    rb = xs_ref.shape[0] // ROW_TILE
    active = i < n_used_ref[0]
    new_expert = (i == 0) | (blk_e_ref[i] != blk_e_ref[jnp.maximum(i - 1, 0)])

    def weight_copies(e, slot):
        return (pltpu.make_async_copy(wgu_hbm.at[e], wgu_f32.at[slot], sems.at[0, slot]),
                pltpu.make_async_copy(wdn_hbm.at[e], wdn_f32.at[slot], sems.at[1, slot]))

    @pl.when(active & new_expert)
    def _():
        slot = grp_ref[i] % 2
        e = blk_e_ref[i]
        nxt = nxt_ref[i]

        @pl.when(i == 0)
        def _():
            for cp in weight_copies(e, slot):
                cp.start()

        @pl.when(nxt >= 0)
        def _():
            for cp in weight_copies(nxt, 1 - slot):
                cp.start()

        for cp in weight_copies(e, slot):
            cp.wait()

        def cast(c, carry):
            rows = pl.ds(pl.multiple_of(c * CAST_CHUNK, CAST_CHUNK), CAST_CHUNK)
            wgu_bf[rows, :] = wgu_f32[slot, rows, :].astype(BF16)
            wdn_bf[rows, :] = wdn_f32[slot, rows, :].astype(BF16)
            return carry

        lax.fori_loop(0, D_MODEL // CAST_CHUNK, cast, 0)

    @pl.when(active)
    def _():
        xb = _load_rows(xs_ref, rb).astype(BF16)
        h = jnp.dot(xb, wgu_bf[...], preferred_element_type=F32) + bgu_ref[0]
        gate = jnp.minimum(h[:, :D_EXPERT], SWIGLU_LIMIT)
        up = jnp.clip(h[:, D_EXPERT:], -SWIGLU_LIMIT, SWIGLU_LIMIT)
        act = (up + 1.0) * gate * jax.nn.sigmoid(SWIGLU_ALPHA * gate)
        y = jnp.dot(act.astype(BF16), wdn_bf[...], preferred_element_type=F32) + bdn_ref[0]
        _store_rows(ys_ref, y)


def _experts(blk_e, n_used, xs, w_gu, b_gu, w_dn, b_dn):
    rb = EXPERT_ROWS
    n_blk = xs.shape[0] // (rb * ROW_TILE)
    ids = jnp.arange(n_blk, dtype=jnp.int32)
    change = jnp.concatenate([jnp.ones((1,), bool), blk_e[1:] != blk_e[:-1]])
    grp = (jnp.cumsum(change.astype(jnp.int32)) - 1).astype(jnp.int32)
    later = jnp.where(change, ids, n_blk)
    nxt_pos = jnp.concatenate([lax.cummin(later[::-1])[::-1][1:], jnp.full((1,), n_blk, jnp.int32)])
    nxt = jnp.where(nxt_pos < n_blk, blk_e[jnp.minimum(nxt_pos, n_blk - 1)], -1).astype(jnp.int32)
    rows = pl.BlockSpec((rb * ROW_TILE, LANES), lambda i, be, gr, nx, nu: (jnp.minimum(i, nu[0] - 1), 0))
    per_e = lambda a: pl.BlockSpec((1,) + a.shape[1:], lambda i, be, gr, nx, nu: (be[i], 0, 0))
    hbm = pl.BlockSpec(memory_space=pl.ANY)
    return pl.pallas_call(
        _expert_kernel,
        grid_spec=pltpu.PrefetchScalarGridSpec(
            num_scalar_prefetch=4,
            grid=(n_blk,),
            in_specs=[rows, hbm, per_e(b_gu), hbm, per_e(b_dn)],
            out_specs=rows,
            scratch_shapes=[pltpu.VMEM((2,) + w_gu.shape[1:], F32), pltpu.VMEM((2,) + w_dn.shape[1:], F32),
                            pltpu.VMEM(w_gu.shape[1:], BF16), pltpu.VMEM(w_dn.shape[1:], BF16),
                            pltpu.SemaphoreType.DMA((2, 2))]),
        out_shape=jax.ShapeDtypeStruct(xs.shape, F32),
        compiler_params=_params(("arbitrary",)),
        name="moe_experts",
    )(blk_e, grp, nxt, n_used, xs, w_gu, b_gu, w_dn, b_dn)


def _combine_kernel(dest_ref, next_ref, gate_ref, x_ref, ys_ref, g_ref, b_ref, o_ref, buf, sems):
    tc = x_ref.shape[0]
    i = pl.program_id(0)
    slot = i % 2

    def row_copy(d_ref, s, t8, u, k):
        src = pl.multiple_of(d_ref[k, t8 * ROW_TILE + u], ROW_TILE)
        dst = pl.multiple_of((t8 * ROW_TILE + u) * ROW_TILE, ROW_TILE)
        return pltpu.make_async_copy(ys_ref.at[pl.ds(src, ROW_TILE)], buf.at[s, k, pl.ds(dst, ROW_TILE)],
                                     sems.at[s])

    def issue(d_ref, s):
        def body(t8, carry):
            for u in range(ROW_TILE):
                for k in range(TOP_K):
                    row_copy(d_ref, s, t8, u, k).start(priority=k % 2)
            return carry

        lax.fori_loop(0, tc // ROW_TILE, body, 0)

    @pl.when(i == 0)
    def _():
        issue(dest_ref, 0)

    @pl.when(i + 1 < pl.num_programs(0))
    def _():
        issue(next_ref, 1 - slot)

    def drain(t8, carry):
        for u in range(ROW_TILE):
            for k in range(TOP_K):
                row_copy(dest_ref, slot, t8, u, k).wait()
        return carry

    lax.fori_loop(0, tc // ROW_TILE, drain, 0)
    gates = gate_ref[...]
    moe = gates[:, 0:1] * _load_rows(buf.at[slot, 0], tc)
    for k in range(1, TOP_K):
        moe = moe + gates[:, k:k + 1] * _load_rows(buf.at[slot, k], tc)
    o_ref[...] = _layer_norm(DN_ALPHA * x_ref[...] + moe, g_ref[...], b_ref[...])


def _combine(dest8, gates_tk, x2d, ys, g, b):
    t = x2d.shape[0]
    tc = COMBINE_TILE
    n = t // tc
    vec = pl.BlockSpec((1, D_MODEL), lambda i: (0, 0))
    return pl.pallas_call(
        _combine_kernel,
        grid=(n,),
        in_specs=[pl.BlockSpec((TOP_K, tc), lambda i: (0, i), memory_space=pltpu.SMEM),
                  pl.BlockSpec((TOP_K, tc), lambda i: (0, jnp.minimum(i + 1, n - 1)),
                               memory_space=pltpu.SMEM),
                  pl.BlockSpec((tc, TOP_K), lambda i: (i, 0)),
                  pl.BlockSpec((tc, D_MODEL), lambda i: (i, 0)),
                  pl.BlockSpec(memory_space=pl.ANY), vec, vec],
        out_specs=pl.BlockSpec((tc, D_MODEL), lambda i: (i, 0)),
        out_shape=jax.ShapeDtypeStruct((t, D_MODEL), F32),
        scratch_shapes=[pltpu.VMEM((2, TOP_K, tc * ROW_TILE, LANES), F32), pltpu.SemaphoreType.DMA((2,))],
        compiler_params=_params(("arbitrary",)),
        name="moe_combine_ln",
    )(dest8, dest8, gates_tk, x2d, ys, g, b)


def _moe_ln(x2d, xt, router_w, router_b, w_gu, b_gu, w_dn, b_dn, ln_g, ln_b, tri):
    t = x2d.shape[0]
    rb = EXPERT_ROWS
    idx, gates, rank, cnt = _router(x2d, router_w.T, router_b.reshape(N_EXPERTS, 1), tri)
    counts = cnt[:, 0].astype(jnp.int32)
    padded = ((counts + rb - 1) // rb) * rb
    pend = jnp.cumsum(padded).astype(jnp.int32)
    pstart = pend - padded
    n_blk = (t * TOP_K) // rb + N_EXPERTS
    n_used = pend[-1] // rb
    blk = jnp.minimum(jnp.arange(n_blk, dtype=jnp.int32), n_used - 1)
    blk_e = jnp.sum((blk[:, None] * rb >= pend[None, :]).astype(jnp.int32), axis=1)
    blk_e = jnp.minimum(blk_e, N_EXPERTS - 1).astype(jnp.int32)
    e_ids = jnp.arange(N_EXPERTS, dtype=jnp.int32)[:, None, None]
    dest = rank + jnp.sum(jnp.where(idx[None] == e_ids, pstart[:, None, None], 0), axis=0)
    dest8 = (dest * ROW_TILE).astype(jnp.int32)

    xs = _dispatch(pend, padded.astype(jnp.int32), dest8, xt, n_blk * rb)
    ys = _experts(blk_e, n_used.reshape(1).astype(jnp.int32), xs,
                  w_gu, b_gu.reshape(N_EXPERTS, 1, -1), w_dn, b_dn.reshape(N_EXPERTS, 1, -1))
    return _combine(dest8, gates.T, x2d, ys, ln_g.reshape(1, -1), ln_b.reshape(1, -1))


def _spread_rope_cols(w):
    half = QK_ROPE_DIM // 2
    z = jnp.zeros(w.shape[:-1] + (LANES // 2 - half,), w.dtype)
    return jnp.concatenate([w[..., :half], z, w[..., half:], z], axis=-1)


def _rope_tables(seq, dim, spread):
    inv_freq = 1.0 / (ROPE_THETA ** (jnp.arange(0, dim, 2, dtype=F32) / dim))
    ang = jnp.arange(seq, dtype=F32)[:, None] * inv_freq[None, :]
    cos, sin = jnp.cos(ang), jnp.sin(ang)
    if spread:
        z = jnp.zeros((seq, LANES // 2 - dim // 2), F32)
        return (jnp.concatenate([cos, z, cos, z], axis=1), jnp.concatenate([-sin, z, sin, z], axis=1))
    return jnp.concatenate([cos, cos], axis=1), jnp.concatenate([-sin, sin], axis=1)


def kernel(x, l0_w_in, l0_q_norm_g, l0_w_q_up, l0_kv_norm_g, l0_w_kv_up, l0_conv_w, l0_conv_b, l0_conv_ln_g, l0_conv_ln_b, l0_w_o, l0_ln1_g, l0_ln1_b, l0_router_w, l0_router_b, l0_w_gu, l0_b_gu, l0_w_dn, l0_b_dn, l0_ln2_g, l0_ln2_b, l1_w_qkv, l1_lambda_q1, l1_lambda_k1, l1_lambda_q2, l1_lambda_k2, l1_subln_g, l1_w_o, l1_ln1_g, l1_ln1_b, l1_router_w, l1_router_b, l1_w_gu, l1_b_gu, l1_w_dn, l1_b_dn, l1_ln2_g, l1_ln2_b):
    b, s, d = x.shape
    t = b * s
    x2d = x.reshape(t, d)
    row = lambda a: a.reshape(1, -1)
    tri = (jnp.arange(TOKEN_TILE)[:, None] < jnp.arange(TOKEN_TILE)[None, :]).astype(BF16)

    o1 = Q_LORA_RANK
    o2 = o1 + KV_LORA_RANK
    o3 = o2 + QK_ROPE_DIM
    w_in = jnp.concatenate([l0_w_in[:, :o2], _spread_rope_cols(l0_w_in[:, o2:o3]), l0_w_in[:, o3:]],
                           axis=1).astype(BF16)
    wq = l0_w_q_up.reshape(Q_LORA_RANK, MLA_HEADS, QK_NOPE_DIM + QK_ROPE_DIM)
    wq = jnp.concatenate([wq[..., :QK_NOPE_DIM], _spread_rope_cols(wq[..., QK_NOPE_DIM:])], axis=-1)
    wq = wq.reshape(Q_LORA_RANK, MLA_HEADS * 2 * LANES).astype(BF16)
    wkv = l0_w_kv_up.reshape(KV_LORA_RANK, MLA_HEADS, QK_NOPE_DIM + V_HEAD_DIM)
    wkv = jnp.concatenate([wkv[..., :QK_NOPE_DIM].reshape(KV_LORA_RANK, -1),
                           wkv[..., QK_NOPE_DIM:].reshape(KV_LORA_RANK, -1)], axis=1).astype(BF16)
    cos0, sin0 = _rope_tables(s, QK_ROPE_DIM, spread=True)
    q, k, v, u = _l0_proj(x2d, w_in, row(l0_q_norm_g), wq, row(l0_kv_norm_g), wkv, cos0, sin0, s)
    hw = MLA_HEADS * 2 * LANES
    attn = _attention(q.reshape(b, s, hw), k.reshape(b, s, hw), v.reshape(b, s, hw),
                      MLA_HEADS, 1, 2 * LANES, 2 * LANES, V_HEAD_DIM)
    conv_w = jnp.concatenate([l0_conv_w, jnp.zeros((1, CONV_CHANNELS), F32)], axis=0)
    uc = _conv_module(u, conv_w, row(l0_conv_b), row(l0_conv_ln_g), row(l0_conv_ln_b), b, s)
    n_attn = MLA_HEADS * V_HEAD_DIM
    w_o = l0_w_o.astype(BF16)
    x2d, xt = _out_ln([attn.reshape(t, n_attn), uc], [w_o[:n_attn], w_o[n_attn:]], x2d,
                      row(l0_ln1_g), row(l0_ln1_b))
    x2d = _moe_ln(x2d, xt, l0_router_w, l0_router_b, l0_w_gu, l0_b_gu, l0_w_dn, l0_b_dn,
                  l0_ln2_g, l0_ln2_b, tri)

    lambda_init = 0.8 - 0.6 * math.exp(-0.3 * 1)
    cos1, sin1 = _rope_tables(s, DIFF_HEAD_DIM, spread=False)
    q, k, v = _l1_proj(x2d, l1_w_qkv.astype(BF16), cos1, sin1, s)
    lam_in = jnp.stack([l1_lambda_q1, l1_lambda_k1, l1_lambda_q2, l1_lambda_k2]).astype(F32)
    dv = 2 * DIFF_HEAD_DIM
    attn = _attention(q.reshape(b, s, d), k.reshape(b, s, d), v.reshape(b, s, d),
                      DIFF_HEADS, 2, DIFF_HEAD_DIM, dv, dv,
                      extra=(lam_in, row(l1_subln_g)), lambda_init=lambda_init)
    x2d, xt = _out_ln([attn.reshape(t, d)], [l1_w_o.astype(BF16)], x2d, row(l1_ln1_g), row(l1_ln1_b))
    x2d = _moe_ln(x2d, xt, l1_router_w, l1_router_b, l1_w_gu, l1_b_gu, l1_w_dn, l1_b_dn,
                  l1_ln2_g, l1_ln2_b, tri)
    return x2d.reshape(b, s, d)
```

```python
import functools
import math

import jax
import jax.numpy as jnp
from jax import lax
from jax.experimental import pallas as pl
from jax.experimental.pallas import tpu as pltpu

F32 = jnp.float32
BF16 = jnp.bfloat16

D_MODEL = 1024
DEPTH = 2
MLA_HEADS = 4
QK_NOPE_DIM = 128
QK_ROPE_DIM = 64
V_HEAD_DIM = 128
Q_LORA_RANK = 384
KV_LORA_RANK = 256
CONV_CHANNELS = D_MODEL - MLA_HEADS * V_HEAD_DIM
CONV_WIDTH = 31
DIFF_HEAD_DIM = 128
DIFF_HEADS = D_MODEL // (2 * DIFF_HEAD_DIM)
N_EXPERTS = 32
TOP_K = 4
D_EXPERT = D_MODEL
SWIGLU_LIMIT = 7.0
SWIGLU_ALPHA = 1.702
ROPE_THETA = 10000.0
DN_ALPHA = (2 * DEPTH) ** 0.25
LN_EPS = 1e-5
RMS_EPS = 1e-6
MASK_VALUE = -1e30

LANES = 128
ROW_TILE = 8
TOKEN_TILE = 512
ATTN_Q_TILE = 2048
ATTN_K_TILE = 1024
ATTN_CHAINS = 8
CONV_TILE = 512
CONV_HALO = 32
CONV_CHUNK = 64
EXPERT_ROWS = 512
DISPATCH_TILE = 1024
COMBINE_TILE = 256
CAST_CHUNK = 128
VMEM_LIMIT = 56 * 1024 * 1024


def _params(sem, vmem=VMEM_LIMIT):
    return pltpu.CompilerParams(dimension_semantics=sem, vmem_limit_bytes=vmem)


def _layer_norm(r, g, b):
    mu = jnp.mean(r, axis=-1, keepdims=True)
    d = r - mu
    var = jnp.mean(d * d, axis=-1, keepdims=True)
    return d * lax.rsqrt(var + LN_EPS) * g + b


def _rms_norm(x, g):
    return x * lax.rsqrt(jnp.mean(x * x, axis=-1, keepdims=True) + RMS_EPS) * g


def _rope(x, cos, sin):
    return x * cos + pltpu.roll(x, 64, 1) * sin


def _repeat_lanes(x, n):
    return x if n == 1 else jnp.concatenate([x] * n, axis=1)


def _load_rows(ref, n_rows):
    return jnp.concatenate([ref[pl.ds(j, n_rows, stride=ROW_TILE), :] for j in range(ROW_TILE)], axis=1)


def _store_rows(ref, val):
    n_rows = val.shape[0]
    for j in range(ROW_TILE):
        ref[pl.ds(j, n_rows, stride=ROW_TILE), :] = val[:, j * LANES:(j + 1) * LANES]


def _l0_proj_kernel(x_ref, win_ref, qg_ref, wq_ref, kvg_ref, wkv_ref, cos_ref, sin_ref,
                    q_ref, k_ref, v_ref, u_ref):
    tm = x_ref.shape[0]
    xb = x_ref[...].astype(BF16)
    proj = jnp.dot(xb, win_ref[...], preferred_element_type=F32)
    o1 = Q_LORA_RANK
    o2 = o1 + KV_LORA_RANK
    o3 = o2 + LANES
    o4 = o3 + CONV_CHANNELS
    u_ref[...] = proj[:, o3:o4] * jax.nn.sigmoid(proj[:, o4:])
    cos = cos_ref[...]
    sin = sin_ref[...]
    scale = (QK_NOPE_DIM + QK_ROPE_DIM) ** -0.5
    qn = _rms_norm(proj[:, :o1], qg_ref[...])
    qup = jnp.dot(qn.astype(BF16), wq_ref[...], preferred_element_type=F32)
    kvn = _rms_norm(proj[:, o1:o2], kvg_ref[...])
    kvup = jnp.dot(kvn.astype(BF16), wkv_ref[...], preferred_element_type=F32)
    k_rope = _rope(proj[:, o2:o3], cos, sin).astype(BF16)
    ones_col = jnp.where(lax.broadcasted_iota(jnp.int32, (tm, LANES), 1) == 0, 1.0, 0.0).astype(BF16)
    for h in range(MLA_HEADS):
        c = 2 * LANES * h
        q_ref[:, c:c + LANES] = (qup[:, c:c + LANES] * scale).astype(BF16)
        q_rope = _rope(qup[:, c + LANES:c + 2 * LANES], cos, sin)
        q_ref[:, c + LANES:c + 2 * LANES] = (q_rope * scale).astype(BF16)
        k_ref[:, c:c + LANES] = kvup[:, LANES * h:LANES * (h + 1)].astype(BF16)
        k_ref[:, c + LANES:c + 2 * LANES] = k_rope
        vh = MLA_HEADS * LANES + LANES * h
        v_ref[:, c:c + LANES] = kvup[:, vh:vh + LANES].astype(BF16)
        v_ref[:, c + LANES:c + 2 * LANES] = ones_col


def _l0_proj(x2d, w_in, qg, wq, kvg, wkv, cos, sin, seq):
    t = x2d.shape[0]
    tm = TOKEN_TILE
    n_pos = seq // tm
    full = lambda a: pl.BlockSpec(a.shape, lambda i: (0,) * a.ndim)
    row = lambda w: pl.BlockSpec((tm, w), lambda i: (i, 0))
    pos = pl.BlockSpec((tm, LANES), lambda i: (i % n_pos, 0))
    hw = MLA_HEADS * 2 * LANES
    wide = jax.ShapeDtypeStruct((t, hw), BF16)
    return pl.pallas_call(
        _l0_proj_kernel,
        grid=(t // tm,),
        in_specs=[row(D_MODEL), full(w_in), full(qg), full(wq), full(kvg), full(wkv), pos, pos],
        out_specs=[row(hw), row(hw), row(hw), row(CONV_CHANNELS)],
        out_shape=[wide, wide, wide, jax.ShapeDtypeStruct((t, CONV_CHANNELS), F32)],
        compiler_params=_params(("parallel",)),
        name="l0_proj",
    )(x2d, w_in, qg, wq, kvg, wkv, cos, sin)


def _l1_proj_kernel(x_ref, w_ref, cos_ref, sin_ref, q_ref, k_ref, v_ref):
    xb = x_ref[...].astype(BF16)
    cos = cos_ref[...]
    sin = sin_ref[...]
    scale = DIFF_HEAD_DIM ** -0.5
    qk_w = DIFF_HEADS * 2 * DIFF_HEAD_DIM
    q = jnp.dot(xb, w_ref[:, :qk_w], preferred_element_type=F32)
    for j in range(qk_w // LANES):
        c = j * LANES
        q_ref[:, c:c + LANES] = (_rope(q[:, c:c + LANES], cos, sin) * scale).astype(BF16)
    k = jnp.dot(xb, w_ref[:, qk_w:2 * qk_w], preferred_element_type=F32)
    for j in range(qk_w // LANES):
        c = j * LANES
        k_ref[:, c:c + LANES] = _rope(k[:, c:c + LANES], cos, sin).astype(BF16)
    v_ref[...] = jnp.dot(xb, w_ref[:, 2 * qk_w:], preferred_element_type=F32).astype(BF16)


def _l1_proj(x2d, w_qkv, cos, sin, seq):
    t = x2d.shape[0]
    tm = TOKEN_TILE
    n_pos = seq // tm
    row = lambda w: pl.BlockSpec((tm, w), lambda i: (i, 0))
    pos = pl.BlockSpec((tm, LANES), lambda i: (i % n_pos, 0))
    out = jax.ShapeDtypeStruct((t, D_MODEL), BF16)
    return pl.pallas_call(
        _l1_proj_kernel,
        grid=(t // tm,),
        in_specs=[row(D_MODEL), pl.BlockSpec(w_qkv.shape, lambda i: (0, 0)), pos, pos],
        out_specs=[row(D_MODEL)] * 3,
        out_shape=[out, out, out],
        compiler_params=_params(("parallel",)),
        name="l1_proj",
    )(x2d, w_qkv, cos, sin)


def _attn_kernel(*refs, n_maps, dk, tq, tk, n_chains, sum_col, lambda_init):
    if n_maps == 2:
        q_ref, k_ref, v_ref, lam_ref, g_ref, o_ref, m_sc, l_sc, acc_sc = refs
    else:
        q_ref, k_ref, v_ref, o_ref, m_sc, l_sc, acc_sc = refs
    qi = pl.program_id(2)
    rs = tq // n_chains
    dv = v_ref.shape[2]
    m_sc[...] = jnp.full(m_sc.shape, MASK_VALUE, F32)
    l_sc[...] = jnp.zeros(l_sc.shape, F32)
    acc_sc[...] = jnp.zeros(acc_sc.shape, F32)

    def chain(m, r, k, v, mask):
        rows = slice(r * rs, (r + 1) * rs)
        q = q_ref[0, rows, m * dk:(m + 1) * dk]
        s = lax.dot_general(q, k, (((1,), (1,)), ((), ())), preferred_element_type=F32)
        if mask is not None:
            s = jnp.where(mask, s, MASK_VALUE)
        m_prev = m_sc[m, rows, :]
        m_new = jnp.maximum(m_prev, jnp.max(s, axis=1, keepdims=True))
        p = jnp.exp(s - _repeat_lanes(m_new, s.shape[1] // LANES))
        a = jnp.exp(m_prev - m_new)
        if sum_col is None:
            l_sc[m, rows, :] = a * l_sc[m, rows, :] + jnp.sum(p, axis=1, keepdims=True)
        acc_sc[m, rows, :] = (_repeat_lanes(a, dv // LANES) * acc_sc[m, rows, :]
                              + jnp.dot(p.astype(BF16), v, preferred_element_type=F32))
        m_sc[m, rows, :] = m_new

    def off_diagonal(j, carry):
        start = pl.multiple_of(j * tk, tk)
        v = v_ref[0, pl.ds(start, tk), :]
        for m in range(n_maps):
            k = k_ref[0, pl.ds(start, tk), m * dk:(m + 1) * dk]
            for r in range(n_chains):
                chain(m, r, k, v, None)
        return carry

    lax.fori_loop(0, qi * (tq // tk), off_diagonal, 0)
    base = pl.multiple_of(qi * tq, tq)
    for r in range(n_chains):
        nk = (r + 1) * rs
        v = v_ref[0, pl.ds(base, nk), :]
        row = lax.broadcasted_iota(jnp.int32, (rs, nk), 0) + r * rs
        col = lax.broadcasted_iota(jnp.int32, (rs, nk), 1)
        mask = col <= row
        for m in range(n_maps):
            k = k_ref[0, pl.ds(base, nk), m * dk:(m + 1) * dk]
            chain(m, r, k, v, mask)

    if n_maps == 1:
        acc = acc_sc[0]
        o_ref[0] = (acc[:, :sum_col] / acc[:, sum_col:sum_col + 1]).astype(o_ref.dtype)
    else:
        lam_in = lam_ref[...]
        lam = (jnp.exp(jnp.sum(lam_in[0:1] * lam_in[1:2], axis=1, keepdims=True))
               - jnp.exp(jnp.sum(lam_in[2:3] * lam_in[3:4], axis=1, keepdims=True)) + lambda_init)
        a = acc_sc[0] / l_sc[0][:, 0:1] - lam * (acc_sc[1] / l_sc[1][:, 0:1])
        o_ref[0] = (_rms_norm(a, g_ref[...]) * (1.0 - lambda_init)).astype(o_ref.dtype)


def _attention(q, k, v, n_heads, n_maps, dk, dv_in, dv_out, extra=(), lambda_init=0.0):
    b, s, _ = q.shape
    tq, tk = ATTN_Q_TILE, ATTN_K_TILE
    qw = n_maps * dk
    sum_col = dv_out if dv_in > dv_out else None
    extra_specs = [pl.BlockSpec(e.shape, lambda bi, h, qi: (0, 0)) for e in extra]
    kern = functools.partial(_attn_kernel, n_maps=n_maps, dk=dk, tq=tq, tk=tk, n_chains=ATTN_CHAINS,
                             sum_col=sum_col, lambda_init=lambda_init)
    return pl.pallas_call(
        kern,
        grid=(b, n_heads, s // tq),
        in_specs=[pl.BlockSpec((1, tq, qw), lambda bi, h, qi: (bi, qi, h)),
                  pl.BlockSpec((1, s, qw), lambda bi, h, qi: (bi, 0, h)),
                  pl.BlockSpec((1, s, dv_in), lambda bi, h, qi: (bi, 0, h))] + extra_specs,
        out_specs=pl.BlockSpec((1, tq, dv_out), lambda bi, h, qi: (bi, qi, h)),
        out_shape=jax.ShapeDtypeStruct((b, s, n_heads * dv_out), BF16),
        scratch_shapes=[pltpu.VMEM((n_maps, tq, LANES), F32), pltpu.VMEM((n_maps, tq, LANES), F32),
                        pltpu.VMEM((n_maps, tq, dv_in), F32)],
        compiler_params=_params(("parallel", "parallel", "arbitrary")),
        name="attention_%dmap" % n_maps,
    )(q, k, v, *extra)


def _conv_kernel(u_ref, w_ref, cb_ref, g_ref, b_ref, o_ref, ext, win_sc):
    ts = u_ref.shape[0]
    si = pl.program_id(1)

    @pl.when(si == 0)
    def _():
        ext[0:CONV_HALO, :] = jnp.zeros((CONV_HALO, CONV_CHANNELS), F32)

    @pl.when(si > 0)
    def _():
        ext[0:CONV_HALO, :] = ext[ts:ts + CONV_HALO, :]

    ext[CONV_HALO:CONV_HALO + ts, :] = u_ref[...]
    first = CONV_HALO - (CONV_WIDTH - 1)
    for c in range(ts // CONV_CHUNK):
        r0 = c * CONV_CHUNK
        acc = jnp.zeros((CONV_CHUNK, CONV_CHANNELS), F32)
        for shift in range(ROW_TILE):
            offs = [first + j - shift for j in range(CONV_WIDTH) if (first + j) % ROW_TILE == shift]
            n_win = max(offs) + CONV_CHUNK
            win = win_sc.at[(c * ROW_TILE + shift) % 2]
            win[0:n_win, :] = ext[r0 + shift:r0 + shift + n_win, :]
            for off in offs:
                j = off + shift - first
                acc = acc + w_ref[j:j + 1, :] * win[off:off + CONV_CHUNK, :]
        y = _layer_norm(acc + cb_ref[...], g_ref[...], b_ref[...])
        o_ref[r0:r0 + CONV_CHUNK, :] = (y * jax.nn.sigmoid(y)).astype(o_ref.dtype)


def _conv_module(u2d, conv_w, conv_b, ln_g, ln_b, batch, seq):
    ts = CONV_TILE
    n_s = seq // ts
    vec = pl.BlockSpec((1, CONV_CHANNELS), lambda bi, si: (0, 0))
    return pl.pallas_call(
        _conv_kernel,
        grid=(batch, n_s),
        in_specs=[pl.BlockSpec((ts, CONV_CHANNELS), lambda bi, si: (bi * n_s + si, 0)),
                  pl.BlockSpec(conv_w.shape, lambda bi, si: (0, 0)), vec, vec, vec],
        out_specs=pl.BlockSpec((ts, CONV_CHANNELS), lambda bi, si: (bi * n_s + si, 0)),
        out_shape=jax.ShapeDtypeStruct((batch * seq, CONV_CHANNELS), BF16),
        scratch_shapes=[pltpu.VMEM((ts + CONV_HALO, CONV_CHANNELS), F32),
                        pltpu.VMEM((2, CONV_CHUNK + CONV_HALO, CONV_CHANNELS), F32)],
        compiler_params=_params(("arbitrary", "arbitrary")),
        name="conv_module",
    )(u2d, conv_w, conv_b, ln_g, ln_b)


def _out_ln_kernel(*refs, n_in):
    a_refs = refs[:n_in]
    w_refs = refs[n_in:2 * n_in]
    x_ref, g_ref, b_ref, o_ref, ot_ref = refs[2 * n_in:]
    mix = jnp.dot(a_refs[0][...], w_refs[0][...], preferred_element_type=F32)
    for a_ref, w_ref in zip(a_refs[1:], w_refs[1:]):
        mix = mix + jnp.dot(a_ref[...], w_ref[...], preferred_element_type=F32)
    y = _layer_norm(DN_ALPHA * x_ref[...] + mix, g_ref[...], b_ref[...])
    o_ref[...] = y
    _store_rows(ot_ref, y)


def _out_ln(acts, weights, x2d, g, b):
    t = x2d.shape[0]
    tm = TOKEN_TILE
    vec = pl.BlockSpec((1, D_MODEL), lambda i: (0, 0))
    return pl.pallas_call(
        functools.partial(_out_ln_kernel, n_in=len(acts)),
        grid=(t // tm,),
        in_specs=([pl.BlockSpec((tm, a.shape[1]), lambda i: (i, 0)) for a in acts]
                  + [pl.BlockSpec(w.shape, lambda i: (0, 0)) for w in weights]
                  + [pl.BlockSpec((tm, D_MODEL), lambda i: (i, 0)), vec, vec]),
        out_specs=[pl.BlockSpec((tm, D_MODEL), lambda i: (i, 0)),
                   pl.BlockSpec((tm * ROW_TILE, LANES), lambda i: (i, 0))],
        out_shape=[jax.ShapeDtypeStruct((t, D_MODEL), F32),
                   jax.ShapeDtypeStruct((t * ROW_TILE, LANES), F32)],
        compiler_params=_params(("parallel",)),
        name="out_proj_ln",
    )(*acts, *weights, x2d, g, b)


def _router_kernel(x_ref, rw_ref, rb_ref, tri_ref, idx_ref, gate_ref, rank_ref, cnt_ref):
    tm = x_ref.shape[0]

    @pl.when(pl.program_id(0) == 0)
    def _():
        cnt_ref[...] = jnp.zeros(cnt_ref.shape, F32)

    logits = lax.dot_general(rw_ref[...], x_ref[...], (((1,), (1,)), ((), ())),
                             precision=lax.Precision.HIGHEST,
                             preferred_element_type=F32) + rb_ref[...]
    e_iota = lax.broadcasted_iota(jnp.int32, (N_EXPERTS, tm), 0)
    vals, sels = [], []
    work = logits
    for k in range(TOP_K):
        top = jnp.max(work, axis=0, keepdims=True)
        idx = jnp.min(jnp.where(work == top, e_iota, N_EXPERTS), axis=0, keepdims=True)
        sel = e_iota == idx
        idx_ref[k:k + 1, :] = idx
        vals.append(top)
        sels.append(sel)
        work = jnp.where(sel, -jnp.inf, work)
    exps = [jnp.exp(v - vals[0]) for v in vals]
    denom = exps[0] + exps[1] + exps[2] + exps[3]
    for k in range(TOP_K):
        gate_ref[k:k + 1, :] = exps[k] / denom
    chosen = jnp.where(sels[0] | sels[1] | sels[2] | sels[3], 1.0, 0.0)
    earlier = jnp.dot(chosen.astype(BF16), tri_ref[...], preferred_element_type=F32)
    base = earlier + cnt_ref[:, 0:1]
    for k in range(TOP_K):
        rank = jnp.sum(jnp.where(sels[k], base, 0.0), axis=0, keepdims=True)
        rank_ref[k:k + 1, :] = rank.astype(jnp.int32)
    cnt_ref[...] = cnt_ref[...] + jnp.sum(chosen, axis=1, keepdims=True)


def _router(x2d, rw_t, rb, tri):
    t = x2d.shape[0]
    tm = TOKEN_TILE
    kt = pl.BlockSpec((TOP_K, tm), lambda i: (0, i))
    return pl.pallas_call(
        _router_kernel,
        grid=(t // tm,),
        in_specs=[pl.BlockSpec((tm, D_MODEL), lambda i: (i, 0)),
                  pl.BlockSpec(rw_t.shape, lambda i: (0, 0)),
                  pl.BlockSpec(rb.shape, lambda i: (0, 0)),
                  pl.BlockSpec(tri.shape, lambda i: (0, 0))],
        out_specs=[kt, kt, kt, pl.BlockSpec((N_EXPERTS, LANES), lambda i: (0, 0))],
        out_shape=[jax.ShapeDtypeStruct((TOP_K, t), jnp.int32),
                   jax.ShapeDtypeStruct((TOP_K, t), F32),
                   jax.ShapeDtypeStruct((TOP_K, t), jnp.int32),
                   jax.ShapeDtypeStruct((N_EXPERTS, LANES), F32)],
        compiler_params=_params(("arbitrary",)),
        name="router",
    )(x2d, rw_t, rb, tri)


def _dispatch_kernel(pend_ref, padded_ref, dest_ref, xt_ref, xt_hbm, xs_ref, zeros, zsem, sem, *, hbm_ks):
    td = dest_ref.shape[0] // TOP_K
    zrows = zeros.shape[0]
    tile0 = pl.program_id(0) * (td * ROW_TILE)

    @pl.when(pl.program_id(0) == 0)
    def _():
        zeros[...] = jnp.zeros(zeros.shape, F32)

        def zero_copy(e):
            start = pl.multiple_of(pend_ref[e] * ROW_TILE - zrows, zrows)
            return pltpu.make_async_copy(zeros, xs_ref.at[pl.ds(start, zrows)], zsem)

        def start_zero(e, carry):
            @pl.when(padded_ref[e] > 0)
            def _():
                zero_copy(e).start()
            return carry

        def wait_zero(e, carry):
            @pl.when(padded_ref[e] > 0)
            def _():
                zero_copy(e).wait()
            return carry

        lax.fori_loop(0, N_EXPERTS, start_zero, 0)
        lax.fori_loop(0, N_EXPERTS, wait_zero, 0)

    def row_copy(t8, u, k):
        src = pl.multiple_of((t8 * ROW_TILE + u) * ROW_TILE, ROW_TILE)
        dst = pl.multiple_of(dest_ref[t8 * (ROW_TILE * TOP_K) + (u * TOP_K + k)], ROW_TILE)
        if k in hbm_ks:
            src_ref = xt_hbm.at[pl.ds(pl.multiple_of(tile0 + src, ROW_TILE), ROW_TILE)]
        else:
            src_ref = xt_ref.at[pl.ds(src, ROW_TILE)]
        return pltpu.make_async_copy(src_ref, xs_ref.at[pl.ds(dst, ROW_TILE)], sem)

    def issue(t8, carry):
        for u in range(ROW_TILE):
            for k in range(TOP_K):
                if k in hbm_ks:
                    row_copy(t8, u, k).start()
                else:
                    row_copy(t8, u, k).start(priority=k % 2)
        return carry

    def drain(t8, carry):
        for u in range(ROW_TILE):
            for k in range(TOP_K):
                row_copy(t8, u, k).wait()
        return carry

    lax.fori_loop(0, td // ROW_TILE, issue, 0)
    lax.fori_loop(0, td // ROW_TILE, drain, 0)


def _dispatch(pend, padded, dest8, xt, n_rows, hbm_ks):
    t = dest8.shape[0] // TOP_K
    td = DISPATCH_TILE
    return pl.pallas_call(
        functools.partial(_dispatch_kernel, hbm_ks=hbm_ks),
        grid_spec=pltpu.PrefetchScalarGridSpec(
            num_scalar_prefetch=2,
            grid=(t // td,),
            in_specs=[pl.BlockSpec((TOP_K * td,), lambda i, pe, pa: (i,), memory_space=pltpu.SMEM),
                      pl.BlockSpec((td * ROW_TILE, LANES), lambda i, pe, pa: (i, 0)),
                      pl.BlockSpec(memory_space=pl.ANY)],
            out_specs=pl.BlockSpec(memory_space=pl.ANY),
            scratch_shapes=[pltpu.VMEM((EXPERT_ROWS * ROW_TILE, LANES), F32),
                            pltpu.SemaphoreType.DMA(()), pltpu.SemaphoreType.DMA(())]),
        out_shape=jax.ShapeDtypeStruct((n_rows * ROW_TILE, LANES), F32),
        compiler_params=_params(("arbitrary",)),
        name="moe_dispatch",
    )(pend, padded, dest8, xt, xt)


def _expert_kernel(blk_e_ref, grp_ref, nxt_ref, n_used_ref, xs_ref, wgu_hbm, bgu_ref, wdn_hbm, bdn_ref,
                   ys_ref, wgu_f32, wdn_f32, wgu_bf, wdn_bf, sems):
    i = pl.program_id(0)
    rb = xs_ref.shape[0] // ROW_TILE
    active = i < n_used_ref[0]
    new_expert = (i == 0) | (blk_e_ref[i] != blk_e_ref[jnp.maximum(i - 1, 0)])

    def weight_copies(e, slot):
        return (pltpu.make_async_copy(wgu_hbm.at[e], wgu_f32.at[slot], sems.at[0, slot]),
                pltpu.make_async_copy(wdn_hbm.at[e], wdn_f32.at[slot], sems.at[1, slot]))

    @pl.when(active & new_expert)
    def _():
        slot = grp_ref[i] % 2
        e = blk_e_ref[i]
        nxt = nxt_ref[i]

        @pl.when(i == 0)
        def _():
            for cp in weight_copies(e, slot):
                cp.start()

        @pl.when(nxt >= 0)
        def _():
            for cp in weight_copies(nxt, 1 - slot):
                cp.start()

        for cp in weight_copies(e, slot):
            cp.wait()

        def cast(c, carry):
            rows = pl.ds(pl.multiple_of(c * CAST_CHUNK, CAST_CHUNK), CAST_CHUNK)
            wgu_bf[rows, :] = wgu_f32[slot, rows, :].astype(BF16)
            wdn_bf[rows, :] = wdn_f32[slot, rows, :].astype(BF16)
            return carry

        lax.fori_loop(0, D_MODEL // CAST_CHUNK, cast, 0)

    @pl.when(active)
    def _():
        xb = _load_rows(xs_ref, rb).astype(BF16)
        h = jnp.dot(xb, wgu_bf[...], preferred_element_type=F32) + bgu_ref[0]
        gate = jnp.minimum(h[:, :D_EXPERT], SWIGLU_LIMIT)
        up = jnp.clip(h[:, D_EXPERT:], -SWIGLU_LIMIT, SWIGLU_LIMIT)
        act = (up + 1.0) * gate * jax.nn.sigmoid(SWIGLU_ALPHA * gate)
        y = jnp.dot(act.astype(BF16), wdn_bf[...], preferred_element_type=F32) + bdn_ref[0]
        _store_rows(ys_ref, y)


def _experts(blk_e, n_used, xs, w_gu, b_gu, w_dn, b_dn):
    rb = EXPERT_ROWS
    n_blk = xs.shape[0] // (rb * ROW_TILE)
    ids = jnp.arange(n_blk, dtype=jnp.int32)
    change = jnp.concatenate([jnp.ones((1,), bool), blk_e[1:] != blk_e[:-1]])
    grp = (jnp.cumsum(change.astype(jnp.int32)) - 1).astype(jnp.int32)
    later = jnp.where(change, ids, n_blk)
    nxt_pos = jnp.concatenate([lax.cummin(later[::-1])[::-1][1:], jnp.full((1,), n_blk, jnp.int32)])
    nxt = jnp.where(nxt_pos < n_blk, blk_e[jnp.minimum(nxt_pos, n_blk - 1)], -1).astype(jnp.int32)
    rows = pl.BlockSpec((rb * ROW_TILE, LANES), lambda i, be, gr, nx, nu: (jnp.minimum(i, nu[0] - 1), 0))
    per_e = lambda a: pl.BlockSpec((1,) + a.shape[1:], lambda i, be, gr, nx, nu: (be[i], 0, 0))
    hbm = pl.BlockSpec(memory_space=pl.ANY)
    return pl.pallas_call(
        _expert_kernel,
        grid_spec=pltpu.PrefetchScalarGridSpec(
            num_scalar_prefetch=4,
            grid=(n_blk,),
            in_specs=[rows, hbm, per_e(b_gu), hbm, per_e(b_dn)],
            out_specs=rows,
            scratch_shapes=[pltpu.VMEM((2,) + w_gu.shape[1:], F32), pltpu.VMEM((2,) + w_dn.shape[1:], F32),
                            pltpu.VMEM(w_gu.shape[1:], BF16), pltpu.VMEM(w_dn.shape[1:], BF16),
                            pltpu.SemaphoreType.DMA((2, 2))]),
        out_shape=jax.ShapeDtypeStruct(xs.shape, F32),
        compiler_params=_params(("arbitrary",)),
        name="moe_experts",
    )(blk_e, grp, nxt, n_used, xs, w_gu, b_gu, w_dn, b_dn)


def _combine_kernel(dest_ref, next_ref, gate_ref, x_ref, ys_ref, g_ref, b_ref, o_ref, buf, sems):
    tc = x_ref.shape[0]
    i = pl.program_id(0)
    slot = i % 2

    def row_copy(d_ref, s, t8, u, k):
        src = pl.multiple_of(d_ref[t8 * (ROW_TILE * TOP_K) + (u * TOP_K + k)], ROW_TILE)
        dst = pl.multiple_of((t8 * ROW_TILE + u) * ROW_TILE, ROW_TILE)
        return pltpu.make_async_copy(ys_ref.at[pl.ds(src, ROW_TILE)], buf.at[s, k, pl.ds(dst, ROW_TILE)],
                                     sems.at[s])

    def issue(d_ref, s):
        def body(t8, carry):
            for u in range(ROW_TILE):
                for k in range(TOP_K):
                    row_copy(d_ref, s, t8, u, k).start(priority=k % 2)
            return carry

        lax.fori_loop(0, tc // ROW_TILE, body, 0)

    @pl.when(i == 0)
    def _():
        issue(dest_ref, 0)

    @pl.when(i + 1 < pl.num_programs(0))
    def _():
        issue(next_ref, 1 - slot)

    def drain(t8, carry):
        for u in range(ROW_TILE):
            for k in range(TOP_K):
                row_copy(dest_ref, slot, t8, u, k).wait()
        return carry

    lax.fori_loop(0, tc // ROW_TILE, drain, 0)
    gates = gate_ref[...]
    moe = gates[:, 0:1] * _load_rows(buf.at[slot, 0], tc)
    for k in range(1, TOP_K):
        moe = moe + gates[:, k:k + 1] * _load_rows(buf.at[slot, k], tc)
    o_ref[...] = _layer_norm(DN_ALPHA * x_ref[...] + moe, g_ref[...], b_ref[...])


def _combine(dest8, gates_tk, x2d, ys, g, b, tc):
    t = x2d.shape[0]
    n = t // tc
    vec = pl.BlockSpec((1, D_MODEL), lambda i: (0, 0))
    return pl.pallas_call(
        _combine_kernel,
        grid=(n,),
        in_specs=[pl.BlockSpec((TOP_K * tc,), lambda i: (i,), memory_space=pltpu.SMEM),
                  pl.BlockSpec((TOP_K * tc,), lambda i: (jnp.minimum(i + 1, n - 1),),
                               memory_space=pltpu.SMEM),
                  pl.BlockSpec((tc, TOP_K), lambda i: (i, 0)),
                  pl.BlockSpec((tc, D_MODEL), lambda i: (i, 0)),
                  pl.BlockSpec(memory_space=pl.ANY), vec, vec],
        out_specs=pl.BlockSpec((tc, D_MODEL), lambda i: (i, 0)),
        out_shape=jax.ShapeDtypeStruct((t, D_MODEL), F32),
        scratch_shapes=[pltpu.VMEM((2, TOP_K, tc * ROW_TILE, LANES), F32), pltpu.SemaphoreType.DMA((2,))],
        compiler_params=_params(("arbitrary",)),
        name="moe_combine_ln",
    )(dest8, dest8, gates_tk, x2d, ys, g, b)


def _moe_ln(x2d, xt, router_w, router_b, w_gu, b_gu, w_dn, b_dn, ln_g, ln_b, tri,
            hbm_ks=(), combine_tile=COMBINE_TILE):
    t = x2d.shape[0]
    rb = EXPERT_ROWS
    idx, gates, rank, cnt = _router(x2d, router_w.T, router_b.reshape(N_EXPERTS, 1), tri)
    counts = cnt[:, 0].astype(jnp.int32)
    padded = ((counts + rb - 1) // rb) * rb
    pend = jnp.cumsum(padded).astype(jnp.int32)
    pstart = pend - padded
    n_blk = (t * TOP_K) // rb + N_EXPERTS
    n_used = pend[-1] // rb
    blk = jnp.minimum(jnp.arange(n_blk, dtype=jnp.int32), n_used - 1)
    blk_e = jnp.sum((blk[:, None] * rb >= pend[None, :]).astype(jnp.int32), axis=1)
    blk_e = jnp.minimum(blk_e, N_EXPERTS - 1).astype(jnp.int32)
    e_ids = jnp.arange(N_EXPERTS, dtype=jnp.int32)[:, None, None]
    dest = rank + jnp.sum(jnp.where(idx[None] == e_ids, pstart[:, None, None], 0), axis=0)
    dest8 = (dest * ROW_TILE).astype(jnp.int32).T.reshape(-1)

    xs = _dispatch(pend, padded.astype(jnp.int32), dest8, xt, n_blk * rb, hbm_ks)
    ys = _experts(blk_e, n_used.reshape(1).astype(jnp.int32), xs,
                  w_gu, b_gu.reshape(N_EXPERTS, 1, -1), w_dn, b_dn.reshape(N_EXPERTS, 1, -1))
    return _combine(dest8, gates.T, x2d, ys, ln_g.reshape(1, -1), ln_b.reshape(1, -1), combine_tile)


def _spread_rope_cols(w):
    half = QK_ROPE_DIM // 2
    z = jnp.zeros(w.shape[:-1] + (LANES // 2 - half,), w.dtype)
    return jnp.concatenate([w[..., :half], z, w[..., half:], z], axis=-1)


def _rope_tables(seq, dim, spread):
    inv_freq = 1.0 / (ROPE_THETA ** (jnp.arange(0, dim, 2, dtype=F32) / dim))
    ang = jnp.arange(seq, dtype=F32)[:, None] * inv_freq[None, :]
    cos, sin = jnp.cos(ang), jnp.sin(ang)
    if spread:
        z = jnp.zeros((seq, LANES // 2 - dim // 2), F32)
        return (jnp.concatenate([cos, z, cos, z], axis=1), jnp.concatenate([-sin, z, sin, z], axis=1))
    return jnp.concatenate([cos, cos], axis=1), jnp.concatenate([-sin, sin], axis=1)


def kernel(x, l0_w_in, l0_q_norm_g, l0_w_q_up, l0_kv_norm_g, l0_w_kv_up, l0_conv_w, l0_conv_b, l0_conv_ln_g, l0_conv_ln_b, l0_w_o, l0_ln1_g, l0_ln1_b, l0_router_w, l0_router_b, l0_w_gu, l0_b_gu, l0_w_dn, l0_b_dn, l0_ln2_g, l0_ln2_b, l1_w_qkv, l1_lambda_q1, l1_lambda_k1, l1_lambda_q2, l1_lambda_k2, l1_subln_g, l1_w_o, l1_ln1_g, l1_ln1_b, l1_router_w, l1_router_b, l1_w_gu, l1_b_gu, l1_w_dn, l1_b_dn, l1_ln2_g, l1_ln2_b):
    b, s, d = x.shape
    t = b * s
    x2d = x.reshape(t, d)
    row = lambda a: a.reshape(1, -1)
    tri = (jnp.arange(TOKEN_TILE)[:, None] < jnp.arange(TOKEN_TILE)[None, :]).astype(BF16)

    o1 = Q_LORA_RANK
    o2 = o1 + KV_LORA_RANK
    o3 = o2 + QK_ROPE_DIM
    w_in = jnp.concatenate([l0_w_in[:, :o2], _spread_rope_cols(l0_w_in[:, o2:o3]), l0_w_in[:, o3:]],
                           axis=1).astype(BF16)
    wq = l0_w_q_up.reshape(Q_LORA_RANK, MLA_HEADS, QK_NOPE_DIM + QK_ROPE_DIM)
    wq = jnp.concatenate([wq[..., :QK_NOPE_DIM], _spread_rope_cols(wq[..., QK_NOPE_DIM:])], axis=-1)
    wq = wq.reshape(Q_LORA_RANK, MLA_HEADS * 2 * LANES).astype(BF16)
    wkv = l0_w_kv_up.reshape(KV_LORA_RANK, MLA_HEADS, QK_NOPE_DIM + V_HEAD_DIM)
    wkv = jnp.concatenate([wkv[..., :QK_NOPE_DIM].reshape(KV_LORA_RANK, -1),
                           wkv[..., QK_NOPE_DIM:].reshape(KV_LORA_RANK, -1)], axis=1).astype(BF16)
    cos0, sin0 = _rope_tables(s, QK_ROPE_DIM, spread=True)
    q, k, v, u = _l0_proj(x2d, w_in, row(l0_q_norm_g), wq, row(l0_kv_norm_g), wkv, cos0, sin0, s)
    hw = MLA_HEADS * 2 * LANES
    attn = _attention(q.reshape(b, s, hw), k.reshape(b, s, hw), v.reshape(b, s, hw),
                      MLA_HEADS, 1, 2 * LANES, 2 * LANES, V_HEAD_DIM)
    conv_w = jnp.concatenate([l0_conv_w, jnp.zeros((1, CONV_CHANNELS), F32)], axis=0)
    uc = _conv_module(u, conv_w, row(l0_conv_b), row(l0_conv_ln_g), row(l0_conv_ln_b), b, s)
    n_attn = MLA_HEADS * V_HEAD_DIM
    w_o = l0_w_o.astype(BF16)
    x2d, xt = _out_ln([attn.reshape(t, n_attn), uc], [w_o[:n_attn], w_o[n_attn:]], x2d,
                      row(l0_ln1_g), row(l0_ln1_b))
    x2d = _moe_ln(x2d, xt, l0_router_w, l0_router_b, l0_w_gu, l0_b_gu, l0_w_dn, l0_b_dn,
                  l0_ln2_g, l0_ln2_b, tri)

    lambda_init = 0.8 - 0.6 * math.exp(-0.3 * 1)
    cos1, sin1 = _rope_tables(s, DIFF_HEAD_DIM, spread=False)
    q, k, v = _l1_proj(x2d, l1_w_qkv.astype(BF16), cos1, sin1, s)
    lam_in = jnp.stack([l1_lambda_q1, l1_lambda_k1, l1_lambda_q2, l1_lambda_k2]).astype(F32)
    dv = 2 * DIFF_HEAD_DIM
    attn = _attention(q.reshape(b, s, d), k.reshape(b, s, d), v.reshape(b, s, d),
                      DIFF_HEADS, 2, DIFF_HEAD_DIM, dv, dv,
                      extra=(lam_in, row(l1_subln_g)), lambda_init=lambda_init)
    x2d, xt = _out_ln([attn.reshape(t, d)], [l1_w_o.astype(BF16)], x2d, row(l1_ln1_g), row(l1_ln1_b))
    x2d = _moe_ln(x2d, xt, l1_router_w, l1_router_b, l1_w_gu, l1_b_gu, l1_w_dn, l1_b_dn,
                  l1_ln2_g, l1_ln2_b, tri, hbm_ks=(2, 3), combine_tile=2 * COMBINE_TILE)
    return x2d.reshape(b, s, d)
```

```python
import functools
import math

import jax
import jax.numpy as jnp
from jax import lax
from jax.experimental import pallas as pl
from jax.experimental.pallas import tpu as pltpu

F32 = jnp.float32
BF16 = jnp.bfloat16

D_MODEL = 1024
DEPTH = 2
MLA_HEADS = 4
QK_NOPE_DIM = 128
QK_ROPE_DIM = 64
V_HEAD_DIM = 128
Q_LORA_RANK = 384
KV_LORA_RANK = 256
CONV_CHANNELS = D_MODEL - MLA_HEADS * V_HEAD_DIM
CONV_WIDTH = 31
DIFF_HEAD_DIM = 128
DIFF_HEADS = D_MODEL // (2 * DIFF_HEAD_DIM)
N_EXPERTS = 32
TOP_K = 4
D_EXPERT = D_MODEL
SWIGLU_LIMIT = 7.0
SWIGLU_ALPHA = 1.702
ROPE_THETA = 10000.0
DN_ALPHA = (2 * DEPTH) ** 0.25
LN_EPS = 1e-5
RMS_EPS = 1e-6
MASK_VALUE = -1e30

LANES = 128
ROW_TILE = 8
TOKEN_TILE = 512
ATTN_Q_TILE = 2048
ATTN_K_TILE = 1024
ATTN_CHAINS = 8
CONV_TILE = 512
CONV_HALO = 32
CONV_CHUNK = 64
EXPERT_ROWS = 512
CAST_CHUNK = 128
VMEM_LIMIT = 56 * 1024 * 1024


def _params(sem, vmem=VMEM_LIMIT):
    return pltpu.CompilerParams(dimension_semantics=sem, vmem_limit_bytes=vmem)


def _layer_norm(r, g, b):
    mu = jnp.mean(r, axis=-1, keepdims=True)
    d = r - mu
    var = jnp.mean(d * d, axis=-1, keepdims=True)
    return d * lax.rsqrt(var + LN_EPS) * g + b


def _rms_norm(x, g):
    return x * lax.rsqrt(jnp.mean(x * x, axis=-1, keepdims=True) + RMS_EPS) * g


def _rope(x, cos, sin):
    return x * cos + pltpu.roll(x, 64, 1) * sin


def _repeat_lanes(x, n):
    return x if n == 1 else jnp.concatenate([x] * n, axis=1)


def _load_rows(ref, n_rows):
    return jnp.concatenate([ref[pl.ds(j, n_rows, stride=ROW_TILE), :] for j in range(ROW_TILE)], axis=1)


def _store_rows(ref, val):
    n_rows = val.shape[0]
    for j in range(ROW_TILE):
        ref[pl.ds(j, n_rows, stride=ROW_TILE), :] = val[:, j * LANES:(j + 1) * LANES]


def _l0_proj_kernel(x_ref, win_ref, qg_ref, wq_ref, kvg_ref, wkv_ref, cos_ref, sin_ref,
                    q_ref, k_ref, v_ref, u_ref):
    tm = x_ref.shape[0]
    xb = x_ref[...].astype(BF16)
    proj = jnp.dot(xb, win_ref[...], preferred_element_type=F32)
    o1 = Q_LORA_RANK
    o2 = o1 + KV_LORA_RANK
    o3 = o2 + LANES
    o4 = o3 + CONV_CHANNELS
    u_ref[...] = proj[:, o3:o4] * jax.nn.sigmoid(proj[:, o4:])
    cos = cos_ref[...]
    sin = sin_ref[...]
    scale = (QK_NOPE_DIM + QK_ROPE_DIM) ** -0.5
    qn = _rms_norm(proj[:, :o1], qg_ref[...])
    qup = jnp.dot(qn.astype(BF16), wq_ref[...], preferred_element_type=F32)
    kvn = _rms_norm(proj[:, o1:o2], kvg_ref[...])
    kvup = jnp.dot(kvn.astype(BF16), wkv_ref[...], preferred_element_type=F32)
    k_rope = _rope(proj[:, o2:o3], cos, sin).astype(BF16)
    ones_col = jnp.where(lax.broadcasted_iota(jnp.int32, (tm, LANES), 1) == 0, 1.0, 0.0).astype(BF16)
    for h in range(MLA_HEADS):
        c = 2 * LANES * h
        q_ref[:, c:c + LANES] = (qup[:, c:c + LANES] * scale).astype(BF16)
        q_rope = _rope(qup[:, c + LANES:c + 2 * LANES], cos, sin)
        q_ref[:, c + LANES:c + 2 * LANES] = (q_rope * scale).astype(BF16)
        k_ref[:, c:c + LANES] = kvup[:, LANES * h:LANES * (h + 1)].astype(BF16)
        k_ref[:, c + LANES:c + 2 * LANES] = k_rope
        vh = MLA_HEADS * LANES + LANES * h
        v_ref[:, c:c + LANES] = kvup[:, vh:vh + LANES].astype(BF16)
        v_ref[:, c + LANES:c + 2 * LANES] = ones_col


def _l0_proj(x2d, w_in, qg, wq, kvg, wkv, cos, sin, seq):
    t = x2d.shape[0]
    tm = TOKEN_TILE
    n_pos = seq // tm
    full = lambda a: pl.BlockSpec(a.shape, lambda i: (0,) * a.ndim)
    row = lambda w: pl.BlockSpec((tm, w), lambda i: (i, 0))
    pos = pl.BlockSpec((tm, LANES), lambda i: (i % n_pos, 0))
    hw = MLA_HEADS * 2 * LANES
    wide = jax.ShapeDtypeStruct((t, hw), BF16)
    return pl.pallas_call(
        _l0_proj_kernel,
        grid=(t // tm,),
        in_specs=[row(D_MODEL), full(w_in), full(qg), full(wq), full(kvg), full(wkv), pos, pos],
        out_specs=[row(hw), row(hw), row(hw), row(CONV_CHANNELS)],
        out_shape=[wide, wide, wide, jax.ShapeDtypeStruct((t, CONV_CHANNELS), F32)],
        compiler_params=_params(("parallel",)),
        name="l0_proj",
    )(x2d, w_in, qg, wq, kvg, wkv, cos, sin)


def _l1_proj_kernel(x_ref, w_ref, cos_ref, sin_ref, q_ref, k_ref, v_ref):
    xb = x_ref[...].astype(BF16)
    cos = cos_ref[...]
    sin = sin_ref[...]
    scale = DIFF_HEAD_DIM ** -0.5
    qk_w = DIFF_HEADS * 2 * DIFF_HEAD_DIM
    q = jnp.dot(xb, w_ref[:, :qk_w], preferred_element_type=F32)
    for j in range(qk_w // LANES):
        c = j * LANES
        q_ref[:, c:c + LANES] = (_rope(q[:, c:c + LANES], cos, sin) * scale).astype(BF16)
    k = jnp.dot(xb, w_ref[:, qk_w:2 * qk_w], preferred_element_type=F32)
    for j in range(qk_w // LANES):
        c = j * LANES
        k_ref[:, c:c + LANES] = _rope(k[:, c:c + LANES], cos, sin).astype(BF16)
    v_ref[...] = jnp.dot(xb, w_ref[:, 2 * qk_w:], preferred_element_type=F32).astype(BF16)


def _l1_proj(x2d, w_qkv, cos, sin, seq):
    t = x2d.shape[0]
    tm = TOKEN_TILE
    n_pos = seq // tm
    row = lambda w: pl.BlockSpec((tm, w), lambda i: (i, 0))
    pos = pl.BlockSpec((tm, LANES), lambda i: (i % n_pos, 0))
    out = jax.ShapeDtypeStruct((t, D_MODEL), BF16)
    return pl.pallas_call(
        _l1_proj_kernel,
        grid=(t // tm,),
        in_specs=[row(D_MODEL), pl.BlockSpec(w_qkv.shape, lambda i: (0, 0)), pos, pos],
        out_specs=[row(D_MODEL)] * 3,
        out_shape=[out, out, out],
        compiler_params=_params(("parallel",)),
        name="l1_proj",
    )(x2d, w_qkv, cos, sin)


def _attn_kernel(*refs, n_maps, dk, tq, tk, n_chains, sum_col, lambda_init):
    if n_maps == 2:
        q_ref, k_ref, v_ref, lam_ref, g_ref, o_ref, m_sc, l_sc, acc_sc = refs
    else:
        q_ref, k_ref, v_ref, o_ref, m_sc, l_sc, acc_sc = refs
    qi = pl.program_id(2)
    rs = tq // n_chains
    dv = v_ref.shape[2]
    m_sc[...] = jnp.full(m_sc.shape, MASK_VALUE, F32)
    l_sc[...] = jnp.zeros(l_sc.shape, F32)
    acc_sc[...] = jnp.zeros(acc_sc.shape, F32)

    def chain(m, r, k, v, mask):
        rows = slice(r * rs, (r + 1) * rs)
        q = q_ref[0, rows, m * dk:(m + 1) * dk]
        s = lax.dot_general(q, k, (((1,), (1,)), ((), ())), preferred_element_type=F32)
        if mask is not None:
            s = jnp.where(mask, s, MASK_VALUE)
        m_prev = m_sc[m, rows, :]
        m_new = jnp.maximum(m_prev, jnp.max(s, axis=1, keepdims=True))
        p = jnp.exp(s - _repeat_lanes(m_new, s.shape[1] // LANES))
        a = jnp.exp(m_prev - m_new)
        if sum_col is None:
            l_sc[m, rows, :] = a * l_sc[m, rows, :] + jnp.sum(p, axis=1, keepdims=True)
        acc_sc[m, rows, :] = (_repeat_lanes(a, dv // LANES) * acc_sc[m, rows, :]
                              + jnp.dot(p.astype(BF16), v, preferred_element_type=F32))
        m_sc[m, rows, :] = m_new

    def off_diagonal(j, carry):
        start = pl.multiple_of(j * tk, tk)
        v = v_ref[0, pl.ds(start, tk), :]
        for m in range(n_maps):
            k = k_ref[0, pl.ds(start, tk), m * dk:(m + 1) * dk]
            for r in range(n_chains):
                chain(m, r, k, v, None)
        return carry

    lax.fori_loop(0, qi * (tq // tk), off_diagonal, 0)
    base = pl.multiple_of(qi * tq, tq)
    for r in range(n_chains):
        nk = (r + 1) * rs
        v = v_ref[0, pl.ds(base, nk), :]
        row = lax.broadcasted_iota(jnp.int32, (rs, nk), 0) + r * rs
        col = lax.broadcasted_iota(jnp.int32, (rs, nk), 1)
        mask = col <= row
        for m in range(n_maps):
            k = k_ref[0, pl.ds(base, nk), m * dk:(m + 1) * dk]
            chain(m, r, k, v, mask)

    if n_maps == 1:
        acc = acc_sc[0]
        o_ref[0] = (acc[:, :sum_col] / acc[:, sum_col:sum_col + 1]).astype(o_ref.dtype)
    else:
        lam_in = lam_ref[...]
        lam = (jnp.exp(jnp.sum(lam_in[0:1] * lam_in[1:2], axis=1, keepdims=True))
               - jnp.exp(jnp.sum(lam_in[2:3] * lam_in[3:4], axis=1, keepdims=True)) + lambda_init)
        a = acc_sc[0] / l_sc[0][:, 0:1] - lam * (acc_sc[1] / l_sc[1][:, 0:1])
        o_ref[0] = (_rms_norm(a, g_ref[...]) * (1.0 - lambda_init)).astype(o_ref.dtype)


def _attention(q, k, v, n_heads, n_maps, dk, dv_in, dv_out, extra=(), lambda_init=0.0):
    b, s, _ = q.shape
    tq, tk = ATTN_Q_TILE, ATTN_K_TILE
    qw = n_maps * dk
    sum_col = dv_out if dv_in > dv_out else None
    extra_specs = [pl.BlockSpec(e.shape, lambda bi, h, qi: (0, 0)) for e in extra]
    kern = functools.partial(_attn_kernel, n_maps=n_maps, dk=dk, tq=tq, tk=tk, n_chains=ATTN_CHAINS,
                             sum_col=sum_col, lambda_init=lambda_init)
    return pl.pallas_call(
        kern,
        grid=(b, n_heads, s // tq),
        in_specs=[pl.BlockSpec((1, tq, qw), lambda bi, h, qi: (bi, qi, h)),
                  pl.BlockSpec((1, s, qw), lambda bi, h, qi: (bi, 0, h)),
                  pl.BlockSpec((1, s, dv_in), lambda bi, h, qi: (bi, 0, h))] + extra_specs,
        out_specs=pl.BlockSpec((1, tq, dv_out), lambda bi, h, qi: (bi, qi, h)),
        out_shape=jax.ShapeDtypeStruct((b, s, n_heads * dv_out), BF16),
        scratch_shapes=[pltpu.VMEM((n_maps, tq, LANES), F32), pltpu.VMEM((n_maps, tq, LANES), F32),
                        pltpu.VMEM((n_maps, tq, dv_in), F32)],
        compiler_params=_params(("parallel", "parallel", "arbitrary")),
        name="attention_%dmap" % n_maps,
    )(q, k, v, *extra)


def _conv_kernel(u_ref, w_ref, cb_ref, g_ref, b_ref, o_ref, ext, win_sc):
    ts = u_ref.shape[0]
    si = pl.program_id(1)

    @pl.when(si == 0)
    def _():
        ext[0:CONV_HALO, :] = jnp.zeros((CONV_HALO, CONV_CHANNELS), F32)

    @pl.when(si > 0)
    def _():
        ext[0:CONV_HALO, :] = ext[ts:ts + CONV_HALO, :]

    ext[CONV_HALO:CONV_HALO + ts, :] = u_ref[...]
    first = CONV_HALO - (CONV_WIDTH - 1)
    for c in range(ts // CONV_CHUNK):
        r0 = c * CONV_CHUNK
        acc = jnp.zeros((CONV_CHUNK, CONV_CHANNELS), F32)
        for shift in range(ROW_TILE):
            offs = [first + j - shift for j in range(CONV_WIDTH) if (first + j) % ROW_TILE == shift]
            n_win = max(offs) + CONV_CHUNK
            win = win_sc.at[(c * ROW_TILE + shift) % 2]
            win[0:n_win, :] = ext[r0 + shift:r0 + shift + n_win, :]
            for off in offs:
                j = off + shift - first
                acc = acc + w_ref[j:j + 1, :] * win[off:off + CONV_CHUNK, :]
        y = _layer_norm(acc + cb_ref[...], g_ref[...], b_ref[...])
        o_ref[r0:r0 + CONV_CHUNK, :] = (y * jax.nn.sigmoid(y)).astype(o_ref.dtype)


def _conv_module(u2d, conv_w, conv_b, ln_g, ln_b, batch, seq):
    ts = CONV_TILE
    n_s = seq // ts
    vec = pl.BlockSpec((1, CONV_CHANNELS), lambda bi, si: (0, 0))
    return pl.pallas_call(
        _conv_kernel,
        grid=(batch, n_s),
        in_specs=[pl.BlockSpec((ts, CONV_CHANNELS), lambda bi, si: (bi * n_s + si, 0)),
                  pl.BlockSpec(conv_w.shape, lambda bi, si: (0, 0)), vec, vec, vec],
        out_specs=pl.BlockSpec((ts, CONV_CHANNELS), lambda bi, si: (bi * n_s + si, 0)),
        out_shape=jax.ShapeDtypeStruct((batch * seq, CONV_CHANNELS), BF16),
        scratch_shapes=[pltpu.VMEM((ts + CONV_HALO, CONV_CHANNELS), F32),
                        pltpu.VMEM((2, CONV_CHUNK + CONV_HALO, CONV_CHANNELS), F32)],
        compiler_params=_params(("arbitrary", "arbitrary")),
        name="conv_module",
    )(u2d, conv_w, conv_b, ln_g, ln_b)


def _out_ln_kernel(*refs, n_in):
    a_refs = refs[:n_in]
    w_refs = refs[n_in:2 * n_in]
    x_ref, g_ref, b_ref, o_ref, ot_ref = refs[2 * n_in:]
    mix = jnp.dot(a_refs[0][...], w_refs[0][...], preferred_element_type=F32)
    for a_ref, w_ref in zip(a_refs[1:], w_refs[1:]):
        mix = mix + jnp.dot(a_ref[...], w_ref[...], preferred_element_type=F32)
    y = _layer_norm(DN_ALPHA * x_ref[...] + mix, g_ref[...], b_ref[...])
    o_ref[...] = y
    _store_rows(ot_ref, y)


def _out_ln(acts, weights, x2d, g, b):
    t = x2d.shape[0]
    tm = TOKEN_TILE
    vec = pl.BlockSpec((1, D_MODEL), lambda i: (0, 0))
    return pl.pallas_call(
        functools.partial(_out_ln_kernel, n_in=len(acts)),
        grid=(t // tm,),
        in_specs=([pl.BlockSpec((tm, a.shape[1]), lambda i: (i, 0)) for a in acts]
                  + [pl.BlockSpec(w.shape, lambda i: (0, 0)) for w in weights]
                  + [pl.BlockSpec((tm, D_MODEL), lambda i: (i, 0)), vec, vec]),
        out_specs=[pl.BlockSpec((tm, D_MODEL), lambda i: (i, 0)),
                   pl.BlockSpec((tm * ROW_TILE, LANES), lambda i: (i, 0))],
        out_shape=[jax.ShapeDtypeStruct((t, D_MODEL), F32),
                   jax.ShapeDtypeStruct((t * ROW_TILE, LANES), F32)],
        compiler_params=_params(("parallel",)),
        name="out_proj_ln",
    )(*acts, *weights, x2d, g, b)


def _router_kernel(x_ref, rw_ref, rb_ref, tri_ref, low_ref, gate_ref, lrow_ref, before_ref, tcnt_ref,
                   cnt_ref):
    tm = x_ref.shape[0]

    @pl.when(pl.program_id(0) == 0)
    def _():
        cnt_ref[...] = jnp.zeros(cnt_ref.shape, F32)

    logits = lax.dot_general(rw_ref[...], x_ref[...], (((1,), (1,)), ((), ())),
                             precision=lax.Precision.HIGHEST,
                             preferred_element_type=F32) + rb_ref[...]
    e_iota = lax.broadcasted_iota(jnp.int32, (N_EXPERTS, tm), 0)
    vals, sels = [], []
    work = logits
    for k in range(TOP_K):
        top = jnp.max(work, axis=0, keepdims=True)
        idx = jnp.min(jnp.where(work == top, e_iota, N_EXPERTS), axis=0, keepdims=True)
        sel = e_iota == idx
        vals.append(top)
        sels.append(sel)
        work = jnp.where(sel, -jnp.inf, work)
    exps = [jnp.exp(v - vals[0]) for v in vals]
    denom = exps[0] + exps[1] + exps[2] + exps[3]
    for k in range(TOP_K):
        gate_ref[k:k + 1, :] = exps[k] / denom
    chosen = jnp.where(sels[0] | sels[1] | sels[2] | sels[3], 1.0, 0.0)
    earlier = jnp.dot(chosen.astype(BF16), tri_ref[...], preferred_element_type=F32)
    before = cnt_ref[...]
    tile_cnt = jnp.sum(chosen, axis=1, keepdims=True)
    local = jnp.dot(low_ref[...], jnp.broadcast_to(tile_cnt, before.shape),
                    precision=lax.Precision.HIGHEST, preferred_element_type=F32)
    lbase = earlier + local[:, 0:1]
    for k in range(TOP_K):
        lrow = jnp.sum(jnp.where(sels[k], lbase, 0.0), axis=0, keepdims=True)
        lrow_ref[k:k + 1, :] = lrow.astype(jnp.int32)
    before_ref[0] = before
    tcnt_ref[0] = jnp.broadcast_to(tile_cnt, before.shape)
    cnt_ref[...] = before + tile_cnt


def _router(x2d, rw_t, rb, tri):
    t = x2d.shape[0]
    tm = TOKEN_TILE
    n = t // tm
    kt = pl.BlockSpec((TOP_K, tm), lambda i: (0, i))
    per_tile = pl.BlockSpec((1, N_EXPERTS, LANES), lambda i: (i, 0, 0))
    low = (jnp.arange(N_EXPERTS)[:, None] > jnp.arange(N_EXPERTS)[None, :]).astype(F32)
    ints = jax.ShapeDtypeStruct((TOP_K, t), jnp.int32)
    tiles = jax.ShapeDtypeStruct((n, N_EXPERTS, LANES), F32)
    return pl.pallas_call(
        _router_kernel,
        grid=(n,),
        in_specs=[pl.BlockSpec((tm, D_MODEL), lambda i: (i, 0)),
                  pl.BlockSpec(rw_t.shape, lambda i: (0, 0)),
                  pl.BlockSpec(rb.shape, lambda i: (0, 0)),
                  pl.BlockSpec(tri.shape, lambda i: (0, 0)),
                  pl.BlockSpec(low.shape, lambda i: (0, 0))],
        out_specs=[kt, kt, per_tile, per_tile, pl.BlockSpec((N_EXPERTS, LANES), lambda i: (0, 0))],
        out_shape=[jax.ShapeDtypeStruct((TOP_K, t), F32), ints, tiles, tiles,
                   jax.ShapeDtypeStruct((N_EXPERTS, LANES), F32)],
        compiler_params=_params(("arbitrary",)),
        name="router",
    )(x2d, rw_t, rb, tri, low)


def _dispatch_kernel(pend_ref, padded_ref, rdst_ref, rsrc_ref, rlen_ref, lrow_ref, xt_ref, xs_ref,
                     zeros, stage, zsem, sems):
    td = lrow_ref.shape[0] // TOP_K
    zrows = zeros.shape[0]
    i = pl.program_id(0)
    slot = i % 2

    @pl.when(pl.program_id(0) == 0)
    def _():
        zeros[...] = jnp.zeros(zeros.shape, F32)

        def zero_copy(e):
            start = pl.multiple_of(pend_ref[e] * ROW_TILE - zrows, zrows)
            return pltpu.make_async_copy(zeros, xs_ref.at[pl.ds(start, zrows)], zsem)

        def start_zero(e, carry):
            @pl.when(padded_ref[e] > 0)
            def _():
                zero_copy(e).start()
            return carry

        def wait_zero(e, carry):
            @pl.when(padded_ref[e] > 0)
            def _():
                zero_copy(e).wait()
            return carry

        lax.fori_loop(0, N_EXPERTS, start_zero, 0)
        lax.fori_loop(0, N_EXPERTS, wait_zero, 0)

    def start_runs(tile, s):
        def per_expert(e, carry):
            n_rows = rlen_ref[tile * N_EXPERTS + e]
            src = rsrc_ref[tile * N_EXPERTS + e]
            dst = rdst_ref[tile * N_EXPERTS + e]
            for bit in reversed(range(td.bit_length())):
                size = (1 << bit) * ROW_TILE
                has = (n_rows & (1 << bit)) != 0

                @pl.when(has)
                def _(src=src, dst=dst, size=size):
                    pltpu.make_async_copy(
                        stage.at[s, pl.ds(pl.multiple_of(src, ROW_TILE), size)],
                        xs_ref.at[pl.ds(pl.multiple_of(dst, ROW_TILE), size)], sems.at[s]).start()

                step = jnp.where(has, size, 0)
                src = src + step
                dst = dst + step
            return carry

        lax.fori_loop(0, N_EXPERTS, per_expert, 0)

    def wait_tile(s):
        pltpu.make_async_copy(stage.at[s], xs_ref.at[pl.ds(0, stage.shape[1])], sems.at[s]).wait()

    @pl.when(i > 0)
    def _():
        wait_tile(1 - slot)

    def compact(t8, carry):
        for u in range(ROW_TILE):
            row = xt_ref[pl.ds(pl.multiple_of((t8 * ROW_TILE + u) * ROW_TILE, ROW_TILE), ROW_TILE), :]
            for k in range(TOP_K):
                dst = pl.multiple_of(lrow_ref[t8 * (ROW_TILE * TOP_K) + (u * TOP_K + k)], ROW_TILE)
                stage[slot, pl.ds(dst, ROW_TILE), :] = row
        return carry

    lax.fori_loop(0, td // ROW_TILE, compact, 0)
    start_runs(i, slot)

    @pl.when(i == pl.num_programs(0) - 1)
    def _():
        wait_tile(slot)


def _dispatch(pend, padded, rdst, rsrc, rlen, lrow8, xt, n_rows):
    t = lrow8.shape[0] // TOP_K
    td = TOKEN_TILE
    smem = lambda i, *_: (i,)
    return pl.pallas_call(
        _dispatch_kernel,
        grid_spec=pltpu.PrefetchScalarGridSpec(
            num_scalar_prefetch=5,
            grid=(t // td,),
            in_specs=[pl.BlockSpec((TOP_K * td,), smem, memory_space=pltpu.SMEM),
                      pl.BlockSpec((td * ROW_TILE, LANES), lambda i, *_: (i, 0))],
            out_specs=pl.BlockSpec(memory_space=pl.ANY),
            scratch_shapes=[pltpu.VMEM((EXPERT_ROWS * ROW_TILE, LANES), F32),
                            pltpu.VMEM((2, TOP_K * td * ROW_TILE, LANES), F32),
                            pltpu.SemaphoreType.DMA(()), pltpu.SemaphoreType.DMA((2,))]),
        out_shape=jax.ShapeDtypeStruct((n_rows * ROW_TILE, LANES), F32),
        compiler_params=_params(("arbitrary",)),
        name="moe_dispatch",
    )(pend, padded, rdst, rsrc, rlen, lrow8, xt)


def _expert_kernel(blk_e_ref, grp_ref, nxt_ref, n_used_ref, xs_ref, wgu_hbm, bgu_ref, wdn_hbm, bdn_ref,
                   ys_ref, wgu_f32, wdn_f32, wgu_bf, wdn_bf, sems):
    i = pl.program_id(0)
    rb = xs_ref.shape[0] // ROW_TILE
    active = i < n_used_ref[0]
    new_expert = (i == 0) | (blk_e_ref[i] != blk_e_ref[jnp.maximum(i - 1, 0)])

    def weight_copies(e, slot):
        return (pltpu.make_async_copy(wgu_hbm.at[e], wgu_f32.at[slot], sems.at[0, slot]),
                pltpu.make_async_copy(wdn_hbm.at[e], wdn_f32.at[slot], sems.at[1, slot]))

    @pl.when(active & new_expert)
    def _():
        slot = grp_ref[i] % 2
        e = blk_e_ref[i]
        nxt = nxt_ref[i]

        @pl.when(i == 0)
        def _():
            for cp in weight_copies(e, slot):
                cp.start()

        @pl.when(nxt >= 0)
        def _():
            for cp in weight_copies(nxt, 1 - slot):
                cp.start()

        for cp in weight_copies(e, slot):
            cp.wait()

        def cast(c, carry):
            rows = pl.ds(pl.multiple_of(c * CAST_CHUNK, CAST_CHUNK), CAST_CHUNK)
            wgu_bf[rows, :] = wgu_f32[slot, rows, :].astype(BF16)
            wdn_bf[rows, :] = wdn_f32[slot, rows, :].astype(BF16)
            return carry

        lax.fori_loop(0, D_MODEL // CAST_CHUNK, cast, 0)

    @pl.when(active)
    def _():
        xb = _load_rows(xs_ref, rb).astype(BF16)
        h = jnp.dot(xb, wgu_bf[...], preferred_element_type=F32) + bgu_ref[0]
        gate = jnp.minimum(h[:, :D_EXPERT], SWIGLU_LIMIT)
        up = jnp.clip(h[:, D_EXPERT:], -SWIGLU_LIMIT, SWIGLU_LIMIT)
        act = (up + 1.0) * gate * jax.nn.sigmoid(SWIGLU_ALPHA * gate)
        y = jnp.dot(act.astype(BF16), wdn_bf[...], preferred_element_type=F32) + bdn_ref[0]
        _store_rows(ys_ref, y)


def _experts(blk_e, n_used, xs, w_gu, b_gu, w_dn, b_dn):
    rb = EXPERT_ROWS
    n_blk = xs.shape[0] // (rb * ROW_TILE)
    ids = jnp.arange(n_blk, dtype=jnp.int32)
    change = jnp.concatenate([jnp.ones((1,), bool), blk_e[1:] != blk_e[:-1]])
    grp = (jnp.cumsum(change.astype(jnp.int32)) - 1).astype(jnp.int32)
    later = jnp.where(change, ids, n_blk)
    nxt_pos = jnp.concatenate([lax.cummin(later[::-1])[::-1][1:], jnp.full((1,), n_blk, jnp.int32)])
    nxt = jnp.where(nxt_pos < n_blk, blk_e[jnp.minimum(nxt_pos, n_blk - 1)], -1).astype(jnp.int32)
    rows = pl.BlockSpec((rb * ROW_TILE, LANES), lambda i, be, gr, nx, nu: (jnp.minimum(i, nu[0] - 1), 0))
    per_e = lambda a: pl.BlockSpec((1,) + a.shape[1:], lambda i, be, gr, nx, nu: (be[i], 0, 0))
    hbm = pl.BlockSpec(memory_space=pl.ANY)
    return pl.pallas_call(
        _expert_kernel,
        grid_spec=pltpu.PrefetchScalarGridSpec(
            num_scalar_prefetch=4,
            grid=(n_blk,),
            in_specs=[rows, hbm, per_e(b_gu), hbm, per_e(b_dn)],
            out_specs=rows,
            scratch_shapes=[pltpu.VMEM((2,) + w_gu.shape[1:], F32), pltpu.VMEM((2,) + w_dn.shape[1:], F32),
                            pltpu.VMEM(w_gu.shape[1:], BF16), pltpu.VMEM(w_dn.shape[1:], BF16),
                            pltpu.SemaphoreType.DMA((2, 2))]),
        out_shape=jax.ShapeDtypeStruct(xs.shape, F32),
        compiler_params=_params(("arbitrary",)),
        name="moe_experts",
    )(blk_e, grp, nxt, n_used, xs, w_gu, b_gu, w_dn, b_dn)


def _combine_kernel(rdst_ref, rsrc_ref, rlen_ref, lrow_ref, gate_ref, x_ref, ys_ref, g_ref, b_ref, o_ref,
                    stage, moe_sc, sems):
    td = x_ref.shape[0]
    i = pl.program_id(0)
    slot = i % 2

    def start_runs(tile, s):
        def per_expert(e, carry):
            n_rows = rlen_ref[tile * N_EXPERTS + e]
            src = rdst_ref[tile * N_EXPERTS + e]
            dst = rsrc_ref[tile * N_EXPERTS + e]
            for bit in reversed(range(td.bit_length())):
                size = (1 << bit) * ROW_TILE
                has = (n_rows & (1 << bit)) != 0

                @pl.when(has)
                def _(src=src, dst=dst, size=size):
                    pltpu.make_async_copy(
                        ys_ref.at[pl.ds(pl.multiple_of(src, ROW_TILE), size)],
                        stage.at[s, pl.ds(pl.multiple_of(dst, ROW_TILE), size)], sems.at[s]).start()

                step = jnp.where(has, size, 0)
                src = src + step
                dst = dst + step
            return carry

        lax.fori_loop(0, N_EXPERTS, per_expert, 0)

    @pl.when(i == 0)
    def _():
        start_runs(0, 0)

    @pl.when(i + 1 < pl.num_programs(0))
    def _():
        start_runs(i + 1, 1 - slot)

    pltpu.make_async_copy(ys_ref.at[pl.ds(0, stage.shape[1])], stage.at[slot], sems.at[slot]).wait()

    def reduce_rows(t8, carry):
        for u in range(ROW_TILE):
            a0 = (t8 * ROW_TILE + u) * TOP_K
            acc = None
            for k in range(TOP_K):
                row = stage[slot, pl.ds(pl.multiple_of(lrow_ref[a0 + k], ROW_TILE), ROW_TILE), :]
                term = gate_ref[a0 + k] * row
                acc = term if acc is None else acc + term
            moe_sc[pl.ds(pl.multiple_of((t8 * ROW_TILE + u) * ROW_TILE, ROW_TILE), ROW_TILE), :] = acc
        return carry

    lax.fori_loop(0, td // ROW_TILE, reduce_rows, 0)
    moe = _load_rows(moe_sc, td)
    o_ref[...] = _layer_norm(DN_ALPHA * x_ref[...] + moe, g_ref[...], b_ref[...])


def _combine(rdst, rsrc, rlen, lrow8, gates_flat, x2d, ys, g, b):
    t = x2d.shape[0]
    td = TOKEN_TILE
    vec = pl.BlockSpec((1, D_MODEL), lambda i, *_: (0, 0))
    smem = pl.BlockSpec((TOP_K * td,), lambda i, *_: (i,), memory_space=pltpu.SMEM)
    return pl.pallas_call(
        _combine_kernel,
        grid_spec=pltpu.PrefetchScalarGridSpec(
            num_scalar_prefetch=3,
            grid=(t // td,),
            in_specs=[smem, smem, pl.BlockSpec((td, D_MODEL), lambda i, *_: (i, 0)),
                      pl.BlockSpec(memory_space=pl.ANY), vec, vec],
            out_specs=pl.BlockSpec((td, D_MODEL), lambda i, *_: (i, 0)),
            scratch_shapes=[pltpu.VMEM((2, TOP_K * td * ROW_TILE, LANES), F32),
                            pltpu.VMEM((td * ROW_TILE, LANES), F32), pltpu.SemaphoreType.DMA((2,))]),
        out_shape=jax.ShapeDtypeStruct((t, D_MODEL), F32),
        compiler_params=_params(("arbitrary",)),
        name="moe_combine_ln",
    )(rdst, rsrc, rlen, lrow8, gates_flat, x2d, ys, g, b)


def _moe_ln(x2d, xt, router_w, router_b, w_gu, b_gu, w_dn, b_dn, ln_g, ln_b, tri):
    t = x2d.shape[0]
    rb = EXPERT_ROWS
    gates, lrow, before, tile_cnt, cnt = _router(
        x2d, router_w.T, router_b.reshape(N_EXPERTS, 1), tri)
    counts = cnt[:, 0].astype(jnp.int32)
    padded = ((counts + rb - 1) // rb) * rb
    pend = jnp.cumsum(padded).astype(jnp.int32)
    pstart = pend - padded
    n_blk = (t * TOP_K) // rb + N_EXPERTS
    n_used = pend[-1] // rb
    blk = jnp.minimum(jnp.arange(n_blk, dtype=jnp.int32), n_used - 1)
    blk_e = jnp.sum((blk[:, None] * rb >= pend[None, :]).astype(jnp.int32), axis=1)
    blk_e = jnp.minimum(blk_e, N_EXPERTS - 1).astype(jnp.int32)
    lrow8 = (lrow * ROW_TILE).astype(jnp.int32).T.reshape(-1)
    run_len = tile_cnt[:, :, 0].astype(jnp.int32)
    run_src = (jnp.cumsum(run_len, axis=1) - run_len) * ROW_TILE
    run_dst = (pstart[None, :] + before[:, :, 0].astype(jnp.int32)) * ROW_TILE
    flat = lambda a: a.reshape(-1).astype(jnp.int32)

    run_dst, run_src, run_len = flat(run_dst), flat(run_src), flat(run_len)

    xs = _dispatch(pend, padded.astype(jnp.int32), run_dst, run_src, run_len, lrow8, xt, n_blk * rb)
    ys = _experts(blk_e, n_used.reshape(1).astype(jnp.int32), xs,
                  w_gu, b_gu.reshape(N_EXPERTS, 1, -1), w_dn, b_dn.reshape(N_EXPERTS, 1, -1))
    return _combine(run_dst, run_src, run_len, lrow8, gates.T.reshape(-1), x2d, ys,
                    ln_g.reshape(1, -1), ln_b.reshape(1, -1))


def _spread_rope_cols(w):
    half = QK_ROPE_DIM // 2
    z = jnp.zeros(w.shape[:-1] + (LANES // 2 - half,), w.dtype)
    return jnp.concatenate([w[..., :half], z, w[..., half:], z], axis=-1)


def _rope_tables(seq, dim, spread):
    inv_freq = 1.0 / (ROPE_THETA ** (jnp.arange(0, dim, 2, dtype=F32) / dim))
    ang = jnp.arange(seq, dtype=F32)[:, None] * inv_freq[None, :]
    cos, sin = jnp.cos(ang), jnp.sin(ang)
    if spread:
        z = jnp.zeros((seq, LANES // 2 - dim // 2), F32)
        return (jnp.concatenate([cos, z, cos, z], axis=1), jnp.concatenate([-sin, z, sin, z], axis=1))
    return jnp.concatenate([cos, cos], axis=1), jnp.concatenate([-sin, sin], axis=1)


def kernel(x, l0_w_in, l0_q_norm_g, l0_w_q_up, l0_kv_norm_g, l0_w_kv_up, l0_conv_w, l0_conv_b, l0_conv_ln_g, l0_conv_ln_b, l0_w_o, l0_ln1_g, l0_ln1_b, l0_router_w, l0_router_b, l0_w_gu, l0_b_gu, l0_w_dn, l0_b_dn, l0_ln2_g, l0_ln2_b, l1_w_qkv, l1_lambda_q1, l1_lambda_k1, l1_lambda_q2, l1_lambda_k2, l1_subln_g, l1_w_o, l1_ln1_g, l1_ln1_b, l1_router_w, l1_router_b, l1_w_gu, l1_b_gu, l1_w_dn, l1_b_dn, l1_ln2_g, l1_ln2_b):
    b, s, d = x.shape
    t = b * s
    x2d = x.reshape(t, d)
    row = lambda a: a.reshape(1, -1)
    tri = (jnp.arange(TOKEN_TILE)[:, None] < jnp.arange(TOKEN_TILE)[None, :]).astype(BF16)

    o1 = Q_LORA_RANK
    o2 = o1 + KV_LORA_RANK
    o3 = o2 + QK_ROPE_DIM
    w_in = jnp.concatenate([l0_w_in[:, :o2], _spread_rope_cols(l0_w_in[:, o2:o3]), l0_w_in[:, o3:]],
                           axis=1).astype(BF16)
    wq = l0_w_q_up.reshape(Q_LORA_RANK, MLA_HEADS, QK_NOPE_DIM + QK_ROPE_DIM)
    wq = jnp.concatenate([wq[..., :QK_NOPE_DIM], _spread_rope_cols(wq[..., QK_NOPE_DIM:])], axis=-1)
    wq = wq.reshape(Q_LORA_RANK, MLA_HEADS * 2 * LANES).astype(BF16)
    wkv = l0_w_kv_up.reshape(KV_LORA_RANK, MLA_HEADS, QK_NOPE_DIM + V_HEAD_DIM)
    wkv = jnp.concatenate([wkv[..., :QK_NOPE_DIM].reshape(KV_LORA_RANK, -1),
                           wkv[..., QK_NOPE_DIM:].reshape(KV_LORA_RANK, -1)], axis=1).astype(BF16)
    cos0, sin0 = _rope_tables(s, QK_ROPE_DIM, spread=True)
    q, k, v, u = _l0_proj(x2d, w_in, row(l0_q_norm_g), wq, row(l0_kv_norm_g), wkv, cos0, sin0, s)
    hw = MLA_HEADS * 2 * LANES
    attn = _attention(q.reshape(b, s, hw), k.reshape(b, s, hw), v.reshape(b, s, hw),
                      MLA_HEADS, 1, 2 * LANES, 2 * LANES, V_HEAD_DIM)
    conv_w = jnp.concatenate([l0_conv_w, jnp.zeros((1, CONV_CHANNELS), F32)], axis=0)
    uc = _conv_module(u, conv_w, row(l0_conv_b), row(l0_conv_ln_g), row(l0_conv_ln_b), b, s)
    n_attn = MLA_HEADS * V_HEAD_DIM
    w_o = l0_w_o.astype(BF16)
    x2d, xt = _out_ln([attn.reshape(t, n_attn), uc], [w_o[:n_attn], w_o[n_attn:]], x2d,
                      row(l0_ln1_g), row(l0_ln1_b))
    x2d = _moe_ln(x2d, xt, l0_router_w, l0_router_b, l0_w_gu, l0_b_gu, l0_w_dn, l0_b_dn,
                  l0_ln2_g, l0_ln2_b, tri)

    lambda_init = 0.8 - 0.6 * math.exp(-0.3 * 1)
    cos1, sin1 = _rope_tables(s, DIFF_HEAD_DIM, spread=False)
    q, k, v = _l1_proj(x2d, l1_w_qkv.astype(BF16), cos1, sin1, s)
    lam_in = jnp.stack([l1_lambda_q1, l1_lambda_k1, l1_lambda_q2, l1_lambda_k2]).astype(F32)
    dv = 2 * DIFF_HEAD_DIM
    attn = _attention(q.reshape(b, s, d), k.reshape(b, s, d), v.reshape(b, s, d),
                      DIFF_HEADS, 2, DIFF_HEAD_DIM, dv, dv,
                      extra=(lam_in, row(l1_subln_g)), lambda_init=lambda_init)
    x2d, xt = _out_ln([attn.reshape(t, d)], [l1_w_o.astype(BF16)], x2d, row(l1_ln1_g), row(l1_ln1_b))
    x2d = _moe_ln(x2d, xt, l1_router_w, l1_router_b, l1_w_gu, l1_b_gu, l1_w_dn, l1_b_dn,
                  l1_ln2_g, l1_ln2_b, tri)
    return x2d.reshape(b, s, d)
```

```python
import functools
import math

import jax
import jax.numpy as jnp
from jax import lax
from jax.experimental import pallas as pl
from jax.experimental.pallas import tpu as pltpu

F32 = jnp.float32
BF16 = jnp.bfloat16

D_MODEL = 1024
DEPTH = 2
MLA_HEADS = 4
QK_NOPE_DIM = 128
QK_ROPE_DIM = 64
V_HEAD_DIM = 128
Q_LORA_RANK = 384
KV_LORA_RANK = 256
CONV_CHANNELS = D_MODEL - MLA_HEADS * V_HEAD_DIM
CONV_WIDTH = 31
DIFF_HEAD_DIM = 128
DIFF_HEADS = D_MODEL // (2 * DIFF_HEAD_DIM)
N_EXPERTS = 32
TOP_K = 4
D_EXPERT = D_MODEL
SWIGLU_LIMIT = 7.0
SWIGLU_ALPHA = 1.702
ROPE_THETA = 10000.0
DN_ALPHA = (2 * DEPTH) ** 0.25
LN_EPS = 1e-5
RMS_EPS = 1e-6
MASK_VALUE = -1e30

LANES = 128
ROW_TILE = 8
TOKEN_TILE = 512
ATTN_Q_TILE = 2048
ATTN_K_TILE = 1024
ATTN_CHAINS = 8
CONV_TILE = 512
CONV_HALO = 32
CONV_CHUNK = 64
EXPERT_ROWS = 512
CAST_CHUNK = 128
VMEM_LIMIT = 56 * 1024 * 1024


def _params(sem, vmem=VMEM_LIMIT):
    return pltpu.CompilerParams(dimension_semantics=sem, vmem_limit_bytes=vmem)


def _layer_norm(r, g, b):
    mu = jnp.mean(r, axis=-1, keepdims=True)
    d = r - mu
    var = jnp.mean(d * d, axis=-1, keepdims=True)
    return d * lax.rsqrt(var + LN_EPS) * g + b


def _rms_norm(x, g):
    return x * lax.rsqrt(jnp.mean(x * x, axis=-1, keepdims=True) + RMS_EPS) * g


def _rope(x, cos, sin):
    return x * cos + pltpu.roll(x, 64, 1) * sin


def _repeat_lanes(x, n):
    return x if n == 1 else jnp.concatenate([x] * n, axis=1)


def _load_rows(ref, n_rows):
    return jnp.concatenate([ref[pl.ds(j, n_rows, stride=ROW_TILE), :] for j in range(ROW_TILE)], axis=1)


def _store_rows(ref, val):
    n_rows = val.shape[0]
    for j in range(ROW_TILE):
        ref[pl.ds(j, n_rows, stride=ROW_TILE), :] = val[:, j * LANES:(j + 1) * LANES]


def _l0_proj_kernel(x_ref, win_ref, qg_ref, wq_ref, kvg_ref, wkv_ref, cos_ref, sin_ref,
                    q_ref, k_ref, v_ref, u_ref):
    tm = x_ref.shape[0]
    xb = x_ref[...].astype(BF16)
    proj = jnp.dot(xb, win_ref[...], preferred_element_type=F32)
    o1 = Q_LORA_RANK
    o2 = o1 + KV_LORA_RANK
    o3 = o2 + LANES
    o4 = o3 + CONV_CHANNELS
    u_ref[...] = proj[:, o3:o4] * jax.nn.sigmoid(proj[:, o4:])
    cos = cos_ref[...]
    sin = sin_ref[...]
    scale = (QK_NOPE_DIM + QK_ROPE_DIM) ** -0.5
    qn = _rms_norm(proj[:, :o1], qg_ref[...])
    qup = jnp.dot(qn.astype(BF16), wq_ref[...], preferred_element_type=F32)
    kvn = _rms_norm(proj[:, o1:o2], kvg_ref[...])
    kvup = jnp.dot(kvn.astype(BF16), wkv_ref[...], preferred_element_type=F32)
    k_rope = _rope(proj[:, o2:o3], cos, sin).astype(BF16)
    ones_col = jnp.where(lax.broadcasted_iota(jnp.int32, (tm, LANES), 1) == 0, 1.0, 0.0).astype(BF16)
    for h in range(MLA_HEADS):
        c = 2 * LANES * h
        q_ref[:, c:c + LANES] = (qup[:, c:c + LANES] * scale).astype(BF16)
        q_rope = _rope(qup[:, c + LANES:c + 2 * LANES], cos, sin)
        q_ref[:, c + LANES:c + 2 * LANES] = (q_rope * scale).astype(BF16)
        k_ref[:, c:c + LANES] = kvup[:, LANES * h:LANES * (h + 1)].astype(BF16)
        k_ref[:, c + LANES:c + 2 * LANES] = k_rope
        vh = MLA_HEADS * LANES + LANES * h
        v_ref[:, c:c + LANES] = kvup[:, vh:vh + LANES].astype(BF16)
        v_ref[:, c + LANES:c + 2 * LANES] = ones_col


def _l0_proj(x2d, w_in, qg, wq, kvg, wkv, cos, sin, seq):
    t = x2d.shape[0]
    tm = TOKEN_TILE
    n_pos = seq // tm
    full = lambda a: pl.BlockSpec(a.shape, lambda i: (0,) * a.ndim)
    row = lambda w: pl.BlockSpec((tm, w), lambda i: (i, 0))
    pos = pl.BlockSpec((tm, LANES), lambda i: (i % n_pos, 0))
    hw = MLA_HEADS * 2 * LANES
    wide = jax.ShapeDtypeStruct((t, hw), BF16)
    return pl.pallas_call(
        _l0_proj_kernel,
        grid=(t // tm,),
        in_specs=[row(D_MODEL), full(w_in), full(qg), full(wq), full(kvg), full(wkv), pos, pos],
        out_specs=[row(hw), row(hw), row(hw), row(CONV_CHANNELS)],
        out_shape=[wide, wide, wide, jax.ShapeDtypeStruct((t, CONV_CHANNELS), F32)],
        compiler_params=_params(("parallel",)),
        name="l0_proj",
    )(x2d, w_in, qg, wq, kvg, wkv, cos, sin)


def _l1_proj_kernel(x_ref, w_ref, cos_ref, sin_ref, q_ref, k_ref, v_ref):
    xb = x_ref[...].astype(BF16)
    cos = cos_ref[...]
    sin = sin_ref[...]
    scale = DIFF_HEAD_DIM ** -0.5
    qk_w = DIFF_HEADS * 2 * DIFF_HEAD_DIM
    q = jnp.dot(xb, w_ref[:, :qk_w], preferred_element_type=F32)
    for j in range(qk_w // LANES):
        c = j * LANES
        q_ref[:, c:c + LANES] = (_rope(q[:, c:c + LANES], cos, sin) * scale).astype(BF16)
    k = jnp.dot(xb, w_ref[:, qk_w:2 * qk_w], preferred_element_type=F32)
    for j in range(qk_w // LANES):
        c = j * LANES
        k_ref[:, c:c + LANES] = _rope(k[:, c:c + LANES], cos, sin).astype(BF16)
    v_ref[...] = jnp.dot(xb, w_ref[:, 2 * qk_w:], preferred_element_type=F32).astype(BF16)


def _l1_proj(x2d, w_qkv, cos, sin, seq):
    t = x2d.shape[0]
    tm = TOKEN_TILE
    n_pos = seq // tm
    row = lambda w: pl.BlockSpec((tm, w), lambda i: (i, 0))
    pos = pl.BlockSpec((tm, LANES), lambda i: (i % n_pos, 0))
    out = jax.ShapeDtypeStruct((t, D_MODEL), BF16)
    return pl.pallas_call(
        _l1_proj_kernel,
        grid=(t // tm,),
        in_specs=[row(D_MODEL), pl.BlockSpec(w_qkv.shape, lambda i: (0, 0)), pos, pos],
        out_specs=[row(D_MODEL)] * 3,
        out_shape=[out, out, out],
        compiler_params=_params(("parallel",)),
        name="l1_proj",
    )(x2d, w_qkv, cos, sin)


def _attn_kernel(*refs, n_maps, dk, tq, tk, n_chains, sum_col, lambda_init):
    if n_maps == 2:
        q_ref, k_ref, v_ref, lam_ref, g_ref, o_ref, m_sc, l_sc, acc_sc = refs
    else:
        q_ref, k_ref, v_ref, o_ref, m_sc, l_sc, acc_sc = refs
    qi = pl.program_id(2)
    rs = tq // n_chains
    dv = v_ref.shape[2]
    m_sc[...] = jnp.full(m_sc.shape, MASK_VALUE, F32)
    l_sc[...] = jnp.zeros(l_sc.shape, F32)
    acc_sc[...] = jnp.zeros(acc_sc.shape, F32)

    def chain(m, r, k, v, mask):
        rows = slice(r * rs, (r + 1) * rs)
        q = q_ref[0, rows, m * dk:(m + 1) * dk]
        s = lax.dot_general(q, k, (((1,), (1,)), ((), ())), preferred_element_type=F32)
        if mask is not None:
            s = jnp.where(mask, s, MASK_VALUE)
        m_prev = m_sc[m, rows, :]
        m_new = jnp.maximum(m_prev, jnp.max(s, axis=1, keepdims=True))
        p = jnp.exp(s - _repeat_lanes(m_new, s.shape[1] // LANES))
        a = jnp.exp(m_prev - m_new)
        if sum_col is None:
            l_sc[m, rows, :] = a * l_sc[m, rows, :] + jnp.sum(p, axis=1, keepdims=True)
        acc_sc[m, rows, :] = (_repeat_lanes(a, dv // LANES) * acc_sc[m, rows, :]
                              + jnp.dot(p.astype(BF16), v, preferred_element_type=F32))
        m_sc[m, rows, :] = m_new

    def off_diagonal(j, carry):
        start = pl.multiple_of(j * tk, tk)
        v = v_ref[0, pl.ds(start, tk), :]
        for m in range(n_maps):
            k = k_ref[0, pl.ds(start, tk), m * dk:(m + 1) * dk]
            for r in range(n_chains):
                chain(m, r, k, v, None)
        return carry

    lax.fori_loop(0, qi * (tq // tk), off_diagonal, 0)
    base = pl.multiple_of(qi * tq, tq)
    for r in range(n_chains):
        nk = (r + 1) * rs
        v = v_ref[0, pl.ds(base, nk), :]
        row = lax.broadcasted_iota(jnp.int32, (rs, nk), 0) + r * rs
        col = lax.broadcasted_iota(jnp.int32, (rs, nk), 1)
        mask = col <= row
        for m in range(n_maps):
            k = k_ref[0, pl.ds(base, nk), m * dk:(m + 1) * dk]
            chain(m, r, k, v, mask)

    if n_maps == 1:
        acc = acc_sc[0]
        o_ref[0] = (acc[:, :sum_col] / acc[:, sum_col:sum_col + 1]).astype(o_ref.dtype)
    else:
        lam_in = lam_ref[...]
        lam = (jnp.exp(jnp.sum(lam_in[0:1] * lam_in[1:2], axis=1, keepdims=True))
               - jnp.exp(jnp.sum(lam_in[2:3] * lam_in[3:4], axis=1, keepdims=True)) + lambda_init)
        a = acc_sc[0] / l_sc[0][:, 0:1] - lam * (acc_sc[1] / l_sc[1][:, 0:1])
        o_ref[0] = (_rms_norm(a, g_ref[...]) * (1.0 - lambda_init)).astype(o_ref.dtype)


def _attention(q, k, v, n_heads, n_maps, dk, dv_in, dv_out, extra=(), lambda_init=0.0):
    b, s, _ = q.shape
    tq, tk = ATTN_Q_TILE, ATTN_K_TILE
    qw = n_maps * dk
    sum_col = dv_out if dv_in > dv_out else None
    extra_specs = [pl.BlockSpec(e.shape, lambda bi, h, qi: (0, 0)) for e in extra]
    kern = functools.partial(_attn_kernel, n_maps=n_maps, dk=dk, tq=tq, tk=tk, n_chains=ATTN_CHAINS,
                             sum_col=sum_col, lambda_init=lambda_init)
    return pl.pallas_call(
        kern,
        grid=(b, n_heads, s // tq),
        in_specs=[pl.BlockSpec((1, tq, qw), lambda bi, h, qi: (bi, qi, h)),
                  pl.BlockSpec((1, s, qw), lambda bi, h, qi: (bi, 0, h)),
                  pl.BlockSpec((1, s, dv_in), lambda bi, h, qi: (bi, 0, h))] + extra_specs,
        out_specs=pl.BlockSpec((1, tq, dv_out), lambda bi, h, qi: (bi, qi, h)),
        out_shape=jax.ShapeDtypeStruct((b, s, n_heads * dv_out), BF16),
        scratch_shapes=[pltpu.VMEM((n_maps, tq, LANES), F32), pltpu.VMEM((n_maps, tq, LANES), F32),
                        pltpu.VMEM((n_maps, tq, dv_in), F32)],
        compiler_params=_params(("parallel", "parallel", "arbitrary")),
        name="attention_%dmap" % n_maps,
    )(q, k, v, *extra)


def _conv_kernel(u_ref, w_ref, cb_ref, g_ref, b_ref, o_ref, ext, win_sc):
    ts = u_ref.shape[0]
    si = pl.program_id(1)

    @pl.when(si == 0)
    def _():
        ext[0:CONV_HALO, :] = jnp.zeros((CONV_HALO, CONV_CHANNELS), F32)

    @pl.when(si > 0)
    def _():
        ext[0:CONV_HALO, :] = ext[ts:ts + CONV_HALO, :]

    ext[CONV_HALO:CONV_HALO + ts, :] = u_ref[...]
    first = CONV_HALO - (CONV_WIDTH - 1)
    for c in range(ts // CONV_CHUNK):
        r0 = c * CONV_CHUNK
        acc = jnp.zeros((CONV_CHUNK, CONV_CHANNELS), F32)
        for shift in range(ROW_TILE):
            offs = [first + j - shift for j in range(CONV_WIDTH) if (first + j) % ROW_TILE == shift]
            n_win = max(offs) + CONV_CHUNK
            win = win_sc.at[(c * ROW_TILE + shift) % 2]
            win[0:n_win, :] = ext[r0 + shift:r0 + shift + n_win, :]
            for off in offs:
                j = off + shift - first
                acc = acc + w_ref[j:j + 1, :] * win[off:off + CONV_CHUNK, :]
        y = _layer_norm(acc + cb_ref[...], g_ref[...], b_ref[...])
        o_ref[r0:r0 + CONV_CHUNK, :] = (y * jax.nn.sigmoid(y)).astype(o_ref.dtype)


def _conv_module(u2d, conv_w, conv_b, ln_g, ln_b, batch, seq):
    ts = CONV_TILE
    n_s = seq // ts
    vec = pl.BlockSpec((1, CONV_CHANNELS), lambda bi, si: (0, 0))
    return pl.pallas_call(
        _conv_kernel,
        grid=(batch, n_s),
        in_specs=[pl.BlockSpec((ts, CONV_CHANNELS), lambda bi, si: (bi * n_s + si, 0)),
                  pl.BlockSpec(conv_w.shape, lambda bi, si: (0, 0)), vec, vec, vec],
        out_specs=pl.BlockSpec((ts, CONV_CHANNELS), lambda bi, si: (bi * n_s + si, 0)),
        out_shape=jax.ShapeDtypeStruct((batch * seq, CONV_CHANNELS), BF16),
        scratch_shapes=[pltpu.VMEM((ts + CONV_HALO, CONV_CHANNELS), F32),
                        pltpu.VMEM((2, CONV_CHUNK + CONV_HALO, CONV_CHANNELS), F32)],
        compiler_params=_params(("arbitrary", "arbitrary")),
        name="conv_module",
    )(u2d, conv_w, conv_b, ln_g, ln_b)


def _out_ln_kernel(*refs, n_in):
    a_refs = refs[:n_in]
    w_refs = refs[n_in:2 * n_in]
    x_ref, g_ref, b_ref, o_ref, ot_ref = refs[2 * n_in:]
    mix = jnp.dot(a_refs[0][...], w_refs[0][...], preferred_element_type=F32)
    for a_ref, w_ref in zip(a_refs[1:], w_refs[1:]):
        mix = mix + jnp.dot(a_ref[...], w_ref[...], preferred_element_type=F32)
    y = _layer_norm(DN_ALPHA * x_ref[...] + mix, g_ref[...], b_ref[...])
    o_ref[...] = y
    _store_rows(ot_ref, y)


def _out_ln(acts, weights, x2d, g, b):
    t = x2d.shape[0]
    tm = TOKEN_TILE
    vec = pl.BlockSpec((1, D_MODEL), lambda i: (0, 0))
    return pl.pallas_call(
        functools.partial(_out_ln_kernel, n_in=len(acts)),
        grid=(t // tm,),
        in_specs=([pl.BlockSpec((tm, a.shape[1]), lambda i: (i, 0)) for a in acts]
                  + [pl.BlockSpec(w.shape, lambda i: (0, 0)) for w in weights]
                  + [pl.BlockSpec((tm, D_MODEL), lambda i: (i, 0)), vec, vec]),
        out_specs=[pl.BlockSpec((tm, D_MODEL), lambda i: (i, 0)),
                   pl.BlockSpec((tm * ROW_TILE, LANES), lambda i: (i, 0))],
        out_shape=[jax.ShapeDtypeStruct((t, D_MODEL), F32),
                   jax.ShapeDtypeStruct((t * ROW_TILE, LANES), F32)],
        compiler_params=_params(("parallel",)),
        name="out_proj_ln",
    )(*acts, *weights, x2d, g, b)


def _router_kernel(x_ref, rw_ref, rb_ref, tri_ref, low_ref, gate_ref, lrow_ref, before_ref, tcnt_ref,
                   cnt_ref):
    tm = x_ref.shape[0]

    @pl.when(pl.program_id(0) == 0)
    def _():
        cnt_ref[...] = jnp.zeros(cnt_ref.shape, F32)

    logits = lax.dot_general(rw_ref[...], x_ref[...], (((1,), (1,)), ((), ())),
                             precision=lax.Precision.HIGHEST,
                             preferred_element_type=F32) + rb_ref[...]
    e_iota = lax.broadcasted_iota(jnp.int32, (N_EXPERTS, tm), 0)
    vals, sels = [], []
    work = logits
    for k in range(TOP_K):
        top = jnp.max(work, axis=0, keepdims=True)
        idx = jnp.min(jnp.where(work == top, e_iota, N_EXPERTS), axis=0, keepdims=True)
        sel = e_iota == idx
        vals.append(top)
        sels.append(sel)
        work = jnp.where(sel, -jnp.inf, work)
    exps = [jnp.exp(v - vals[0]) for v in vals]
    denom = exps[0] + exps[1] + exps[2] + exps[3]
    for k in range(TOP_K):
        gate_ref[k:k + 1, :] = exps[k] / denom
    chosen = jnp.where(sels[0] | sels[1] | sels[2] | sels[3], 1.0, 0.0)
    earlier = jnp.dot(chosen.astype(BF16), tri_ref[...], preferred_element_type=F32)
    before = cnt_ref[...]
    tile_cnt = jnp.sum(chosen, axis=1, keepdims=True)
    local = jnp.dot(low_ref[...], jnp.broadcast_to(tile_cnt, before.shape),
                    precision=lax.Precision.HIGHEST, preferred_element_type=F32)
    lbase = earlier + local[:, 0:1]
    for k in range(TOP_K):
        lrow = jnp.sum(jnp.where(sels[k], lbase, 0.0), axis=0, keepdims=True)
        lrow_ref[k:k + 1, :] = lrow.astype(jnp.int32)
    before_ref[0] = before
    tcnt_ref[0] = jnp.broadcast_to(tile_cnt, before.shape)
    cnt_ref[...] = before + tile_cnt


def _router(x2d, rw_t, rb, tri):
    t = x2d.shape[0]
    tm = TOKEN_TILE
    n = t // tm
    kt = pl.BlockSpec((TOP_K, tm), lambda i: (0, i))
    per_tile = pl.BlockSpec((1, N_EXPERTS, LANES), lambda i: (i, 0, 0))
    low = (jnp.arange(N_EXPERTS)[:, None] > jnp.arange(N_EXPERTS)[None, :]).astype(F32)
    ints = jax.ShapeDtypeStruct((TOP_K, t), jnp.int32)
    tiles = jax.ShapeDtypeStruct((n, N_EXPERTS, LANES), F32)
    return pl.pallas_call(
        _router_kernel,
        grid=(n,),
        in_specs=[pl.BlockSpec((tm, D_MODEL), lambda i: (i, 0)),
                  pl.BlockSpec(rw_t.shape, lambda i: (0, 0)),
                  pl.BlockSpec(rb.shape, lambda i: (0, 0)),
                  pl.BlockSpec(tri.shape, lambda i: (0, 0)),
                  pl.BlockSpec(low.shape, lambda i: (0, 0))],
        out_specs=[kt, kt, per_tile, per_tile, pl.BlockSpec((N_EXPERTS, LANES), lambda i: (0, 0))],
        out_shape=[jax.ShapeDtypeStruct((TOP_K, t), F32), ints, tiles, tiles,
                   jax.ShapeDtypeStruct((N_EXPERTS, LANES), F32)],
        compiler_params=_params(("arbitrary",)),
        name="router",
    )(x2d, rw_t, rb, tri, low)


def _dispatch_kernel(pend_ref, padded_ref, rdst_ref, rsrc_ref, rlen_ref, lrow_ref, xt_ref, xs_ref,
                     zeros, stage, zsem, sems):
    td = lrow_ref.shape[0] // TOP_K
    zrows = zeros.shape[0]
    i = pl.program_id(0)
    slot = i % 2

    @pl.when(pl.program_id(0) == 0)
    def _():
        zeros[...] = jnp.zeros(zeros.shape, F32)

        def zero_copy(e):
            start = pl.multiple_of(pend_ref[e] * ROW_TILE - zrows, zrows)
            return pltpu.make_async_copy(zeros, xs_ref.at[pl.ds(start, zrows)], zsem)

        def start_zero(e, carry):
            @pl.when(padded_ref[e] > 0)
            def _():
                zero_copy(e).start()
            return carry

        def wait_zero(e, carry):
            @pl.when(padded_ref[e] > 0)
            def _():
                zero_copy(e).wait()
            return carry

        lax.fori_loop(0, N_EXPERTS, start_zero, 0)
        lax.fori_loop(0, N_EXPERTS, wait_zero, 0)

    def start_runs(tile, s):
        def per_expert(e, carry):
            n_rows = rlen_ref[tile * N_EXPERTS + e]
            src = rsrc_ref[tile * N_EXPERTS + e]
            dst = rdst_ref[tile * N_EXPERTS + e]
            for bit in reversed(range(td.bit_length())):
                size = (1 << bit) * ROW_TILE
                has = (n_rows & (1 << bit)) != 0

                @pl.when(has)
                def _(src=src, dst=dst, size=size):
                    pltpu.make_async_copy(
                        stage.at[s, pl.ds(pl.multiple_of(src, ROW_TILE), size)],
                        xs_ref.at[pl.ds(pl.multiple_of(dst, ROW_TILE), size)], sems.at[s]).start()

                step = jnp.where(has, size, 0)
                src = src + step
                dst = dst + step
            return carry

        lax.fori_loop(0, N_EXPERTS, per_expert, 0)

    def wait_tile(s):
        pltpu.make_async_copy(stage.at[s], xs_ref.at[pl.ds(0, stage.shape[1])], sems.at[s]).wait()

    def compact(t8, carry):
        for u in range(ROW_TILE):
            row = xt_ref[pl.ds(pl.multiple_of((t8 * ROW_TILE + u) * ROW_TILE, ROW_TILE), ROW_TILE), :]
            for k in range(TOP_K):
                dst = pl.multiple_of(lrow_ref[t8 * (ROW_TILE * TOP_K) + (u * TOP_K + k)], ROW_TILE)
                stage[slot, pl.ds(dst, ROW_TILE), :] = row
        return carry

    lax.fori_loop(0, td // ROW_TILE, compact, 0)
    start_runs(i, slot)

    @pl.when(i > 0)
    def _():
        wait_tile(1 - slot)

    @pl.when(i == pl.num_programs(0) - 1)
    def _():
        wait_tile(slot)


def _dispatch(pend, padded, rdst, rsrc, rlen, lrow8, xt, n_rows):
    t = lrow8.shape[0] // TOP_K
    td = TOKEN_TILE
    smem = lambda i, *_: (i,)
    return pl.pallas_call(
        _dispatch_kernel,
        grid_spec=pltpu.PrefetchScalarGridSpec(
            num_scalar_prefetch=5,
            grid=(t // td,),
            in_specs=[pl.BlockSpec((TOP_K * td,), smem, memory_space=pltpu.SMEM),
                      pl.BlockSpec((td * ROW_TILE, LANES), lambda i, *_: (i, 0))],
            out_specs=pl.BlockSpec(memory_space=pl.ANY),
            scratch_shapes=[pltpu.VMEM((EXPERT_ROWS * ROW_TILE, LANES), F32),
                            pltpu.VMEM((2, TOP_K * td * ROW_TILE, LANES), F32),
                            pltpu.SemaphoreType.DMA(()), pltpu.SemaphoreType.DMA((2,))]),
        out_shape=jax.ShapeDtypeStruct((n_rows * ROW_TILE, LANES), F32),
        compiler_params=_params(("arbitrary",)),
        name="moe_dispatch",
    )(pend, padded, rdst, rsrc, rlen, lrow8, xt)


def _expert_kernel(blk_e_ref, grp_ref, nxt_ref, n_used_ref, xs_ref, wgu_hbm, bgu_ref, wdn_hbm, bdn_ref,
                   ys_ref, wgu_f32, wdn_f32, wgu_bf, wdn_bf, sems):
    i = pl.program_id(0)
    rb = xs_ref.shape[0] // ROW_TILE
    active = i < n_used_ref[0]
    new_expert = (i == 0) | (blk_e_ref[i] != blk_e_ref[jnp.maximum(i - 1, 0)])

    def weight_copies(e, slot):
        return (pltpu.make_async_copy(wgu_hbm.at[e], wgu_f32.at[slot], sems.at[0, slot]),
                pltpu.make_async_copy(wdn_hbm.at[e], wdn_f32.at[slot], sems.at[1, slot]))

    @pl.when(active & new_expert)
    def _():
        slot = grp_ref[i] % 2
        e = blk_e_ref[i]
        nxt = nxt_ref[i]

        @pl.when(i == 0)
        def _():
            for cp in weight_copies(e, slot):
                cp.start()

        @pl.when(nxt >= 0)
        def _():
            for cp in weight_copies(nxt, 1 - slot):
                cp.start()

        for cp in weight_copies(e, slot):
            cp.wait()

        def cast(c, carry):
            rows = pl.ds(pl.multiple_of(c * CAST_CHUNK, CAST_CHUNK), CAST_CHUNK)
            wgu_bf[rows, :] = wgu_f32[slot, rows, :].astype(BF16)
            wdn_bf[rows, :] = wdn_f32[slot, rows, :].astype(BF16)
            return carry

        lax.fori_loop(0, D_MODEL // CAST_CHUNK, cast, 0)

    @pl.when(active)
    def _():
        xb = _load_rows(xs_ref, rb).astype(BF16)
        h = jnp.dot(xb, wgu_bf[...], preferred_element_type=F32) + bgu_ref[0]
        gate = jnp.minimum(h[:, :D_EXPERT], SWIGLU_LIMIT)
        up = jnp.clip(h[:, D_EXPERT:], -SWIGLU_LIMIT, SWIGLU_LIMIT)
        act = (up + 1.0) * gate * jax.nn.sigmoid(SWIGLU_ALPHA * gate)
        y = jnp.dot(act.astype(BF16), wdn_bf[...], preferred_element_type=F32) + bdn_ref[0]
        _store_rows(ys_ref, y)


def _experts(blk_e, n_used, xs, w_gu, b_gu, w_dn, b_dn):
    rb = EXPERT_ROWS
    n_blk = xs.shape[0] // (rb * ROW_TILE)
    ids = jnp.arange(n_blk, dtype=jnp.int32)
    change = jnp.concatenate([jnp.ones((1,), bool), blk_e[1:] != blk_e[:-1]])
    grp = (jnp.cumsum(change.astype(jnp.int32)) - 1).astype(jnp.int32)
    later = jnp.where(change, ids, n_blk)
    nxt_pos = jnp.concatenate([lax.cummin(later[::-1])[::-1][1:], jnp.full((1,), n_blk, jnp.int32)])
    nxt = jnp.where(nxt_pos < n_blk, blk_e[jnp.minimum(nxt_pos, n_blk - 1)], -1).astype(jnp.int32)
    rows = pl.BlockSpec((rb * ROW_TILE, LANES), lambda i, be, gr, nx, nu: (jnp.minimum(i, nu[0] - 1), 0))
    per_e = lambda a: pl.BlockSpec((1,) + a.shape[1:], lambda i, be, gr, nx, nu: (be[i], 0, 0))
    hbm = pl.BlockSpec(memory_space=pl.ANY)
    return pl.pallas_call(
        _expert_kernel,
        grid_spec=pltpu.PrefetchScalarGridSpec(
            num_scalar_prefetch=4,
            grid=(n_blk,),
            in_specs=[rows, hbm, per_e(b_gu), hbm, per_e(b_dn)],
            out_specs=rows,
            scratch_shapes=[pltpu.VMEM((2,) + w_gu.shape[1:], F32), pltpu.VMEM((2,) + w_dn.shape[1:], F32),
                            pltpu.VMEM(w_gu.shape[1:], BF16), pltpu.VMEM(w_dn.shape[1:], BF16),
                            pltpu.SemaphoreType.DMA((2, 2))]),
        out_shape=jax.ShapeDtypeStruct(xs.shape, F32),
        compiler_params=_params(("arbitrary",)),
        name="moe_experts",
    )(blk_e, grp, nxt, n_used, xs, w_gu, b_gu, w_dn, b_dn)


def _combine_kernel(rdst_ref, rsrc_ref, rlen_ref, lrow_ref, gate_ref, x_ref, ys_ref, g_ref, b_ref, o_ref,
                    stage, moe_sc, sems):
    td = x_ref.shape[0]
    i = pl.program_id(0)
    slot = i % 2

    def start_runs(tile, s):
        def per_expert(e, carry):
            n_rows = rlen_ref[tile * N_EXPERTS + e]
            src = rdst_ref[tile * N_EXPERTS + e]
            dst = rsrc_ref[tile * N_EXPERTS + e]
            for bit in reversed(range(td.bit_length())):
                size = (1 << bit) * ROW_TILE
                has = (n_rows & (1 << bit)) != 0

                @pl.when(has)
                def _(src=src, dst=dst, size=size):
                    pltpu.make_async_copy(
                        ys_ref.at[pl.ds(pl.multiple_of(src, ROW_TILE), size)],
                        stage.at[s, pl.ds(pl.multiple_of(dst, ROW_TILE), size)], sems.at[s]).start()

                step = jnp.where(has, size, 0)
                src = src + step
                dst = dst + step
            return carry

        lax.fori_loop(0, N_EXPERTS, per_expert, 0)

    @pl.when(i == 0)
    def _():
        start_runs(0, 0)

    @pl.when(i + 1 < pl.num_programs(0))
    def _():
        start_runs(i + 1, 1 - slot)

    pltpu.make_async_copy(ys_ref.at[pl.ds(0, stage.shape[1])], stage.at[slot], sems.at[slot]).wait()

    def reduce_rows(t8, carry):
        for u in range(ROW_TILE):
            a0 = (t8 * ROW_TILE + u) * TOP_K
            acc = None
            for k in range(TOP_K):
                row = stage[slot, pl.ds(pl.multiple_of(lrow_ref[a0 + k], ROW_TILE), ROW_TILE), :]
                term = gate_ref[a0 + k] * row
                acc = term if acc is None else acc + term
            moe_sc[pl.ds(pl.multiple_of((t8 * ROW_TILE + u) * ROW_TILE, ROW_TILE), ROW_TILE), :] = acc
        return carry

    lax.fori_loop(0, td // ROW_TILE, reduce_rows, 0)
    moe = _load_rows(moe_sc, td)
    o_ref[...] = _layer_norm(DN_ALPHA * x_ref[...] + moe, g_ref[...], b_ref[...])


def _combine(rdst, rsrc, rlen, lrow8, gates_flat, x2d, ys, g, b):
    t = x2d.shape[0]
    td = TOKEN_TILE
    vec = pl.BlockSpec((1, D_MODEL), lambda i, *_: (0, 0))
    smem = pl.BlockSpec((TOP_K * td,), lambda i, *_: (i,), memory_space=pltpu.SMEM)
    return pl.pallas_call(
        _combine_kernel,
        grid_spec=pltpu.PrefetchScalarGridSpec(
            num_scalar_prefetch=3,
            grid=(t // td,),
            in_specs=[smem, smem, pl.BlockSpec((td, D_MODEL), lambda i, *_: (i, 0)),
                      pl.BlockSpec(memory_space=pl.ANY), vec, vec],
            out_specs=pl.BlockSpec((td, D_MODEL), lambda i, *_: (i, 0)),
            scratch_shapes=[pltpu.VMEM((2, TOP_K * td * ROW_TILE, LANES), F32),
                            pltpu.VMEM((td * ROW_TILE, LANES), F32), pltpu.SemaphoreType.DMA((2,))]),
        out_shape=jax.ShapeDtypeStruct((t, D_MODEL), F32),
        compiler_params=_params(("arbitrary",)),
        name="moe_combine_ln",
    )(rdst, rsrc, rlen, lrow8, gates_flat, x2d, ys, g, b)


def _moe_ln(x2d, xt, router_w, router_b, w_gu, b_gu, w_dn, b_dn, ln_g, ln_b, tri):
    t = x2d.shape[0]
    rb = EXPERT_ROWS
    gates, lrow, before, tile_cnt, cnt = _router(
        x2d, router_w.T, router_b.reshape(N_EXPERTS, 1), tri)
    counts = cnt[:, 0].astype(jnp.int32)
    padded = ((counts + rb - 1) // rb) * rb
    pend = jnp.cumsum(padded).astype(jnp.int32)
    pstart = pend - padded
    n_blk = (t * TOP_K) // rb + N_EXPERTS
    n_used = pend[-1] // rb
    blk = jnp.minimum(jnp.arange(n_blk, dtype=jnp.int32), n_used - 1)
    blk_e = jnp.sum((blk[:, None] * rb >= pend[None, :]).astype(jnp.int32), axis=1)
    blk_e = jnp.minimum(blk_e, N_EXPERTS - 1).astype(jnp.int32)
    lrow8 = (lrow * ROW_TILE).astype(jnp.int32).T.reshape(-1)
    run_len = tile_cnt[:, :, 0].astype(jnp.int32)
    run_src = (jnp.cumsum(run_len, axis=1) - run_len) * ROW_TILE
    run_dst = (pstart[None, :] + before[:, :, 0].astype(jnp.int32)) * ROW_TILE
    flat = lambda a: a.reshape(-1).astype(jnp.int32)

    run_dst, run_src, run_len = flat(run_dst), flat(run_src), flat(run_len)

    xs = _dispatch(pend, padded.astype(jnp.int32), run_dst, run_src, run_len, lrow8, xt, n_blk * rb)
    ys = _experts(blk_e, n_used.reshape(1).astype(jnp.int32), xs,
                  w_gu, b_gu.reshape(N_EXPERTS, 1, -1), w_dn, b_dn.reshape(N_EXPERTS, 1, -1))
    return _combine(run_dst, run_src, run_len, lrow8, gates.T.reshape(-1), x2d, ys,
                    ln_g.reshape(1, -1), ln_b.reshape(1, -1))


def _spread_rope_cols(w):
    half = QK_ROPE_DIM // 2
    z = jnp.zeros(w.shape[:-1] + (LANES // 2 - half,), w.dtype)
    return jnp.concatenate([w[..., :half], z, w[..., half:], z], axis=-1)


def _rope_tables(seq, dim, spread):
    inv_freq = 1.0 / (ROPE_THETA ** (jnp.arange(0, dim, 2, dtype=F32) / dim))
    ang = jnp.arange(seq, dtype=F32)[:, None] * inv_freq[None, :]
    cos, sin = jnp.cos(ang), jnp.sin(ang)
    if spread:
        z = jnp.zeros((seq, LANES // 2 - dim // 2), F32)
        return (jnp.concatenate([cos, z, cos, z], axis=1), jnp.concatenate([-sin, z, sin, z], axis=1))
    return jnp.concatenate([cos, cos], axis=1), jnp.concatenate([-sin, sin], axis=1)


def kernel(x, l0_w_in, l0_q_norm_g, l0_w_q_up, l0_kv_norm_g, l0_w_kv_up, l0_conv_w, l0_conv_b, l0_conv_ln_g, l0_conv_ln_b, l0_w_o, l0_ln1_g, l0_ln1_b, l0_router_w, l0_router_b, l0_w_gu, l0_b_gu, l0_w_dn, l0_b_dn, l0_ln2_g, l0_ln2_b, l1_w_qkv, l1_lambda_q1, l1_lambda_k1, l1_lambda_q2, l1_lambda_k2, l1_subln_g, l1_w_o, l1_ln1_g, l1_ln1_b, l1_router_w, l1_router_b, l1_w_gu, l1_b_gu, l1_w_dn, l1_b_dn, l1_ln2_g, l1_ln2_b):
    b, s, d = x.shape
    t = b * s
    x2d = x.reshape(t, d)
    row = lambda a: a.reshape(1, -1)
    tri = (jnp.arange(TOKEN_TILE)[:, None] < jnp.arange(TOKEN_TILE)[None, :]).astype(BF16)

    o1 = Q_LORA_RANK
    o2 = o1 + KV_LORA_RANK
    o3 = o2 + QK_ROPE_DIM
    w_in = jnp.concatenate([l0_w_in[:, :o2], _spread_rope_cols(l0_w_in[:, o2:o3]), l0_w_in[:, o3:]],
                           axis=1).astype(BF16)
    wq = l0_w_q_up.reshape(Q_LORA_RANK, MLA_HEADS, QK_NOPE_DIM + QK_ROPE_DIM)
    wq = jnp.concatenate([wq[..., :QK_NOPE_DIM], _spread_rope_cols(wq[..., QK_NOPE_DIM:])], axis=-1)
    wq = wq.reshape(Q_LORA_RANK, MLA_HEADS * 2 * LANES).astype(BF16)
    wkv = l0_w_kv_up.reshape(KV_LORA_RANK, MLA_HEADS, QK_NOPE_DIM + V_HEAD_DIM)
    wkv = jnp.concatenate([wkv[..., :QK_NOPE_DIM].reshape(KV_LORA_RANK, -1),
                           wkv[..., QK_NOPE_DIM:].reshape(KV_LORA_RANK, -1)], axis=1).astype(BF16)
    cos0, sin0 = _rope_tables(s, QK_ROPE_DIM, spread=True)
    q, k, v, u = _l0_proj(x2d, w_in, row(l0_q_norm_g), wq, row(l0_kv_norm_g), wkv, cos0, sin0, s)
    hw = MLA_HEADS * 2 * LANES
    attn = _attention(q.reshape(b, s, hw), k.reshape(b, s, hw), v.reshape(b, s, hw),
                      MLA_HEADS, 1, 2 * LANES, 2 * LANES, V_HEAD_DIM)
    conv_w = jnp.concatenate([l0_conv_w, jnp.zeros((1, CONV_CHANNELS), F32)], axis=0)
    uc = _conv_module(u, conv_w, row(l0_conv_b), row(l0_conv_ln_g), row(l0_conv_ln_b), b, s)
    n_attn = MLA_HEADS * V_HEAD_DIM
    w_o = l0_w_o.astype(BF16)
    x2d, xt = _out_ln([attn.reshape(t, n_attn), uc], [w_o[:n_attn], w_o[n_attn:]], x2d,
                      row(l0_ln1_g), row(l0_ln1_b))
    x2d = _moe_ln(x2d, xt, l0_router_w, l0_router_b, l0_w_gu, l0_b_gu, l0_w_dn, l0_b_dn,
                  l0_ln2_g, l0_ln2_b, tri)

    lambda_init = 0.8 - 0.6 * math.exp(-0.3 * 1)
    cos1, sin1 = _rope_tables(s, DIFF_HEAD_DIM, spread=False)
    q, k, v = _l1_proj(x2d, l1_w_qkv.astype(BF16), cos1, sin1, s)
    lam_in = jnp.stack([l1_lambda_q1, l1_lambda_k1, l1_lambda_q2, l1_lambda_k2]).astype(F32)
    dv = 2 * DIFF_HEAD_DIM
    attn = _attention(q.reshape(b, s, d), k.reshape(b, s, d), v.reshape(b, s, d),
                      DIFF_HEADS, 2, DIFF_HEAD_DIM, dv, dv,
                      extra=(lam_in, row(l1_subln_g)), lambda_init=lambda_init)
    x2d, xt = _out_ln([attn.reshape(t, d)], [l1_w_o.astype(BF16)], x2d, row(l1_ln1_g), row(l1_ln1_b))
    x2d = _moe_ln(x2d, xt, l1_router_w, l1_router_b, l1_w_gu, l1_b_gu, l1_w_dn, l1_b_dn,
                  l1_ln2_g, l1_ln2_b, tri)
    return x2d.reshape(b, s, d)
```

```python
import functools
import math

import jax
import jax.numpy as jnp
from jax import lax
from jax.experimental import pallas as pl
from jax.experimental.pallas import tpu as pltpu

F32 = jnp.float32
BF16 = jnp.bfloat16

D_MODEL = 1024
DEPTH = 2
MLA_HEADS = 4
QK_NOPE_DIM = 128
QK_ROPE_DIM = 64
V_HEAD_DIM = 128
Q_LORA_RANK = 384
KV_LORA_RANK = 256
CONV_CHANNELS = D_MODEL - MLA_HEADS * V_HEAD_DIM
CONV_WIDTH = 31
DIFF_HEAD_DIM = 128
DIFF_HEADS = D_MODEL // (2 * DIFF_HEAD_DIM)
N_EXPERTS = 32
TOP_K = 4
D_EXPERT = D_MODEL
SWIGLU_LIMIT = 7.0
SWIGLU_ALPHA = 1.702
ROPE_THETA = 10000.0
DN_ALPHA = (2 * DEPTH) ** 0.25
LN_EPS = 1e-5
RMS_EPS = 1e-6
MASK_VALUE = -1e30

LANES = 128
ROW_TILE = 8
TOKEN_TILE = 512
ATTN_Q_TILE = 2048
ATTN_K_TILE = 1024
ATTN_CHAINS = 8
CONV_TILE = 512
CONV_HALO = 32
CONV_CHUNK = 64
EXPERT_ROWS = 512
CAST_CHUNK = 128
VMEM_LIMIT = 56 * 1024 * 1024


def _params(sem, vmem=VMEM_LIMIT):
    return pltpu.CompilerParams(dimension_semantics=sem, vmem_limit_bytes=vmem)


def _layer_norm(r, g, b):
    mu = jnp.mean(r, axis=-1, keepdims=True)
    d = r - mu
    var = jnp.mean(d * d, axis=-1, keepdims=True)
    return d * lax.rsqrt(var + LN_EPS) * g + b


def _rms_norm(x, g):
    return x * lax.rsqrt(jnp.mean(x * x, axis=-1, keepdims=True) + RMS_EPS) * g


def _rope(x, cos, sin):
    return x * cos + pltpu.roll(x, 64, 1) * sin


def _repeat_lanes(x, n):
    return x if n == 1 else jnp.concatenate([x] * n, axis=1)


def _load_rows(ref, n_rows):
    return jnp.concatenate([ref[pl.ds(j, n_rows, stride=ROW_TILE), :] for j in range(ROW_TILE)], axis=1)


def _store_rows(ref, val):
    n_rows = val.shape[0]
    for j in range(ROW_TILE):
        ref[pl.ds(j, n_rows, stride=ROW_TILE), :] = val[:, j * LANES:(j + 1) * LANES]


def _l0_proj_kernel(x_ref, win_ref, qg_ref, wq_ref, kvg_ref, wkv_ref, cos_ref, sin_ref,
                    q_ref, k_ref, v_ref, u_ref):
    tm = x_ref.shape[0]
    xb = x_ref[...].astype(BF16)
    proj = jnp.dot(xb, win_ref[...], preferred_element_type=F32)
    o1 = Q_LORA_RANK
    o2 = o1 + KV_LORA_RANK
    o3 = o2 + LANES
    o4 = o3 + CONV_CHANNELS
    u_ref[...] = proj[:, o3:o4] * jax.nn.sigmoid(proj[:, o4:])
    cos = cos_ref[...]
    sin = sin_ref[...]
    scale = (QK_NOPE_DIM + QK_ROPE_DIM) ** -0.5
    qn = _rms_norm(proj[:, :o1], qg_ref[...])
    qup = jnp.dot(qn.astype(BF16), wq_ref[...], preferred_element_type=F32)
    kvn = _rms_norm(proj[:, o1:o2], kvg_ref[...])
    kvup = jnp.dot(kvn.astype(BF16), wkv_ref[...], preferred_element_type=F32)
    k_rope = _rope(proj[:, o2:o3], cos, sin).astype(BF16)
    ones_col = jnp.where(lax.broadcasted_iota(jnp.int32, (tm, LANES), 1) == 0, 1.0, 0.0).astype(BF16)
    for h in range(MLA_HEADS):
        c = 2 * LANES * h
        q_ref[:, c:c + LANES] = (qup[:, c:c + LANES] * scale).astype(BF16)
        q_rope = _rope(qup[:, c + LANES:c + 2 * LANES], cos, sin)
        q_ref[:, c + LANES:c + 2 * LANES] = (q_rope * scale).astype(BF16)
        k_ref[:, c:c + LANES] = kvup[:, LANES * h:LANES * (h + 1)].astype(BF16)
        k_ref[:, c + LANES:c + 2 * LANES] = k_rope
        vh = MLA_HEADS * LANES + LANES * h
        v_ref[:, c:c + LANES] = kvup[:, vh:vh + LANES].astype(BF16)
        v_ref[:, c + LANES:c + 2 * LANES] = ones_col


def _l0_proj(x2d, w_in, qg, wq, kvg, wkv, cos, sin, seq):
    t = x2d.shape[0]
    tm = TOKEN_TILE
    n_pos = seq // tm
    full = lambda a: pl.BlockSpec(a.shape, lambda i: (0,) * a.ndim)
    row = lambda w: pl.BlockSpec((tm, w), lambda i: (i, 0))
    pos = pl.BlockSpec((tm, LANES), lambda i: (i % n_pos, 0))
    hw = MLA_HEADS * 2 * LANES
    wide = jax.ShapeDtypeStruct((t, hw), BF16)
    return pl.pallas_call(
        _l0_proj_kernel,
        grid=(t // tm,),
        in_specs=[row(D_MODEL), full(w_in), full(qg), full(wq), full(kvg), full(wkv), pos, pos],
        out_specs=[row(hw), row(hw), row(hw), row(CONV_CHANNELS)],
        out_shape=[wide, wide, wide, jax.ShapeDtypeStruct((t, CONV_CHANNELS), F32)],
        compiler_params=_params(("parallel",)),
        name="l0_proj",
    )(x2d, w_in, qg, wq, kvg, wkv, cos, sin)


def _l1_proj_kernel(x_ref, w_ref, cos_ref, sin_ref, q_ref, k_ref, v_ref):
    xb = x_ref[...].astype(BF16)
    cos = cos_ref[...]
    sin = sin_ref[...]
    scale = DIFF_HEAD_DIM ** -0.5
    qk_w = DIFF_HEADS * 2 * DIFF_HEAD_DIM
    q = jnp.dot(xb, w_ref[:, :qk_w], preferred_element_type=F32)
    for j in range(qk_w // LANES):
        c = j * LANES
        q_ref[:, c:c + LANES] = (_rope(q[:, c:c + LANES], cos, sin) * scale).astype(BF16)
    k = jnp.dot(xb, w_ref[:, qk_w:2 * qk_w], preferred_element_type=F32)
    for j in range(qk_w // LANES):
        c = j * LANES
        k_ref[:, c:c + LANES] = _rope(k[:, c:c + LANES], cos, sin).astype(BF16)
    v_ref[...] = jnp.dot(xb, w_ref[:, 2 * qk_w:], preferred_element_type=F32).astype(BF16)


def _l1_proj(x2d, w_qkv, cos, sin, seq):
    t = x2d.shape[0]
    tm = TOKEN_TILE
    n_pos = seq // tm
    row = lambda w: pl.BlockSpec((tm, w), lambda i: (i, 0))
    pos = pl.BlockSpec((tm, LANES), lambda i: (i % n_pos, 0))
    out = jax.ShapeDtypeStruct((t, D_MODEL), BF16)
    return pl.pallas_call(
        _l1_proj_kernel,
        grid=(t // tm,),
        in_specs=[row(D_MODEL), pl.BlockSpec(w_qkv.shape, lambda i: (0, 0)), pos, pos],
        out_specs=[row(D_MODEL)] * 3,
        out_shape=[out, out, out],
        compiler_params=_params(("parallel",)),
        name="l1_proj",
    )(x2d, w_qkv, cos, sin)


def _attn_kernel(*refs, n_maps, dk, tq, tk, n_chains, sum_col, lambda_init):
    if n_maps == 2:
        q_ref, k_ref, v_ref, lam_ref, g_ref, o_ref, m_sc, l_sc, acc_sc = refs
    else:
        q_ref, k_ref, v_ref, o_ref, m_sc, l_sc, acc_sc = refs
    qi = pl.program_id(2)
    rs = tq // n_chains
    dv = v_ref.shape[2]
    m_sc[...] = jnp.full(m_sc.shape, MASK_VALUE, F32)
    l_sc[...] = jnp.zeros(l_sc.shape, F32)
    acc_sc[...] = jnp.zeros(acc_sc.shape, F32)

    def chain(m, r, k, v, mask):
        rows = slice(r * rs, (r + 1) * rs)
        q = q_ref[0, rows, m * dk:(m + 1) * dk]
        s = lax.dot_general(q, k, (((1,), (1,)), ((), ())), preferred_element_type=F32)
        if mask is not None:
            s = jnp.where(mask, s, MASK_VALUE)
        m_prev = m_sc[m, rows, :]
        m_new = jnp.maximum(m_prev, jnp.max(s, axis=1, keepdims=True))
        p = jnp.exp(s - _repeat_lanes(m_new, s.shape[1] // LANES))
        a = jnp.exp(m_prev - m_new)
        if sum_col is None:
            l_sc[m, rows, :] = a * l_sc[m, rows, :] + jnp.sum(p, axis=1, keepdims=True)
        acc_sc[m, rows, :] = (_repeat_lanes(a, dv // LANES) * acc_sc[m, rows, :]
                              + jnp.dot(p.astype(BF16), v, preferred_element_type=F32))
        m_sc[m, rows, :] = m_new

    def off_diagonal(j, carry):
        start = pl.multiple_of(j * tk, tk)
        v = v_ref[0, pl.ds(start, tk), :]
        for m in range(n_maps):
            k = k_ref[0, pl.ds(start, tk), m * dk:(m + 1) * dk]
            for r in range(n_chains):
                chain(m, r, k, v, None)
        return carry

    lax.fori_loop(0, qi * (tq // tk), off_diagonal, 0)
    base = pl.multiple_of(qi * tq, tq)
    for r in range(n_chains):
        nk = (r + 1) * rs
        v = v_ref[0, pl.ds(base, nk), :]
        row = lax.broadcasted_iota(jnp.int32, (rs, nk), 0) + r * rs
        col = lax.broadcasted_iota(jnp.int32, (rs, nk), 1)
        mask = col <= row
        for m in range(n_maps):
            k = k_ref[0, pl.ds(base, nk), m * dk:(m + 1) * dk]
            chain(m, r, k, v, mask)

    if n_maps == 1:
        acc = acc_sc[0]
        o_ref[0] = (acc[:, :sum_col] / acc[:, sum_col:sum_col + 1]).astype(o_ref.dtype)
    else:
        lam_in = lam_ref[...]
        lam = (jnp.exp(jnp.sum(lam_in[0:1] * lam_in[1:2], axis=1, keepdims=True))
               - jnp.exp(jnp.sum(lam_in[2:3] * lam_in[3:4], axis=1, keepdims=True)) + lambda_init)
        a = acc_sc[0] / l_sc[0][:, 0:1] - lam * (acc_sc[1] / l_sc[1][:, 0:1])
        o_ref[0] = (_rms_norm(a, g_ref[...]) * (1.0 - lambda_init)).astype(o_ref.dtype)


def _attention(q, k, v, n_heads, n_maps, dk, dv_in, dv_out, extra=(), lambda_init=0.0):
    b, s, _ = q.shape
    tq, tk = ATTN_Q_TILE, ATTN_K_TILE
    qw = n_maps * dk
    sum_col = dv_out if dv_in > dv_out else None
    extra_specs = [pl.BlockSpec(e.shape, lambda bi, h, qi: (0, 0)) for e in extra]
    kern = functools.partial(_attn_kernel, n_maps=n_maps, dk=dk, tq=tq, tk=tk, n_chains=ATTN_CHAINS,
                             sum_col=sum_col, lambda_init=lambda_init)
    return pl.pallas_call(
        kern,
        grid=(b, n_heads, s // tq),
        in_specs=[pl.BlockSpec((1, tq, qw), lambda bi, h, qi: (bi, qi, h)),
                  pl.BlockSpec((1, s, qw), lambda bi, h, qi: (bi, 0, h)),
                  pl.BlockSpec((1, s, dv_in), lambda bi, h, qi: (bi, 0, h))] + extra_specs,
        out_specs=pl.BlockSpec((1, tq, dv_out), lambda bi, h, qi: (bi, qi, h)),
        out_shape=jax.ShapeDtypeStruct((b, s, n_heads * dv_out), BF16),
        scratch_shapes=[pltpu.VMEM((n_maps, tq, LANES), F32), pltpu.VMEM((n_maps, tq, LANES), F32),
                        pltpu.VMEM((n_maps, tq, dv_in), F32)],
        compiler_params=_params(("parallel", "parallel", "arbitrary")),
        name="attention_%dmap" % n_maps,
    )(q, k, v, *extra)


def _conv_kernel(u_ref, w_ref, cb_ref, g_ref, b_ref, o_ref, ext, win_sc):
    ts = u_ref.shape[0]
    si = pl.program_id(1)

    @pl.when(si == 0)
    def _():
        ext[0:CONV_HALO, :] = jnp.zeros((CONV_HALO, CONV_CHANNELS), F32)

    @pl.when(si > 0)
    def _():
        ext[0:CONV_HALO, :] = ext[ts:ts + CONV_HALO, :]

    ext[CONV_HALO:CONV_HALO + ts, :] = u_ref[...]
    first = CONV_HALO - (CONV_WIDTH - 1)
    for c in range(ts // CONV_CHUNK):
        r0 = c * CONV_CHUNK
        acc = jnp.zeros((CONV_CHUNK, CONV_CHANNELS), F32)
        for shift in range(ROW_TILE):
            offs = [first + j - shift for j in range(CONV_WIDTH) if (first + j) % ROW_TILE == shift]
            n_win = max(offs) + CONV_CHUNK
            win = win_sc.at[(c * ROW_TILE + shift) % 2]
            win[0:n_win, :] = ext[r0 + shift:r0 + shift + n_win, :]
            for off in offs:
                j = off + shift - first
                acc = acc + w_ref[j:j + 1, :] * win[off:off + CONV_CHUNK, :]
        y = _layer_norm(acc + cb_ref[...], g_ref[...], b_ref[...])
        o_ref[r0:r0 + CONV_CHUNK, :] = (y * jax.nn.sigmoid(y)).astype(o_ref.dtype)


def _conv_module(u2d, conv_w, conv_b, ln_g, ln_b, batch, seq):
    ts = CONV_TILE
    n_s = seq // ts
    vec = pl.BlockSpec((1, CONV_CHANNELS), lambda bi, si: (0, 0))
    return pl.pallas_call(
        _conv_kernel,
        grid=(batch, n_s),
        in_specs=[pl.BlockSpec((ts, CONV_CHANNELS), lambda bi, si: (bi * n_s + si, 0)),
                  pl.BlockSpec(conv_w.shape, lambda bi, si: (0, 0)), vec, vec, vec],
        out_specs=pl.BlockSpec((ts, CONV_CHANNELS), lambda bi, si: (bi * n_s + si, 0)),
        out_shape=jax.ShapeDtypeStruct((batch * seq, CONV_CHANNELS), BF16),
        scratch_shapes=[pltpu.VMEM((ts + CONV_HALO, CONV_CHANNELS), F32),
                        pltpu.VMEM((2, CONV_CHUNK + CONV_HALO, CONV_CHANNELS), F32)],
        compiler_params=_params(("arbitrary", "arbitrary")),
        name="conv_module",
    )(u2d, conv_w, conv_b, ln_g, ln_b)


def _out_ln_kernel(*refs, n_in):
    a_refs = refs[:n_in]
    w_refs = refs[n_in:2 * n_in]
    x_ref, g_ref, b_ref, o_ref, ot_ref = refs[2 * n_in:]
    mix = jnp.dot(a_refs[0][...], w_refs[0][...], preferred_element_type=F32)
    for a_ref, w_ref in zip(a_refs[1:], w_refs[1:]):
        mix = mix + jnp.dot(a_ref[...], w_ref[...], preferred_element_type=F32)
    y = _layer_norm(DN_ALPHA * x_ref[...] + mix, g_ref[...], b_ref[...])
    o_ref[...] = y
    _store_rows(ot_ref, y)


def _out_ln(acts, weights, x2d, g, b):
    t = x2d.shape[0]
    tm = TOKEN_TILE
    vec = pl.BlockSpec((1, D_MODEL), lambda i: (0, 0))
    return pl.pallas_call(
        functools.partial(_out_ln_kernel, n_in=len(acts)),
        grid=(t // tm,),
        in_specs=([pl.BlockSpec((tm, a.shape[1]), lambda i: (i, 0)) for a in acts]
                  + [pl.BlockSpec(w.shape, lambda i: (0, 0)) for w in weights]
                  + [pl.BlockSpec((tm, D_MODEL), lambda i: (i, 0)), vec, vec]),
        out_specs=[pl.BlockSpec((tm, D_MODEL), lambda i: (i, 0)),
                   pl.BlockSpec((tm * ROW_TILE, LANES), lambda i: (i, 0))],
        out_shape=[jax.ShapeDtypeStruct((t, D_MODEL), F32),
                   jax.ShapeDtypeStruct((t * ROW_TILE, LANES), F32)],
        compiler_params=_params(("parallel",)),
        name="out_proj_ln",
    )(*acts, *weights, x2d, g, b)


def _router_kernel(x_ref, rw_ref, rb_ref, tri_ref, low_ref, gate_ref, lrow_ref, before_ref, tcnt_ref,
                   cnt_ref):
    tm = x_ref.shape[0]

    @pl.when(pl.program_id(0) == 0)
    def _():
        cnt_ref[...] = jnp.zeros(cnt_ref.shape, F32)

    x = x_ref[...]
    x_top = pltpu.bitcast(pltpu.bitcast(x, jnp.uint32) & jnp.uint32(0xFFFF0000), F32)
    x_hi = x_top.astype(BF16)
    x_lo = (x - x_top).astype(BF16)
    nt = (((1,), (1,)), ((), ()))
    by_hi = lax.dot_general(rw_ref[...], x_hi, nt, preferred_element_type=F32)
    by_lo = lax.dot_general(rw_ref[...], x_lo, nt, preferred_element_type=F32)
    logits = ((by_hi[:N_EXPERTS] + by_hi[N_EXPERTS:]) + (by_lo[:N_EXPERTS] + by_lo[N_EXPERTS:])
              + rb_ref[...])
    e_iota = lax.broadcasted_iota(jnp.int32, (N_EXPERTS, tm), 0)
    vals, sels = [], []
    work = logits
    for k in range(TOP_K):
        top = jnp.max(work, axis=0, keepdims=True)
        idx = jnp.min(jnp.where(work == top, e_iota, N_EXPERTS), axis=0, keepdims=True)
        sel = e_iota == idx
        vals.append(top)
        sels.append(sel)
        work = jnp.where(sel, -jnp.inf, work)
    exps = [jnp.exp(v - vals[0]) for v in vals]
    denom = exps[0] + exps[1] + exps[2] + exps[3]
    for k in range(TOP_K):
        gate_ref[k:k + 1, :] = exps[k] / denom
    chosen = jnp.where(sels[0] | sels[1] | sels[2] | sels[3], 1.0, 0.0)
    earlier = jnp.dot(chosen.astype(BF16), tri_ref[...], preferred_element_type=F32)
    before = cnt_ref[...]
    tile_cnt = jnp.sum(chosen, axis=1, keepdims=True)
    cnt16 = jnp.floor(tile_cnt * (1.0 / 16.0))
    cnt_r = tile_cnt - 16.0 * cnt16
    prefix = lambda c: jnp.dot(low_ref[...], jnp.broadcast_to(c, before.shape).astype(BF16),
                               preferred_element_type=F32)
    local = 16.0 * prefix(cnt16) + prefix(cnt_r)
    lbase = earlier + local[:, 0:1]
    for k in range(TOP_K):
        lrow = jnp.sum(jnp.where(sels[k], lbase, 0.0), axis=0, keepdims=True)
        lrow_ref[k:k + 1, :] = lrow.astype(jnp.int32)
    before_ref[0] = before
    tcnt_ref[0] = jnp.broadcast_to(tile_cnt, before.shape)
    cnt_ref[...] = before + tile_cnt


def _router(x2d, rw_t, rb, tri):
    t = x2d.shape[0]
    tm = TOKEN_TILE
    n = t // tm
    kt = pl.BlockSpec((TOP_K, tm), lambda i: (0, i))
    per_tile = pl.BlockSpec((1, N_EXPERTS, LANES), lambda i: (i, 0, 0))
    low = (jnp.arange(N_EXPERTS)[:, None] > jnp.arange(N_EXPERTS)[None, :]).astype(BF16)
    ints = jax.ShapeDtypeStruct((TOP_K, t), jnp.int32)
    tiles = jax.ShapeDtypeStruct((n, N_EXPERTS, LANES), F32)
    return pl.pallas_call(
        _router_kernel,
        grid=(n,),
        in_specs=[pl.BlockSpec((tm, D_MODEL), lambda i: (i, 0)),
                  pl.BlockSpec(rw_t.shape, lambda i: (0, 0)),
                  pl.BlockSpec(rb.shape, lambda i: (0, 0)),
                  pl.BlockSpec(tri.shape, lambda i: (0, 0)),
                  pl.BlockSpec(low.shape, lambda i: (0, 0))],
        out_specs=[kt, kt, per_tile, per_tile, pl.BlockSpec((N_EXPERTS, LANES), lambda i: (0, 0))],
        out_shape=[jax.ShapeDtypeStruct((TOP_K, t), F32), ints, tiles, tiles,
                   jax.ShapeDtypeStruct((N_EXPERTS, LANES), F32)],
        compiler_params=_params(("arbitrary",)),
        name="router",
    )(x2d, rw_t, rb, tri, low)


def _dispatch_kernel(pend_ref, padded_ref, rdst_ref, rsrc_ref, rlen_ref, lrow_ref, xt_ref, xs_ref,
                     zeros, stage, zsem, sems):
    td = lrow_ref.shape[0] // TOP_K
    zrows = zeros.shape[0]
    i = pl.program_id(0)
    slot = i % 2

    @pl.when(pl.program_id(0) == 0)
    def _():
        zeros[...] = jnp.zeros(zeros.shape, F32)

        def zero_copy(e):
            start = pl.multiple_of(pend_ref[e] * ROW_TILE - zrows, zrows)
            return pltpu.make_async_copy(zeros, xs_ref.at[pl.ds(start, zrows)], zsem)

        def start_zero(e, carry):
            @pl.when(padded_ref[e] > 0)
            def _():
                zero_copy(e).start()
            return carry

        def wait_zero(e, carry):
            @pl.when(padded_ref[e] > 0)
            def _():
                zero_copy(e).wait()
            return carry

        lax.fori_loop(0, N_EXPERTS, start_zero, 0)
        lax.fori_loop(0, N_EXPERTS, wait_zero, 0)

    def start_runs(tile, s):
        def per_expert(e, carry):
            n_rows = rlen_ref[tile * N_EXPERTS + e]
            src = rsrc_ref[tile * N_EXPERTS + e]
            dst = rdst_ref[tile * N_EXPERTS + e]
            for bit in reversed(range(td.bit_length())):
                size = (1 << bit) * ROW_TILE
                has = (n_rows & (1 << bit)) != 0

                @pl.when(has)
                def _(src=src, dst=dst, size=size):
                    pltpu.make_async_copy(
                        stage.at[s, pl.ds(pl.multiple_of(src, ROW_TILE), size)],
                        xs_ref.at[pl.ds(pl.multiple_of(dst, ROW_TILE), size)], sems.at[s]).start()

                step = jnp.where(has, size, 0)
                src = src + step
                dst = dst + step
            return carry

        lax.fori_loop(0, N_EXPERTS, per_expert, 0)

    def wait_tile(s):
        pltpu.make_async_copy(stage.at[s], xs_ref.at[pl.ds(0, stage.shape[1])], sems.at[s]).wait()

    def compact(t8, carry):
        for u in range(ROW_TILE):
            row = xt_ref[pl.ds(pl.multiple_of((t8 * ROW_TILE + u) * ROW_TILE, ROW_TILE), ROW_TILE), :]
            for k in range(TOP_K):
                dst = pl.multiple_of(lrow_ref[t8 * (ROW_TILE * TOP_K) + (u * TOP_K + k)], ROW_TILE)
                stage[slot, pl.ds(dst, ROW_TILE), :] = row
        return carry

    lax.fori_loop(0, td // ROW_TILE, compact, 0)
    start_runs(i, slot)

    @pl.when(i > 0)
    def _():
        wait_tile(1 - slot)

    @pl.when(i == pl.num_programs(0) - 1)
    def _():
        wait_tile(slot)


def _dispatch(pend, padded, rdst, rsrc, rlen, lrow8, xt, n_rows):
    t = lrow8.shape[0] // TOP_K
    td = TOKEN_TILE
    smem = lambda i, *_: (i,)
    return pl.pallas_call(
        _dispatch_kernel,
        grid_spec=pltpu.PrefetchScalarGridSpec(
            num_scalar_prefetch=5,
            grid=(t // td,),
            in_specs=[pl.BlockSpec((TOP_K * td,), smem, memory_space=pltpu.SMEM),
                      pl.BlockSpec((td * ROW_TILE, LANES), lambda i, *_: (i, 0))],
            out_specs=pl.BlockSpec(memory_space=pl.ANY),
            scratch_shapes=[pltpu.VMEM((EXPERT_ROWS * ROW_TILE, LANES), F32),
                            pltpu.VMEM((2, TOP_K * td * ROW_TILE, LANES), F32),
                            pltpu.SemaphoreType.DMA(()), pltpu.SemaphoreType.DMA((2,))]),
        out_shape=jax.ShapeDtypeStruct((n_rows * ROW_TILE, LANES), F32),
        compiler_params=_params(("arbitrary",)),
        name="moe_dispatch",
    )(pend, padded, rdst, rsrc, rlen, lrow8, xt)


def _expert_kernel(blk_e_ref, grp_ref, nxt_ref, valid_ref, n_used_ref, xs_ref, wgu_hbm, bgu_ref, wdn_hbm,
                   bdn_ref, ys_ref, wgu_f32, wdn_f32, wgu_bf, wdn_bf, sems):
    i = pl.program_id(0)
    rb = xs_ref.shape[0] // ROW_TILE
    active = i < n_used_ref[0]
    new_expert = (i == 0) | (blk_e_ref[i] != blk_e_ref[jnp.maximum(i - 1, 0)])

    def weight_copies(e, slot):
        return (pltpu.make_async_copy(wgu_hbm.at[e], wgu_f32.at[slot], sems.at[0, slot]),
                pltpu.make_async_copy(wdn_hbm.at[e], wdn_f32.at[slot], sems.at[1, slot]))

    @pl.when(active & new_expert)
    def _():
        slot = grp_ref[i] % 2
        e = blk_e_ref[i]
        nxt = nxt_ref[i]

        @pl.when(i == 0)
        def _():
            for cp in weight_copies(e, slot):
                cp.start()

        @pl.when(nxt >= 0)
        def _():
            for cp in weight_copies(nxt, 1 - slot):
                cp.start()

        for cp in weight_copies(e, slot):
            cp.wait()

        def cast(c, carry):
            rows = pl.ds(pl.multiple_of(c * CAST_CHUNK, CAST_CHUNK), CAST_CHUNK)
            wgu_bf[rows, :] = wgu_f32[slot, rows, :].astype(BF16)
            wdn_bf[rows, :] = wdn_f32[slot, rows, :].astype(BF16)
            return carry

        lax.fori_loop(0, D_MODEL // CAST_CHUNK, cast, 0)

    def swiglu_rows(n_rows):
        xb = _load_rows(xs_ref, n_rows).astype(BF16)
        h = jnp.dot(xb, wgu_bf[...], preferred_element_type=F32) + bgu_ref[0]
        gate = jnp.minimum(h[:, :D_EXPERT], SWIGLU_LIMIT)
        up = jnp.clip(h[:, D_EXPERT:], -SWIGLU_LIMIT, SWIGLU_LIMIT)
        act = (up + 1.0) * gate * jax.nn.sigmoid(SWIGLU_ALPHA * gate)
        y = jnp.dot(act.astype(BF16), wdn_bf[...], preferred_element_type=F32) + bdn_ref[0]
        _store_rows(ys_ref, y)

    half_only = valid_ref[i] <= rb // 2

    @pl.when(active & jnp.logical_not(half_only))
    def _():
        swiglu_rows(rb)

    @pl.when(active & half_only)
    def _():
        swiglu_rows(rb // 2)
        ys_ref[rb // 2 * ROW_TILE:, :] = jnp.zeros((rb // 2 * ROW_TILE, LANES), F32)


def _experts(blk_e, valid, n_used, xs, w_gu, b_gu, w_dn, b_dn):
    rb = EXPERT_ROWS
    n_blk = xs.shape[0] // (rb * ROW_TILE)
    ids = jnp.arange(n_blk, dtype=jnp.int32)
    change = jnp.concatenate([jnp.ones((1,), bool), blk_e[1:] != blk_e[:-1]])
    grp = (jnp.cumsum(change.astype(jnp.int32)) - 1).astype(jnp.int32)
    later = jnp.where(change, ids, n_blk)
    nxt_pos = jnp.concatenate([lax.cummin(later[::-1])[::-1][1:], jnp.full((1,), n_blk, jnp.int32)])
    nxt = jnp.where(nxt_pos < n_blk, blk_e[jnp.minimum(nxt_pos, n_blk - 1)], -1).astype(jnp.int32)
    rows = pl.BlockSpec((rb * ROW_TILE, LANES), lambda i, be, gr, nx, va, nu: (jnp.minimum(i, nu[0] - 1), 0))
    per_e = lambda a: pl.BlockSpec((1,) + a.shape[1:], lambda i, be, gr, nx, va, nu: (be[i], 0, 0))
    hbm = pl.BlockSpec(memory_space=pl.ANY)
    return pl.pallas_call(
        _expert_kernel,
        grid_spec=pltpu.PrefetchScalarGridSpec(
            num_scalar_prefetch=5,
            grid=(n_blk,),
            in_specs=[rows, hbm, per_e(b_gu), hbm, per_e(b_dn)],
            out_specs=rows,
            scratch_shapes=[pltpu.VMEM((2,) + w_gu.shape[1:], F32), pltpu.VMEM((2,) + w_dn.shape[1:], F32),
                            pltpu.VMEM(w_gu.shape[1:], BF16), pltpu.VMEM(w_dn.shape[1:], BF16),
                            pltpu.SemaphoreType.DMA((2, 2))]),
        out_shape=jax.ShapeDtypeStruct(xs.shape, F32),
        compiler_params=_params(("arbitrary",)),
        name="moe_experts",
    )(blk_e, grp, nxt, valid, n_used, xs, w_gu, b_gu, w_dn, b_dn)


def _combine_kernel(rdst_ref, rsrc_ref, rlen_ref, lrow_ref, gate_ref, x_ref, ys_ref, g_ref, b_ref, o_ref,
                    stage, moe_sc, sems):
    td = x_ref.shape[0]
    i = pl.program_id(0)
    slot = i % 2

    def start_runs(tile, s):
        def per_expert(e, carry):
            n_rows = rlen_ref[tile * N_EXPERTS + e]
            src = rdst_ref[tile * N_EXPERTS + e]
            dst = rsrc_ref[tile * N_EXPERTS + e]
            for bit in reversed(range(td.bit_length())):
                size = (1 << bit) * ROW_TILE
                has = (n_rows & (1 << bit)) != 0

                @pl.when(has)
                def _(src=src, dst=dst, size=size):
                    pltpu.make_async_copy(
                        ys_ref.at[pl.ds(pl.multiple_of(src, ROW_TILE), size)],
                        stage.at[s, pl.ds(pl.multiple_of(dst, ROW_TILE), size)], sems.at[s]).start()

                step = jnp.where(has, size, 0)
                src = src + step
                dst = dst + step
            return carry

        lax.fori_loop(0, N_EXPERTS, per_expert, 0)

    @pl.when(i == 0)
    def _():
        start_runs(0, 0)

    @pl.when(i + 1 < pl.num_programs(0))
    def _():
        start_runs(i + 1, 1 - slot)

    pltpu.make_async_copy(ys_ref.at[pl.ds(0, stage.shape[1])], stage.at[slot], sems.at[slot]).wait()

    def reduce_rows(t8, carry):
        for u in range(ROW_TILE):
            a0 = (t8 * ROW_TILE + u) * TOP_K
            acc = None
            for k in range(TOP_K):
                row = stage[slot, pl.ds(pl.multiple_of(lrow_ref[a0 + k], ROW_TILE), ROW_TILE), :]
                term = gate_ref[a0 + k] * row
                acc = term if acc is None else acc + term
            moe_sc[pl.ds(pl.multiple_of((t8 * ROW_TILE + u) * ROW_TILE, ROW_TILE), ROW_TILE), :] = acc
        return carry

    lax.fori_loop(0, td // ROW_TILE, reduce_rows, 0)
    moe = _load_rows(moe_sc, td)
    o_ref[...] = _layer_norm(DN_ALPHA * x_ref[...] + moe, g_ref[...], b_ref[...])


def _combine(rdst, rsrc, rlen, lrow8, gates_flat, x2d, ys, g, b):
    t = x2d.shape[0]
    td = TOKEN_TILE
    vec = pl.BlockSpec((1, D_MODEL), lambda i, *_: (0, 0))
    smem = pl.BlockSpec((TOP_K * td,), lambda i, *_: (i,), memory_space=pltpu.SMEM)
    return pl.pallas_call(
        _combine_kernel,
        grid_spec=pltpu.PrefetchScalarGridSpec(
            num_scalar_prefetch=3,
            grid=(t // td,),
            in_specs=[smem, smem, pl.BlockSpec((td, D_MODEL), lambda i, *_: (i, 0)),
                      pl.BlockSpec(memory_space=pl.ANY), vec, vec],
            out_specs=pl.BlockSpec((td, D_MODEL), lambda i, *_: (i, 0)),
            scratch_shapes=[pltpu.VMEM((2, TOP_K * td * ROW_TILE, LANES), F32),
                            pltpu.VMEM((td * ROW_TILE, LANES), F32), pltpu.SemaphoreType.DMA((2,))]),
        out_shape=jax.ShapeDtypeStruct((t, D_MODEL), F32),
        compiler_params=_params(("arbitrary",)),
        name="moe_combine_ln",
    )(rdst, rsrc, rlen, lrow8, gates_flat, x2d, ys, g, b)


def _moe_ln(x2d, xt, router_w, router_b, w_gu, b_gu, w_dn, b_dn, ln_g, ln_b, tri):
    t = x2d.shape[0]
    rb = EXPERT_ROWS
    rw_t = router_w.T.astype(F32)
    rw_top = lax.bitcast_convert_type(
        lax.bitcast_convert_type(rw_t, jnp.uint32) & jnp.uint32(0xFFFF0000), F32)
    rw_split = jnp.concatenate([rw_top.astype(BF16), (rw_t - rw_top).astype(BF16)], axis=0)
    gates, lrow, before, tile_cnt, cnt = _router(x2d, rw_split, router_b.reshape(N_EXPERTS, 1), tri)
    counts = cnt[:, 0].astype(jnp.int32)
    padded = ((counts + rb - 1) // rb) * rb
    pend = jnp.cumsum(padded).astype(jnp.int32)
    pstart = pend - padded
    n_blk = (t * TOP_K) // rb + N_EXPERTS
    n_used = pend[-1] // rb
    blk = jnp.minimum(jnp.arange(n_blk, dtype=jnp.int32), n_used - 1)
    blk_e = jnp.sum((blk[:, None] * rb >= pend[None, :]).astype(jnp.int32), axis=1)
    blk_e = jnp.minimum(blk_e, N_EXPERTS - 1).astype(jnp.int32)
    valid = jnp.clip((pstart + counts)[blk_e] - blk * rb, 0, rb).astype(jnp.int32)
    lrow8 = (lrow * ROW_TILE).astype(jnp.int32).T.reshape(-1)
    run_len = tile_cnt[:, :, 0].astype(jnp.int32)
    run_src = (jnp.cumsum(run_len, axis=1) - run_len) * ROW_TILE
    run_dst = (pstart[None, :] + before[:, :, 0].astype(jnp.int32)) * ROW_TILE
    flat = lambda a: a.reshape(-1).astype(jnp.int32)

    run_dst, run_src, run_len = flat(run_dst), flat(run_src), flat(run_len)

    xs = _dispatch(pend, padded.astype(jnp.int32), run_dst, run_src, run_len, lrow8, xt, n_blk * rb)
    ys = _experts(blk_e, valid, n_used.reshape(1).astype(jnp.int32), xs,
                  w_gu, b_gu.reshape(N_EXPERTS, 1, -1), w_dn, b_dn.reshape(N_EXPERTS, 1, -1))
    return _combine(run_dst, run_src, run_len, lrow8, gates.T.reshape(-1), x2d, ys,
                    ln_g.reshape(1, -1), ln_b.reshape(1, -1))


def _spread_rope_cols(w):
    half = QK_ROPE_DIM // 2
    z = jnp.zeros(w.shape[:-1] + (LANES // 2 - half,), w.dtype)
    return jnp.concatenate([w[..., :half], z, w[..., half:], z], axis=-1)


def _rope_tables(seq, dim, spread):
    inv_freq = 1.0 / (ROPE_THETA ** (jnp.arange(0, dim, 2, dtype=F32) / dim))
    ang = jnp.arange(seq, dtype=F32)[:, None] * inv_freq[None, :]
    cos, sin = jnp.cos(ang), jnp.sin(ang)
    if spread:
        z = jnp.zeros((seq, LANES // 2 - dim // 2), F32)
        return (jnp.concatenate([cos, z, cos, z], axis=1), jnp.concatenate([-sin, z, sin, z], axis=1))
    return jnp.concatenate([cos, cos], axis=1), jnp.concatenate([-sin, sin], axis=1)


def kernel(x, l0_w_in, l0_q_norm_g, l0_w_q_up, l0_kv_norm_g, l0_w_kv_up, l0_conv_w, l0_conv_b, l0_conv_ln_g, l0_conv_ln_b, l0_w_o, l0_ln1_g, l0_ln1_b, l0_router_w, l0_router_b, l0_w_gu, l0_b_gu, l0_w_dn, l0_b_dn, l0_ln2_g, l0_ln2_b, l1_w_qkv, l1_lambda_q1, l1_lambda_k1, l1_lambda_q2, l1_lambda_k2, l1_subln_g, l1_w_o, l1_ln1_g, l1_ln1_b, l1_router_w, l1_router_b, l1_w_gu, l1_b_gu, l1_w_dn, l1_b_dn, l1_ln2_g, l1_ln2_b):
    b, s, d = x.shape
    t = b * s
    x2d = x.reshape(t, d)
    row = lambda a: a.reshape(1, -1)
    tri = (jnp.arange(TOKEN_TILE)[:, None] < jnp.arange(TOKEN_TILE)[None, :]).astype(BF16)

    o1 = Q_LORA_RANK
    o2 = o1 + KV_LORA_RANK
    o3 = o2 + QK_ROPE_DIM
    w_in = jnp.concatenate([l0_w_in[:, :o2], _spread_rope_cols(l0_w_in[:, o2:o3]), l0_w_in[:, o3:]],
                           axis=1).astype(BF16)
    wq = l0_w_q_up.reshape(Q_LORA_RANK, MLA_HEADS, QK_NOPE_DIM + QK_ROPE_DIM)
    wq = jnp.concatenate([wq[..., :QK_NOPE_DIM], _spread_rope_cols(wq[..., QK_NOPE_DIM:])], axis=-1)
    wq = wq.reshape(Q_LORA_RANK, MLA_HEADS * 2 * LANES).astype(BF16)
    wkv = l0_w_kv_up.reshape(KV_LORA_RANK, MLA_HEADS, QK_NOPE_DIM + V_HEAD_DIM)
    wkv = jnp.concatenate([wkv[..., :QK_NOPE_DIM].reshape(KV_LORA_RANK, -1),
                           wkv[..., QK_NOPE_DIM:].reshape(KV_LORA_RANK, -1)], axis=1).astype(BF16)
    cos0, sin0 = _rope_tables(s, QK_ROPE_DIM, spread=True)
    q, k, v, u = _l0_proj(x2d, w_in, row(l0_q_norm_g), wq, row(l0_kv_norm_g), wkv, cos0, sin0, s)
    hw = MLA_HEADS * 2 * LANES
    attn = _attention(q.reshape(b, s, hw), k.reshape(b, s, hw), v.reshape(b, s, hw),
                      MLA_HEADS, 1, 2 * LANES, 2 * LANES, V_HEAD_DIM)
    conv_w = jnp.concatenate([l0_conv_w, jnp.zeros((1, CONV_CHANNELS), F32)], axis=0)
    uc = _conv_module(u, conv_w, row(l0_conv_b), row(l0_conv_ln_g), row(l0_conv_ln_b), b, s)
    n_attn = MLA_HEADS * V_HEAD_DIM
    w_o = l0_w_o.astype(BF16)
    x2d, xt = _out_ln([attn.reshape(t, n_attn), uc], [w_o[:n_attn], w_o[n_attn:]], x2d,
                      row(l0_ln1_g), row(l0_ln1_b))
    x2d = _moe_ln(x2d, xt, l0_router_w, l0_router_b, l0_w_gu, l0_b_gu, l0_w_dn, l0_b_dn,
                  l0_ln2_g, l0_ln2_b, tri)

    lambda_init = 0.8 - 0.6 * math.exp(-0.3 * 1)
    cos1, sin1 = _rope_tables(s, DIFF_HEAD_DIM, spread=False)
    q, k, v = _l1_proj(x2d, l1_w_qkv.astype(BF16), cos1, sin1, s)
    lam_in = jnp.stack([l1_lambda_q1, l1_lambda_k1, l1_lambda_q2, l1_lambda_k2]).astype(F32)
    dv = 2 * DIFF_HEAD_DIM
    attn = _attention(q.reshape(b, s, d), k.reshape(b, s, d), v.reshape(b, s, d),
                      DIFF_HEADS, 2, DIFF_HEAD_DIM, dv, dv,
                      extra=(lam_in, row(l1_subln_g)), lambda_init=lambda_init)
    x2d, xt = _out_ln([attn.reshape(t, d)], [l1_w_o.astype(BF16)], x2d, row(l1_ln1_g), row(l1_ln1_b))
    x2d = _moe_ln(x2d, xt, l1_router_w, l1_router_b, l1_w_gu, l1_b_gu, l1_w_dn, l1_b_dn,
                  l1_ln2_g, l1_ln2_b, tri)
    return x2d.reshape(b, s, d)
```

```python
import functools
import math

import jax
import jax.numpy as jnp
from jax import lax
from jax.experimental import pallas as pl
from jax.experimental.pallas import tpu as pltpu

F32 = jnp.float32
BF16 = jnp.bfloat16

D_MODEL = 1024
DEPTH = 2
MLA_HEADS = 4
QK_NOPE_DIM = 128
QK_ROPE_DIM = 64
V_HEAD_DIM = 128
Q_LORA_RANK = 384
KV_LORA_RANK = 256
CONV_CHANNELS = D_MODEL - MLA_HEADS * V_HEAD_DIM
CONV_WIDTH = 31
DIFF_HEAD_DIM = 128
DIFF_HEADS = D_MODEL // (2 * DIFF_HEAD_DIM)
N_EXPERTS = 32
TOP_K = 4
D_EXPERT = D_MODEL
SWIGLU_LIMIT = 7.0
SWIGLU_ALPHA = 1.702
ROPE_THETA = 10000.0
DN_ALPHA = (2 * DEPTH) ** 0.25
LN_EPS = 1e-5
RMS_EPS = 1e-6
MASK_VALUE = -1e30

LANES = 128
ROW_TILE = 8
TOKEN_TILE = 512
ATTN_Q_TILE = 2048
ATTN_K_TILE = 1024
ATTN_CHAINS = 8
CONV_TILE = 512
CONV_HALO = 32
CONV_CHUNK = 64
EXPERT_ROWS = 512
CAST_CHUNK = 128
VMEM_LIMIT = 56 * 1024 * 1024


def _params(sem, vmem=VMEM_LIMIT):
    return pltpu.CompilerParams(dimension_semantics=sem, vmem_limit_bytes=vmem)


def _layer_norm(r, g, b):
    mu = jnp.mean(r, axis=-1, keepdims=True)
    d = r - mu
    var = jnp.mean(d * d, axis=-1, keepdims=True)
    return d * lax.rsqrt(var + LN_EPS) * g + b


def _rms_norm(x, g):
    return x * lax.rsqrt(jnp.mean(x * x, axis=-1, keepdims=True) + RMS_EPS) * g


def _rope(x, cos, sin):
    return x * cos + pltpu.roll(x, 64, 1) * sin


def _repeat_lanes(x, n):
    return x if n == 1 else jnp.concatenate([x] * n, axis=1)


def _load_rows(ref, n_rows):
    return jnp.concatenate([ref[pl.ds(j, n_rows, stride=ROW_TILE), :] for j in range(ROW_TILE)], axis=1)


def _store_rows(ref, val):
    n_rows = val.shape[0]
    for j in range(ROW_TILE):
        ref[pl.ds(j, n_rows, stride=ROW_TILE), :] = val[:, j * LANES:(j + 1) * LANES]


def _l0_proj_kernel(x_ref, win_ref, qg_ref, wq_ref, kvg_ref, wkv_ref, cos_ref, sin_ref,
                    q_ref, k_ref, v_ref, u_ref):
    tm = x_ref.shape[0]
    xb = x_ref[...].astype(BF16)
    proj = jnp.dot(xb, win_ref[...], preferred_element_type=F32)
    o1 = Q_LORA_RANK
    o2 = o1 + KV_LORA_RANK
    o3 = o2 + LANES
    o4 = o3 + CONV_CHANNELS
    u_ref[...] = proj[:, o3:o4] * jax.nn.sigmoid(proj[:, o4:])
    cos = cos_ref[...]
    sin = sin_ref[...]
    scale = (QK_NOPE_DIM + QK_ROPE_DIM) ** -0.5
    qn = _rms_norm(proj[:, :o1], qg_ref[...])
    qup = jnp.dot(qn.astype(BF16), wq_ref[...], preferred_element_type=F32)
    kvn = _rms_norm(proj[:, o1:o2], kvg_ref[...])
    kvup = jnp.dot(kvn.astype(BF16), wkv_ref[...], preferred_element_type=F32)
    k_rope = _rope(proj[:, o2:o3], cos, sin).astype(BF16)
    ones_col = jnp.where(lax.broadcasted_iota(jnp.int32, (tm, LANES), 1) == 0, 1.0, 0.0).astype(BF16)
    for h in range(MLA_HEADS):
        c = 2 * LANES * h
        q_ref[:, c:c + LANES] = (qup[:, c:c + LANES] * scale).astype(BF16)
        q_rope = _rope(qup[:, c + LANES:c + 2 * LANES], cos, sin)
        q_ref[:, c + LANES:c + 2 * LANES] = (q_rope * scale).astype(BF16)
        k_ref[:, c:c + LANES] = kvup[:, LANES * h:LANES * (h + 1)].astype(BF16)
        k_ref[:, c + LANES:c + 2 * LANES] = k_rope
        vh = MLA_HEADS * LANES + LANES * h
        v_ref[:, c:c + LANES] = kvup[:, vh:vh + LANES].astype(BF16)
        v_ref[:, c + LANES:c + 2 * LANES] = ones_col


def _l0_proj(x2d, w_in, qg, wq, kvg, wkv, cos, sin, seq):
    t = x2d.shape[0]
    tm = TOKEN_TILE
    n_pos = seq // tm
    full = lambda a: pl.BlockSpec(a.shape, lambda i: (0,) * a.ndim)
    row = lambda w: pl.BlockSpec((tm, w), lambda i: (i, 0))
    pos = pl.BlockSpec((tm, LANES), lambda i: (i % n_pos, 0))
    hw = MLA_HEADS * 2 * LANES
    wide = jax.ShapeDtypeStruct((t, hw), BF16)
    return pl.pallas_call(
        _l0_proj_kernel,
        grid=(t // tm,),
        in_specs=[row(D_MODEL), full(w_in), full(qg), full(wq), full(kvg), full(wkv), pos, pos],
        out_specs=[row(hw), row(hw), row(hw), row(CONV_CHANNELS)],
        out_shape=[wide, wide, wide, jax.ShapeDtypeStruct((t, CONV_CHANNELS), F32)],
        compiler_params=_params(("parallel",)),
        name="l0_proj",
    )(x2d, w_in, qg, wq, kvg, wkv, cos, sin)


def _l1_proj_kernel(x_ref, w_ref, cos_ref, sin_ref, q_ref, k_ref, v_ref):
    xb = x_ref[...].astype(BF16)
    cos = cos_ref[...]
    sin = sin_ref[...]
    scale = DIFF_HEAD_DIM ** -0.5
    qk_w = DIFF_HEADS * 2 * DIFF_HEAD_DIM
    q = jnp.dot(xb, w_ref[:, :qk_w], preferred_element_type=F32)
    for j in range(qk_w // LANES):
        c = j * LANES
        q_ref[:, c:c + LANES] = (_rope(q[:, c:c + LANES], cos, sin) * scale).astype(BF16)
    k = jnp.dot(xb, w_ref[:, qk_w:2 * qk_w], preferred_element_type=F32)
    for j in range(qk_w // LANES):
        c = j * LANES
        k_ref[:, c:c + LANES] = _rope(k[:, c:c + LANES], cos, sin).astype(BF16)
    v_ref[...] = jnp.dot(xb, w_ref[:, 2 * qk_w:], preferred_element_type=F32).astype(BF16)


def _l1_proj(x2d, w_qkv, cos, sin, seq):
    t = x2d.shape[0]
    tm = TOKEN_TILE
    n_pos = seq // tm
    row = lambda w: pl.BlockSpec((tm, w), lambda i: (i, 0))
    pos = pl.BlockSpec((tm, LANES), lambda i: (i % n_pos, 0))
    out = jax.ShapeDtypeStruct((t, D_MODEL), BF16)
    return pl.pallas_call(
        _l1_proj_kernel,
        grid=(t // tm,),
        in_specs=[row(D_MODEL), pl.BlockSpec(w_qkv.shape, lambda i: (0, 0)), pos, pos],
        out_specs=[row(D_MODEL)] * 3,
        out_shape=[out, out, out],
        compiler_params=_params(("parallel",)),
        name="l1_proj",
    )(x2d, w_qkv, cos, sin)


def _attn_kernel(*refs, n_maps, dk, tq, tk, n_chains, sum_col, lambda_init):
    if n_maps == 2:
        q_ref, k_ref, v_ref, lam_ref, g_ref, o_ref, m_sc, l_sc, acc_sc = refs
    else:
        q_ref, k_ref, v_ref, o_ref, m_sc, l_sc, acc_sc = refs
    qi = pl.program_id(2)
    rs = tq // n_chains
    dv = v_ref.shape[2]
    m_sc[...] = jnp.full(m_sc.shape, MASK_VALUE, F32)
    l_sc[...] = jnp.zeros(l_sc.shape, F32)
    acc_sc[...] = jnp.zeros(acc_sc.shape, F32)

    def chain(m, r, k, v, mask):
        rows = slice(r * rs, (r + 1) * rs)
        q = q_ref[0, rows, m * dk:(m + 1) * dk]
        s = lax.dot_general(q, k, (((1,), (1,)), ((), ())), preferred_element_type=F32)
        if mask is not None:
            s = jnp.where(mask, s, MASK_VALUE)
        m_prev = m_sc[m, rows, :]
        m_new = jnp.maximum(m_prev, jnp.max(s, axis=1, keepdims=True))
        p = jnp.exp(s - _repeat_lanes(m_new, s.shape[1] // LANES))
        a = jnp.exp(m_prev - m_new)
        if sum_col is None:
            l_sc[m, rows, :] = a * l_sc[m, rows, :] + jnp.sum(p, axis=1, keepdims=True)
        acc_sc[m, rows, :] = (_repeat_lanes(a, dv // LANES) * acc_sc[m, rows, :]
                              + jnp.dot(p.astype(BF16), v, preferred_element_type=F32))
        m_sc[m, rows, :] = m_new

    def off_diagonal(j, carry):
        start = pl.multiple_of(j * tk, tk)
        v = v_ref[0, pl.ds(start, tk), :]
        for m in range(n_maps):
            k = k_ref[0, pl.ds(start, tk), m * dk:(m + 1) * dk]
            for r in range(n_chains):
                chain(m, r, k, v, None)
        return carry

    lax.fori_loop(0, qi * (tq // tk), off_diagonal, 0)
    base = pl.multiple_of(qi * tq, tq)
    for r in range(n_chains):
        nk = (r + 1) * rs
        v = v_ref[0, pl.ds(base, nk), :]
        row = lax.broadcasted_iota(jnp.int32, (rs, nk), 0) + r * rs
        col = lax.broadcasted_iota(jnp.int32, (rs, nk), 1)
        mask = col <= row
        for m in range(n_maps):
            k = k_ref[0, pl.ds(base, nk), m * dk:(m + 1) * dk]
            chain(m, r, k, v, mask)

    if n_maps == 1:
        acc = acc_sc[0]
        o_ref[0] = (acc[:, :sum_col] / acc[:, sum_col:sum_col + 1]).astype(o_ref.dtype)
    else:
        lam_in = lam_ref[...]
        lam = (jnp.exp(jnp.sum(lam_in[0:1] * lam_in[1:2], axis=1, keepdims=True))
               - jnp.exp(jnp.sum(lam_in[2:3] * lam_in[3:4], axis=1, keepdims=True)) + lambda_init)
        a = acc_sc[0] / l_sc[0][:, 0:1] - lam * (acc_sc[1] / l_sc[1][:, 0:1])
        o_ref[0] = (_rms_norm(a, g_ref[...]) * (1.0 - lambda_init)).astype(o_ref.dtype)


def _attention(q, k, v, n_heads, n_maps, dk, dv_in, dv_out, extra=(), lambda_init=0.0):
    b, s, _ = q.shape
    tq, tk = ATTN_Q_TILE, ATTN_K_TILE
    qw = n_maps * dk
    sum_col = dv_out if dv_in > dv_out else None
    extra_specs = [pl.BlockSpec(e.shape, lambda bi, h, qi: (0, 0)) for e in extra]
    kern = functools.partial(_attn_kernel, n_maps=n_maps, dk=dk, tq=tq, tk=tk, n_chains=ATTN_CHAINS,
                             sum_col=sum_col, lambda_init=lambda_init)
    return pl.pallas_call(
        kern,
        grid=(b, n_heads, s // tq),
        in_specs=[pl.BlockSpec((1, tq, qw), lambda bi, h, qi: (bi, qi, h)),
                  pl.BlockSpec((1, s, qw), lambda bi, h, qi: (bi, 0, h)),
                  pl.BlockSpec((1, s, dv_in), lambda bi, h, qi: (bi, 0, h))] + extra_specs,
        out_specs=pl.BlockSpec((1, tq, dv_out), lambda bi, h, qi: (bi, qi, h)),
        out_shape=jax.ShapeDtypeStruct((b, s, n_heads * dv_out), BF16),
        scratch_shapes=[pltpu.VMEM((n_maps, tq, LANES), F32), pltpu.VMEM((n_maps, tq, LANES), F32),
                        pltpu.VMEM((n_maps, tq, dv_in), F32)],
        compiler_params=_params(("parallel", "parallel", "arbitrary")),
        name="attention_%dmap" % n_maps,
    )(q, k, v, *extra)


def _conv_kernel(u_ref, w_ref, cb_ref, g_ref, b_ref, o_ref, ext, win_sc):
    ts = u_ref.shape[0]
    si = pl.program_id(1)

    @pl.when(si == 0)
    def _():
        ext[0:CONV_HALO, :] = jnp.zeros((CONV_HALO, CONV_CHANNELS), F32)

    @pl.when(si > 0)
    def _():
        ext[0:CONV_HALO, :] = ext[ts:ts + CONV_HALO, :]

    ext[CONV_HALO:CONV_HALO + ts, :] = u_ref[...]
    first = CONV_HALO - (CONV_WIDTH - 1)
    for c in range(ts // CONV_CHUNK):
        r0 = c * CONV_CHUNK
        acc = jnp.zeros((CONV_CHUNK, CONV_CHANNELS), F32)
        for shift in range(ROW_TILE):
            offs = [first + j - shift for j in range(CONV_WIDTH) if (first + j) % ROW_TILE == shift]
            n_win = max(offs) + CONV_CHUNK
            win = win_sc.at[(c * ROW_TILE + shift) % 2]
            win[0:n_win, :] = ext[r0 + shift:r0 + shift + n_win, :]
            for off in offs:
                j = off + shift - first
                acc = acc + w_ref[j:j + 1, :] * win[off:off + CONV_CHUNK, :]
        y = _layer_norm(acc + cb_ref[...], g_ref[...], b_ref[...])
        o_ref[r0:r0 + CONV_CHUNK, :] = (y * jax.nn.sigmoid(y)).astype(o_ref.dtype)


def _conv_module(u2d, conv_w, conv_b, ln_g, ln_b, batch, seq):
    ts = CONV_TILE
    n_s = seq // ts
    vec = pl.BlockSpec((1, CONV_CHANNELS), lambda bi, si: (0, 0))
    return pl.pallas_call(
        _conv_kernel,
        grid=(batch, n_s),
        in_specs=[pl.BlockSpec((ts, CONV_CHANNELS), lambda bi, si: (bi * n_s + si, 0)),
                  pl.BlockSpec(conv_w.shape, lambda bi, si: (0, 0)), vec, vec, vec],
        out_specs=pl.BlockSpec((ts, CONV_CHANNELS), lambda bi, si: (bi * n_s + si, 0)),
        out_shape=jax.ShapeDtypeStruct((batch * seq, CONV_CHANNELS), BF16),
        scratch_shapes=[pltpu.VMEM((ts + CONV_HALO, CONV_CHANNELS), F32),
                        pltpu.VMEM((2, CONV_CHUNK + CONV_HALO, CONV_CHANNELS), F32)],
        compiler_params=_params(("arbitrary", "arbitrary")),
        name="conv_module",
    )(u2d, conv_w, conv_b, ln_g, ln_b)


def _out_ln_kernel(*refs, n_in):
    a_refs = refs[:n_in]
    w_refs = refs[n_in:2 * n_in]
    x_ref, g_ref, b_ref, o_ref, ot_ref = refs[2 * n_in:]
    mix = jnp.dot(a_refs[0][...], w_refs[0][...], preferred_element_type=F32)
    for a_ref, w_ref in zip(a_refs[1:], w_refs[1:]):
        mix = mix + jnp.dot(a_ref[...], w_ref[...], preferred_element_type=F32)
    y = _layer_norm(DN_ALPHA * x_ref[...] + mix, g_ref[...], b_ref[...])
    o_ref[...] = y
    _store_rows(ot_ref, y)


def _out_ln(acts, weights, x2d, g, b):
    t = x2d.shape[0]
    tm = TOKEN_TILE
    vec = pl.BlockSpec((1, D_MODEL), lambda i: (0, 0))
    return pl.pallas_call(
        functools.partial(_out_ln_kernel, n_in=len(acts)),
        grid=(t // tm,),
        in_specs=([pl.BlockSpec((tm, a.shape[1]), lambda i: (i, 0)) for a in acts]
                  + [pl.BlockSpec(w.shape, lambda i: (0, 0)) for w in weights]
                  + [pl.BlockSpec((tm, D_MODEL), lambda i: (i, 0)), vec, vec]),
        out_specs=[pl.BlockSpec((tm, D_MODEL), lambda i: (i, 0)),
                   pl.BlockSpec((tm * ROW_TILE, LANES), lambda i: (i, 0))],
        out_shape=[jax.ShapeDtypeStruct((t, D_MODEL), F32),
                   jax.ShapeDtypeStruct((t * ROW_TILE, LANES), F32)],
        compiler_params=_params(("parallel",)),
        name="out_proj_ln",
    )(*acts, *weights, x2d, g, b)


def _router_kernel(x_ref, rw_ref, rb_ref, tri_ref, low_ref, gate_ref, lrow_ref, before_ref, tcnt_ref,
                   cnt_ref):
    tm = x_ref.shape[0]

    @pl.when(pl.program_id(0) == 0)
    def _():
        cnt_ref[...] = jnp.zeros(cnt_ref.shape, F32)

    x = x_ref[...]
    x_top = pltpu.bitcast(pltpu.bitcast(x, jnp.uint32) & jnp.uint32(0xFFFF0000), F32)
    x_hi = x_top.astype(BF16)
    x_lo = (x - x_top).astype(BF16)
    nt = (((1,), (1,)), ((), ()))
    by_hi = lax.dot_general(rw_ref[...], x_hi, nt, preferred_element_type=F32)
    by_lo = lax.dot_general(rw_ref[...], x_lo, nt, preferred_element_type=F32)
    logits = ((by_hi[:N_EXPERTS] + by_hi[N_EXPERTS:]) + (by_lo[:N_EXPERTS] + by_lo[N_EXPERTS:])
              + rb_ref[...])
    e_iota = lax.broadcasted_iota(jnp.int32, (N_EXPERTS, tm), 0)
    vals, sels = [], []
    work = logits
    for k in range(TOP_K):
        top = jnp.max(work, axis=0, keepdims=True)
        idx = jnp.min(jnp.where(work == top, e_iota, N_EXPERTS), axis=0, keepdims=True)
        sel = e_iota == idx
        vals.append(top)
        sels.append(sel)
        work = jnp.where(sel, -jnp.inf, work)
    exps = [jnp.exp(v - vals[0]) for v in vals]
    denom = exps[0] + exps[1] + exps[2] + exps[3]
    for k in range(TOP_K):
        gate_ref[k:k + 1, :] = exps[k] / denom
    chosen = jnp.where(sels[0] | sels[1] | sels[2] | sels[3], 1.0, 0.0)
    earlier = jnp.dot(chosen.astype(BF16), tri_ref[...], preferred_element_type=F32)
    before = cnt_ref[...]
    tile_cnt = jnp.sum(chosen, axis=1, keepdims=True)
    cnt16 = jnp.floor(tile_cnt * (1.0 / 16.0))
    cnt_r = tile_cnt - 16.0 * cnt16
    prefix = lambda c: jnp.dot(low_ref[...], jnp.broadcast_to(c, before.shape).astype(BF16),
                               preferred_element_type=F32)
    local = 16.0 * prefix(cnt16) + prefix(cnt_r)
    lbase = earlier + local[:, 0:1]
    for k in range(TOP_K):
        lrow = jnp.sum(jnp.where(sels[k], lbase, 0.0), axis=0, keepdims=True)
        lrow_ref[k:k + 1, :] = lrow.astype(jnp.int32)
    before_ref[0] = before
    tcnt_ref[0] = jnp.broadcast_to(tile_cnt, before.shape)
    cnt_ref[...] = before + tile_cnt


def _router(x2d, rw_t, rb, tri):
    t = x2d.shape[0]
    tm = TOKEN_TILE
    n = t // tm
    kt = pl.BlockSpec((TOP_K, tm), lambda i: (0, i))
    per_tile = pl.BlockSpec((1, N_EXPERTS, LANES), lambda i: (i, 0, 0))
    low = (jnp.arange(N_EXPERTS)[:, None] > jnp.arange(N_EXPERTS)[None, :]).astype(BF16)
    ints = jax.ShapeDtypeStruct((TOP_K, t), jnp.int32)
    tiles = jax.ShapeDtypeStruct((n, N_EXPERTS, LANES), F32)
    return pl.pallas_call(
        _router_kernel,
        grid=(n,),
        in_specs=[pl.BlockSpec((tm, D_MODEL), lambda i: (i, 0)),
                  pl.BlockSpec(rw_t.shape, lambda i: (0, 0)),
                  pl.BlockSpec(rb.shape, lambda i: (0, 0)),
                  pl.BlockSpec(tri.shape, lambda i: (0, 0)),
                  pl.BlockSpec(low.shape, lambda i: (0, 0))],
        out_specs=[kt, kt, per_tile, per_tile, pl.BlockSpec((N_EXPERTS, LANES), lambda i: (0, 0))],
        out_shape=[jax.ShapeDtypeStruct((TOP_K, t), F32), ints, tiles, tiles,
                   jax.ShapeDtypeStruct((N_EXPERTS, LANES), F32)],
        compiler_params=_params(("arbitrary",)),
        name="router",
    )(x2d, rw_t, rb, tri, low)


def _dispatch_kernel(pend_ref, padded_ref, rdst_ref, rsrc_ref, rlen_ref, lrow_ref, xt_ref, xs_ref,
                     zeros, stage, zsem, sems):
    td = lrow_ref.shape[0] // TOP_K
    zrows = zeros.shape[0]
    slot_rows = stage.shape[0] // 2
    i = pl.program_id(0)
    slot = i % 2

    @pl.when(pl.program_id(0) == 0)
    def _():
        zeros[...] = jnp.zeros(zeros.shape, F32)

        def zero_copy(e):
            start = pl.multiple_of(pend_ref[e] * ROW_TILE - zrows, zrows)
            return pltpu.make_async_copy(zeros, xs_ref.at[pl.ds(start, zrows)], zsem)

        def start_zero(e, carry):
            @pl.when(padded_ref[e] > 0)
            def _():
                zero_copy(e).start()
            return carry

        def wait_zero(e, carry):
            @pl.when(padded_ref[e] > 0)
            def _():
                zero_copy(e).wait()
            return carry

        lax.fori_loop(0, N_EXPERTS, start_zero, 0)
        lax.fori_loop(0, N_EXPERTS, wait_zero, 0)

    def start_runs(tile, s):
        def per_expert(e, carry):
            n_rows = rlen_ref[tile * N_EXPERTS + e]
            src = rsrc_ref[tile * N_EXPERTS + e]
            dst = rdst_ref[tile * N_EXPERTS + e]
            for bit in reversed(range(td.bit_length())):
                size = (1 << bit) * ROW_TILE
                has = (n_rows & (1 << bit)) != 0

                @pl.when(has)
                def _(src=src, dst=dst, size=size):
                    pltpu.make_async_copy(
                        stage.at[pl.ds(pl.multiple_of(s * slot_rows + src, ROW_TILE), size)],
                        xs_ref.at[pl.ds(pl.multiple_of(dst, ROW_TILE), size)], sems.at[s]).start()

                step = jnp.where(has, size, 0)
                src = src + step
                dst = dst + step
            return carry

        lax.fori_loop(0, N_EXPERTS, per_expert, 0)

    def wait_tile(s):
        pltpu.make_async_copy(stage.at[pl.ds(pl.multiple_of(s * slot_rows, ROW_TILE), slot_rows)],
                              xs_ref.at[pl.ds(0, slot_rows)], sems.at[s]).wait()

    def compact(t8, carry):
        for u in range(ROW_TILE):
            row = xt_ref[pl.ds(pl.multiple_of((t8 * ROW_TILE + u) * ROW_TILE, ROW_TILE), ROW_TILE), :]
            for k in range(TOP_K):
                dst = pl.multiple_of(lrow_ref[t8 * (ROW_TILE * TOP_K) + (u * TOP_K + k)], ROW_TILE)
                stage[pl.ds(dst, ROW_TILE), :] = row
        return carry

    lax.fori_loop(0, td // ROW_TILE, compact, 0)
    start_runs(i, slot)

    @pl.when(i > 0)
    def _():
        wait_tile(1 - slot)

    @pl.when(i == pl.num_programs(0) - 1)
    def _():
        wait_tile(slot)


def _dispatch(pend, padded, rdst, rsrc, rlen, lrow8, xt, n_rows):
    t = lrow8.shape[0] // TOP_K
    td = TOKEN_TILE
    smem = lambda i, *_: (i,)
    return pl.pallas_call(
        _dispatch_kernel,
        grid_spec=pltpu.PrefetchScalarGridSpec(
            num_scalar_prefetch=5,
            grid=(t // td,),
            in_specs=[pl.BlockSpec((TOP_K * td,), smem, memory_space=pltpu.SMEM),
                      pl.BlockSpec((td * ROW_TILE, LANES), lambda i, *_: (i, 0))],
            out_specs=pl.BlockSpec(memory_space=pl.ANY),
            scratch_shapes=[pltpu.VMEM((EXPERT_ROWS * ROW_TILE, LANES), F32),
                            pltpu.VMEM((2 * TOP_K * td * ROW_TILE, LANES), F32),
                            pltpu.SemaphoreType.DMA(()), pltpu.SemaphoreType.DMA((2,))]),
        out_shape=jax.ShapeDtypeStruct((n_rows * ROW_TILE, LANES), F32),
        compiler_params=_params(("arbitrary",)),
        name="moe_dispatch",
    )(pend, padded, rdst, rsrc, rlen, lrow8, xt)


def _expert_kernel(blk_e_ref, grp_ref, nxt_ref, valid_ref, n_used_ref, xs_ref, wgu_hbm, bgu_ref, wdn_hbm,
                   bdn_ref, ys_ref, wgu_f32, wdn_f32, wgu_bf, wdn_bf, sems):
    i = pl.program_id(0)
    rb = xs_ref.shape[0] // ROW_TILE
    active = i < n_used_ref[0]
    new_expert = (i == 0) | (blk_e_ref[i] != blk_e_ref[jnp.maximum(i - 1, 0)])

    def weight_copies(e, slot):
        return (pltpu.make_async_copy(wgu_hbm.at[e], wgu_f32.at[slot], sems.at[0, slot]),
                pltpu.make_async_copy(wdn_hbm.at[e], wdn_f32.at[slot], sems.at[1, slot]))

    @pl.when(active & new_expert)
    def _():
        slot = grp_ref[i] % 2
        e = blk_e_ref[i]
        nxt = nxt_ref[i]

        @pl.when(i == 0)
        def _():
            for cp in weight_copies(e, slot):
                cp.start()

        @pl.when(nxt >= 0)
        def _():
            for cp in weight_copies(nxt, 1 - slot):
                cp.start()

        for cp in weight_copies(e, slot):
            cp.wait()

        def cast(c, carry):
            rows = pl.ds(pl.multiple_of(c * CAST_CHUNK, CAST_CHUNK), CAST_CHUNK)
            wgu_bf[rows, :] = wgu_f32[slot, rows, :].astype(BF16)
            wdn_bf[rows, :] = wdn_f32[slot, rows, :].astype(BF16)
            return carry

        lax.fori_loop(0, D_MODEL // CAST_CHUNK, cast, 0)

    def swiglu_rows(n_rows):
        xb = _load_rows(xs_ref, n_rows).astype(BF16)
        h = jnp.dot(xb, wgu_bf[...], preferred_element_type=F32) + bgu_ref[0]
        gate = jnp.minimum(h[:, :D_EXPERT], SWIGLU_LIMIT)
        up = jnp.clip(h[:, D_EXPERT:], -SWIGLU_LIMIT, SWIGLU_LIMIT)
        act = (up + 1.0) * gate * jax.nn.sigmoid(SWIGLU_ALPHA * gate)
        y = jnp.dot(act.astype(BF16), wdn_bf[...], preferred_element_type=F32) + bdn_ref[0]
        _store_rows(ys_ref, y)

    half_only = valid_ref[i] <= rb // 2

    @pl.when(active & jnp.logical_not(half_only))
    def _():
        swiglu_rows(rb)

    @pl.when(active & half_only)
    def _():
        swiglu_rows(rb // 2)
        ys_ref[rb // 2 * ROW_TILE:, :] = jnp.zeros((rb // 2 * ROW_TILE, LANES), F32)


def _experts(blk_e, valid, n_used, xs, w_gu, b_gu, w_dn, b_dn):
    rb = EXPERT_ROWS
    n_blk = xs.shape[0] // (rb * ROW_TILE)
    ids = jnp.arange(n_blk, dtype=jnp.int32)
    change = jnp.concatenate([jnp.ones((1,), bool), blk_e[1:] != blk_e[:-1]])
    grp = (jnp.cumsum(change.astype(jnp.int32)) - 1).astype(jnp.int32)
    later = jnp.where(change, ids, n_blk)
    nxt_pos = jnp.concatenate([lax.cummin(later[::-1])[::-1][1:], jnp.full((1,), n_blk, jnp.int32)])
    nxt = jnp.where(nxt_pos < n_blk, blk_e[jnp.minimum(nxt_pos, n_blk - 1)], -1).astype(jnp.int32)
    rows = pl.BlockSpec((rb * ROW_TILE, LANES), lambda i, be, gr, nx, va, nu: (jnp.minimum(i, nu[0] - 1), 0))
    per_e = lambda a: pl.BlockSpec((1,) + a.shape[1:], lambda i, be, gr, nx, va, nu: (be[i], 0, 0))
    hbm = pl.BlockSpec(memory_space=pl.ANY)
    return pl.pallas_call(
        _expert_kernel,
        grid_spec=pltpu.PrefetchScalarGridSpec(
            num_scalar_prefetch=5,
            grid=(n_blk,),
            in_specs=[rows, hbm, per_e(b_gu), hbm, per_e(b_dn)],
            out_specs=rows,
            scratch_shapes=[pltpu.VMEM((2,) + w_gu.shape[1:], F32), pltpu.VMEM((2,) + w_dn.shape[1:], F32),
                            pltpu.VMEM(w_gu.shape[1:], BF16), pltpu.VMEM(w_dn.shape[1:], BF16),
                            pltpu.SemaphoreType.DMA((2, 2))]),
        out_shape=jax.ShapeDtypeStruct(xs.shape, F32),
        compiler_params=_params(("arbitrary",)),
        name="moe_experts",
    )(blk_e, grp, nxt, valid, n_used, xs, w_gu, b_gu, w_dn, b_dn)


def _combine_kernel(rdst_ref, rsrc_ref, rlen_ref, lrow_ref, gate_ref, x_ref, ys_ref, g_ref, b_ref, o_ref,
                    stage, moe_sc, sems):
    td = x_ref.shape[0]
    slot_rows = stage.shape[0] // 2
    i = pl.program_id(0)
    slot = i % 2

    def start_runs(tile, s):
        def per_expert(e, carry):
            n_rows = rlen_ref[tile * N_EXPERTS + e]
            src = rdst_ref[tile * N_EXPERTS + e]
            dst = rsrc_ref[tile * N_EXPERTS + e]
            for bit in reversed(range(td.bit_length())):
                size = (1 << bit) * ROW_TILE
                has = (n_rows & (1 << bit)) != 0

                @pl.when(has)
                def _(src=src, dst=dst, size=size):
                    pltpu.make_async_copy(
                        ys_ref.at[pl.ds(pl.multiple_of(src, ROW_TILE), size)],
                        stage.at[pl.ds(pl.multiple_of(s * slot_rows + dst, ROW_TILE), size)],
                        sems.at[s]).start()

                step = jnp.where(has, size, 0)
                src = src + step
                dst = dst + step
            return carry

        lax.fori_loop(0, N_EXPERTS, per_expert, 0)

    @pl.when(i == 0)
    def _():
        start_runs(0, 0)

    @pl.when(i + 1 < pl.num_programs(0))
    def _():
        start_runs(i + 1, 1 - slot)

    pltpu.make_async_copy(ys_ref.at[pl.ds(0, slot_rows)],
                          stage.at[pl.ds(pl.multiple_of(slot * slot_rows, ROW_TILE), slot_rows)],
                          sems.at[slot]).wait()

    def reduce_rows(t8, carry):
        for u in range(ROW_TILE):
            a0 = (t8 * ROW_TILE + u) * TOP_K
            acc = None
            for k in range(TOP_K):
                row = stage[pl.ds(pl.multiple_of(lrow_ref[a0 + k], ROW_TILE), ROW_TILE), :]
                term = gate_ref[a0 + k] * row
                acc = term if acc is None else acc + term
            moe_sc[pl.ds(pl.multiple_of((t8 * ROW_TILE + u) * ROW_TILE, ROW_TILE), ROW_TILE), :] = acc
        return carry

    lax.fori_loop(0, td // ROW_TILE, reduce_rows, 0)
    moe = _load_rows(moe_sc, td)
    o_ref[...] = _layer_norm(DN_ALPHA * x_ref[...] + moe, g_ref[...], b_ref[...])


def _combine(rdst, rsrc, rlen, lrow8, gates_flat, x2d, ys, g, b):
    t = x2d.shape[0]
    td = TOKEN_TILE
    vec = pl.BlockSpec((1, D_MODEL), lambda i, *_: (0, 0))
    smem = pl.BlockSpec((TOP_K * td,), lambda i, *_: (i,), memory_space=pltpu.SMEM)
    return pl.pallas_call(
        _combine_kernel,
        grid_spec=pltpu.PrefetchScalarGridSpec(
            num_scalar_prefetch=3,
            grid=(t // td,),
            in_specs=[smem, smem, pl.BlockSpec((td, D_MODEL), lambda i, *_: (i, 0)),
                      pl.BlockSpec(memory_space=pl.ANY), vec, vec],
            out_specs=pl.BlockSpec((td, D_MODEL), lambda i, *_: (i, 0)),
            scratch_shapes=[pltpu.VMEM((2 * TOP_K * td * ROW_TILE, LANES), F32),
                            pltpu.VMEM((td * ROW_TILE, LANES), F32), pltpu.SemaphoreType.DMA((2,))]),
        out_shape=jax.ShapeDtypeStruct((t, D_MODEL), F32),
        compiler_params=_params(("arbitrary",)),
        name="moe_combine_ln",
    )(rdst, rsrc, rlen, lrow8, gates_flat, x2d, ys, g, b)


def _moe_ln(x2d, xt, router_w, router_b, w_gu, b_gu, w_dn, b_dn, ln_g, ln_b, tri):
    t = x2d.shape[0]
    rb = EXPERT_ROWS
    rw_t = router_w.T.astype(F32)
    rw_top = lax.bitcast_convert_type(
        lax.bitcast_convert_type(rw_t, jnp.uint32) & jnp.uint32(0xFFFF0000), F32)
    rw_split = jnp.concatenate([rw_top.astype(BF16), (rw_t - rw_top).astype(BF16)], axis=0)
    gates, lrow, before, tile_cnt, cnt = _router(x2d, rw_split, router_b.reshape(N_EXPERTS, 1), tri)
    counts = cnt[:, 0].astype(jnp.int32)
    padded = ((counts + rb - 1) // rb) * rb
    pend = jnp.cumsum(padded).astype(jnp.int32)
    pstart = pend - padded
    n_blk = (t * TOP_K) // rb + N_EXPERTS
    n_used = pend[-1] // rb
    blk = jnp.minimum(jnp.arange(n_blk, dtype=jnp.int32), n_used - 1)
    blk_e = jnp.sum((blk[:, None] * rb >= pend[None, :]).astype(jnp.int32), axis=1)
    blk_e = jnp.minimum(blk_e, N_EXPERTS - 1).astype(jnp.int32)
    valid = jnp.clip((pstart + counts)[blk_e] - blk * rb, 0, rb).astype(jnp.int32)
    slot_rows = TOP_K * TOKEN_TILE * ROW_TILE
    tile_parity = (jnp.arange(t * TOP_K, dtype=jnp.int32) // (TOP_K * TOKEN_TILE)) % 2
    lrow8 = (lrow * ROW_TILE).astype(jnp.int32).T.reshape(-1) + tile_parity * slot_rows
    run_len = tile_cnt[:, :, 0].astype(jnp.int32)
    run_src = (jnp.cumsum(run_len, axis=1) - run_len) * ROW_TILE
    run_dst = (pstart[None, :] + before[:, :, 0].astype(jnp.int32)) * ROW_TILE
    flat = lambda a: a.reshape(-1).astype(jnp.int32)

    run_dst, run_src, run_len = flat(run_dst), flat(run_src), flat(run_len)

    xs = _dispatch(pend, padded.astype(jnp.int32), run_dst, run_src, run_len, lrow8, xt, n_blk * rb)
    ys = _experts(blk_e, valid, n_used.reshape(1).astype(jnp.int32), xs,
                  w_gu, b_gu.reshape(N_EXPERTS, 1, -1), w_dn, b_dn.reshape(N_EXPERTS, 1, -1))
    return _combine(run_dst, run_src, run_len, lrow8, gates.T.reshape(-1), x2d, ys,
                    ln_g.reshape(1, -1), ln_b.reshape(1, -1))


def _spread_rope_cols(w):
    half = QK_ROPE_DIM // 2
    z = jnp.zeros(w.shape[:-1] + (LANES // 2 - half,), w.dtype)
    return jnp.concatenate([w[..., :half], z, w[..., half:], z], axis=-1)


def _rope_tables(seq, dim, spread):
    inv_freq = 1.0 / (ROPE_THETA ** (jnp.arange(0, dim, 2, dtype=F32) / dim))
    ang = jnp.arange(seq, dtype=F32)[:, None] * inv_freq[None, :]
    cos, sin = jnp.cos(ang), jnp.sin(ang)
    if spread:
        z = jnp.zeros((seq, LANES // 2 - dim // 2), F32)
        return (jnp.concatenate([cos, z, cos, z], axis=1), jnp.concatenate([-sin, z, sin, z], axis=1))
    return jnp.concatenate([cos, cos], axis=1), jnp.concatenate([-sin, sin], axis=1)


def kernel(x, l0_w_in, l0_q_norm_g, l0_w_q_up, l0_kv_norm_g, l0_w_kv_up, l0_conv_w, l0_conv_b, l0_conv_ln_g, l0_conv_ln_b, l0_w_o, l0_ln1_g, l0_ln1_b, l0_router_w, l0_router_b, l0_w_gu, l0_b_gu, l0_w_dn, l0_b_dn, l0_ln2_g, l0_ln2_b, l1_w_qkv, l1_lambda_q1, l1_lambda_k1, l1_lambda_q2, l1_lambda_k2, l1_subln_g, l1_w_o, l1_ln1_g, l1_ln1_b, l1_router_w, l1_router_b, l1_w_gu, l1_b_gu, l1_w_dn, l1_b_dn, l1_ln2_g, l1_ln2_b):
    b, s, d = x.shape
    t = b * s
    x2d = x.reshape(t, d)
    row = lambda a: a.reshape(1, -1)
    tri = (jnp.arange(TOKEN_TILE)[:, None] < jnp.arange(TOKEN_TILE)[None, :]).astype(BF16)

    o1 = Q_LORA_RANK
    o2 = o1 + KV_LORA_RANK
    o3 = o2 + QK_ROPE_DIM
    w_in = jnp.concatenate([l0_w_in[:, :o2], _spread_rope_cols(l0_w_in[:, o2:o3]), l0_w_in[:, o3:]],
                           axis=1).astype(BF16)
    wq = l0_w_q_up.reshape(Q_LORA_RANK, MLA_HEADS, QK_NOPE_DIM + QK_ROPE_DIM)
    wq = jnp.concatenate([wq[..., :QK_NOPE_DIM], _spread_rope_cols(wq[..., QK_NOPE_DIM:])], axis=-1)
    wq = wq.reshape(Q_LORA_RANK, MLA_HEADS * 2 * LANES).astype(BF16)
    wkv = l0_w_kv_up.reshape(KV_LORA_RANK, MLA_HEADS, QK_NOPE_DIM + V_HEAD_DIM)
    wkv = jnp.concatenate([wkv[..., :QK_NOPE_DIM].reshape(KV_LORA_RANK, -1),
                           wkv[..., QK_NOPE_DIM:].reshape(KV_LORA_RANK, -1)], axis=1).astype(BF16)
    cos0, sin0 = _rope_tables(s, QK_ROPE_DIM, spread=True)
    q, k, v, u = _l0_proj(x2d, w_in, row(l0_q_norm_g), wq, row(l0_kv_norm_g), wkv, cos0, sin0, s)
    hw = MLA_HEADS * 2 * LANES
    attn = _attention(q.reshape(b, s, hw), k.reshape(b, s, hw), v.reshape(b, s, hw),
                      MLA_HEADS, 1, 2 * LANES, 2 * LANES, V_HEAD_DIM)
    conv_w = jnp.concatenate([l0_conv_w, jnp.zeros((1, CONV_CHANNELS), F32)], axis=0)
    uc = _conv_module(u, conv_w, row(l0_conv_b), row(l0_conv_ln_g), row(l0_conv_ln_b), b, s)
    n_attn = MLA_HEADS * V_HEAD_DIM
    w_o = l0_w_o.astype(BF16)
    x2d, xt = _out_ln([attn.reshape(t, n_attn), uc], [w_o[:n_attn], w_o[n_attn:]], x2d,
                      row(l0_ln1_g), row(l0_ln1_b))
    x2d = _moe_ln(x2d, xt, l0_router_w, l0_router_b, l0_w_gu, l0_b_gu, l0_w_dn, l0_b_dn,
                  l0_ln2_g, l0_ln2_b, tri)

    lambda_init = 0.8 - 0.6 * math.exp(-0.3 * 1)
    cos1, sin1 = _rope_tables(s, DIFF_HEAD_DIM, spread=False)
    q, k, v = _l1_proj(x2d, l1_w_qkv.astype(BF16), cos1, sin1, s)
    lam_in = jnp.stack([l1_lambda_q1, l1_lambda_k1, l1_lambda_q2, l1_lambda_k2]).astype(F32)
    dv = 2 * DIFF_HEAD_DIM
    attn = _attention(q.reshape(b, s, d), k.reshape(b, s, d), v.reshape(b, s, d),
                      DIFF_HEADS, 2, DIFF_HEAD_DIM, dv, dv,
                      extra=(lam_in, row(l1_subln_g)), lambda_init=lambda_init)
    x2d, xt = _out_ln([attn.reshape(t, d)], [l1_w_o.astype(BF16)], x2d, row(l1_ln1_g), row(l1_ln1_b))
    x2d = _moe_ln(x2d, xt, l1_router_w, l1_router_b, l1_w_gu, l1_b_gu, l1_w_dn, l1_b_dn,
                  l1_ln2_g, l1_ln2_b, tri)
    return x2d.reshape(b, s, d)
```

```python
import functools
import math

import jax
import jax.numpy as jnp
from jax import lax
from jax.experimental import pallas as pl
from jax.experimental.pallas import tpu as pltpu

F32 = jnp.float32
BF16 = jnp.bfloat16

D_MODEL = 1024
DEPTH = 2
MLA_HEADS = 4
QK_NOPE_DIM = 128
QK_ROPE_DIM = 64
V_HEAD_DIM = 128
Q_LORA_RANK = 384
KV_LORA_RANK = 256
CONV_CHANNELS = D_MODEL - MLA_HEADS * V_HEAD_DIM
CONV_WIDTH = 31
DIFF_HEAD_DIM = 128
DIFF_HEADS = D_MODEL // (2 * DIFF_HEAD_DIM)
N_EXPERTS = 32
TOP_K = 4
D_EXPERT = D_MODEL
SWIGLU_LIMIT = 7.0
SWIGLU_ALPHA = 1.702
ROPE_THETA = 10000.0
DN_ALPHA = (2 * DEPTH) ** 0.25
LN_EPS = 1e-5
RMS_EPS = 1e-6
MASK_VALUE = -1e30

LANES = 128
ROW_TILE = 8
TOKEN_TILE = 512
ATTN_Q_TILE = 2048
ATTN_K_TILE = 1024
ATTN_CHAINS = 8
CONV_TILE = 512
CONV_HALO = 32
CONV_CHUNK = 64
EXPERT_ROWS = 512
CAST_CHUNK = 128
VMEM_LIMIT = 56 * 1024 * 1024


def _params(sem, vmem=VMEM_LIMIT):
    return pltpu.CompilerParams(dimension_semantics=sem, vmem_limit_bytes=vmem)


def _layer_norm(r, g, b):
    mu = jnp.mean(r, axis=-1, keepdims=True)
    d = r - mu
    var = jnp.mean(d * d, axis=-1, keepdims=True)
    return d * lax.rsqrt(var + LN_EPS) * g + b


def _rms_norm(x, g):
    return x * lax.rsqrt(jnp.mean(x * x, axis=-1, keepdims=True) + RMS_EPS) * g


def _rope(x, cos, sin):
    return x * cos + pltpu.roll(x, 64, 1) * sin


def _repeat_lanes(x, n):
    return x if n == 1 else jnp.concatenate([x] * n, axis=1)


def _load_rows(ref, n_rows):
    return jnp.concatenate([ref[pl.ds(j, n_rows, stride=ROW_TILE), :] for j in range(ROW_TILE)], axis=1)


def _store_rows(ref, val):
    n_rows = val.shape[0]
    for j in range(ROW_TILE):
        ref[pl.ds(j, n_rows, stride=ROW_TILE), :] = val[:, j * LANES:(j + 1) * LANES]


def _l0_proj_kernel(x_ref, win_ref, qg_ref, wq_ref, kvg_ref, wkv_ref, cos_ref, sin_ref,
                    q_ref, k_ref, v_ref, u_ref):
    tm = x_ref.shape[0]
    xb = x_ref[...].astype(BF16)
    proj = jnp.dot(xb, win_ref[...], preferred_element_type=F32)
    o1 = Q_LORA_RANK
    o2 = o1 + KV_LORA_RANK
    o3 = o2 + LANES
    o4 = o3 + CONV_CHANNELS
    u_ref[...] = proj[:, o3:o4] * jax.nn.sigmoid(proj[:, o4:])
    cos = cos_ref[...]
    sin = sin_ref[...]
    scale = (QK_NOPE_DIM + QK_ROPE_DIM) ** -0.5
    qn = _rms_norm(proj[:, :o1], qg_ref[...])
    qup = jnp.dot(qn.astype(BF16), wq_ref[...], preferred_element_type=F32)
    kvn = _rms_norm(proj[:, o1:o2], kvg_ref[...])
    kvup = jnp.dot(kvn.astype(BF16), wkv_ref[...], preferred_element_type=F32)
    k_rope = _rope(proj[:, o2:o3], cos, sin).astype(BF16)
    ones_col = jnp.where(lax.broadcasted_iota(jnp.int32, (tm, LANES), 1) == 0, 1.0, 0.0).astype(BF16)
    for h in range(MLA_HEADS):
        c = 2 * LANES * h
        q_ref[:, c:c + LANES] = (qup[:, c:c + LANES] * scale).astype(BF16)
        q_rope = _rope(qup[:, c + LANES:c + 2 * LANES], cos, sin)
        q_ref[:, c + LANES:c + 2 * LANES] = (q_rope * scale).astype(BF16)
        k_ref[:, c:c + LANES] = kvup[:, LANES * h:LANES * (h + 1)].astype(BF16)
        k_ref[:, c + LANES:c + 2 * LANES] = k_rope
        vh = MLA_HEADS * LANES + LANES * h
        v_ref[:, c:c + LANES] = kvup[:, vh:vh + LANES].astype(BF16)
        v_ref[:, c + LANES:c + 2 * LANES] = ones_col


def _l0_proj(x2d, w_in, qg, wq, kvg, wkv, cos, sin, seq):
    t = x2d.shape[0]
    tm = TOKEN_TILE
    n_pos = seq // tm
    full = lambda a: pl.BlockSpec(a.shape, lambda i: (0,) * a.ndim)
    row = lambda w: pl.BlockSpec((tm, w), lambda i: (i, 0))
    pos = pl.BlockSpec((tm, LANES), lambda i: (i % n_pos, 0))
    hw = MLA_HEADS * 2 * LANES
    wide = jax.ShapeDtypeStruct((t, hw), BF16)
    return pl.pallas_call(
        _l0_proj_kernel,
        grid=(t // tm,),
        in_specs=[row(D_MODEL), full(w_in), full(qg), full(wq), full(kvg), full(wkv), pos, pos],
        out_specs=[row(hw), row(hw), row(hw), row(CONV_CHANNELS)],
        out_shape=[wide, wide, wide, jax.ShapeDtypeStruct((t, CONV_CHANNELS), F32)],
        compiler_params=_params(("parallel",)),
        name="l0_proj",
    )(x2d, w_in, qg, wq, kvg, wkv, cos, sin)


def _l1_proj_kernel(x_ref, w_ref, cos_ref, sin_ref, q_ref, k_ref, v_ref):
    xb = x_ref[...].astype(BF16)
    cos = cos_ref[...]
    sin = sin_ref[...]
    scale = DIFF_HEAD_DIM ** -0.5
    qk_w = DIFF_HEADS * 2 * DIFF_HEAD_DIM
    q = jnp.dot(xb, w_ref[:, :qk_w], preferred_element_type=F32)
    for j in range(qk_w // LANES):
        c = j * LANES
        q_ref[:, c:c + LANES] = (_rope(q[:, c:c + LANES], cos, sin) * scale).astype(BF16)
    k = jnp.dot(xb, w_ref[:, qk_w:2 * qk_w], preferred_element_type=F32)
    for j in range(qk_w // LANES):
        c = j * LANES
        k_ref[:, c:c + LANES] = _rope(k[:, c:c + LANES], cos, sin).astype(BF16)
    v_ref[...] = jnp.dot(xb, w_ref[:, 2 * qk_w:], preferred_element_type=F32).astype(BF16)


def _l1_proj(x2d, w_qkv, cos, sin, seq):
    t = x2d.shape[0]
    tm = TOKEN_TILE
    n_pos = seq // tm
    row = lambda w: pl.BlockSpec((tm, w), lambda i: (i, 0))
    pos = pl.BlockSpec((tm, LANES), lambda i: (i % n_pos, 0))
    out = jax.ShapeDtypeStruct((t, D_MODEL), BF16)
    return pl.pallas_call(
        _l1_proj_kernel,
        grid=(t // tm,),
        in_specs=[row(D_MODEL), pl.BlockSpec(w_qkv.shape, lambda i: (0, 0)), pos, pos],
        out_specs=[row(D_MODEL)] * 3,
        out_shape=[out, out, out],
        compiler_params=_params(("parallel",)),
        name="l1_proj",
    )(x2d, w_qkv, cos, sin)


def _attn_kernel(*refs, n_maps, dk, tq, tk, n_chains, sum_col, lambda_init):
    if n_maps == 2:
        q_ref, k_ref, v_ref, lam_ref, g_ref, o_ref, m_sc, l_sc, acc_sc = refs
    else:
        q_ref, k_ref, v_ref, o_ref, m_sc, l_sc, acc_sc = refs
    qi = pl.program_id(2)
    rs = tq // n_chains
    dv = v_ref.shape[2]
    m_sc[...] = jnp.full(m_sc.shape, MASK_VALUE, F32)
    l_sc[...] = jnp.zeros(l_sc.shape, F32)
    acc_sc[...] = jnp.zeros(acc_sc.shape, F32)

    def chain(m, r, k, v, mask):
        rows = slice(r * rs, (r + 1) * rs)
        q = q_ref[0, rows, m * dk:(m + 1) * dk]
        s = lax.dot_general(q, k, (((1,), (1,)), ((), ())), preferred_element_type=F32)
        if mask is not None:
            s = jnp.where(mask, s, MASK_VALUE)
        m_prev = m_sc[m, rows, :]
        m_new = jnp.maximum(m_prev, jnp.max(s, axis=1, keepdims=True))
        p = jnp.exp(s - _repeat_lanes(m_new, s.shape[1] // LANES))
        a = jnp.exp(m_prev - m_new)
        if sum_col is None:
            l_sc[m, rows, :] = a * l_sc[m, rows, :] + jnp.sum(p, axis=1, keepdims=True)
        acc_sc[m, rows, :] = (_repeat_lanes(a, dv // LANES) * acc_sc[m, rows, :]
                              + jnp.dot(p.astype(BF16), v, preferred_element_type=F32))
        m_sc[m, rows, :] = m_new

    def off_diagonal(j, carry):
        start = pl.multiple_of(j * tk, tk)
        v = v_ref[0, pl.ds(start, tk), :]
        for m in range(n_maps):
            k = k_ref[0, pl.ds(start, tk), m * dk:(m + 1) * dk]
            for r in range(n_chains):
                chain(m, r, k, v, None)
        return carry

    lax.fori_loop(0, qi * (tq // tk), off_diagonal, 0)
    base = pl.multiple_of(qi * tq, tq)
    for r in range(n_chains):
        nk = (r + 1) * rs
        v = v_ref[0, pl.ds(base, nk), :]
        row = lax.broadcasted_iota(jnp.int32, (rs, nk), 0) + r * rs
        col = lax.broadcasted_iota(jnp.int32, (rs, nk), 1)
        mask = col <= row
        for m in range(n_maps):
            k = k_ref[0, pl.ds(base, nk), m * dk:(m + 1) * dk]
            chain(m, r, k, v, mask)

    if n_maps == 1:
        acc = acc_sc[0]
        o_ref[0] = (acc[:, :sum_col] / acc[:, sum_col:sum_col + 1]).astype(o_ref.dtype)
    else:
        lam_in = lam_ref[...]
        lam = (jnp.exp(jnp.sum(lam_in[0:1] * lam_in[1:2], axis=1, keepdims=True))
               - jnp.exp(jnp.sum(lam_in[2:3] * lam_in[3:4], axis=1, keepdims=True)) + lambda_init)
        a = acc_sc[0] / l_sc[0][:, 0:1] - lam * (acc_sc[1] / l_sc[1][:, 0:1])
        o_ref[0] = (_rms_norm(a, g_ref[...]) * (1.0 - lambda_init)).astype(o_ref.dtype)


def _attention(q, k, v, n_heads, n_maps, dk, dv_in, dv_out, extra=(), lambda_init=0.0):
    b, s, _ = q.shape
    tq, tk = ATTN_Q_TILE, ATTN_K_TILE
    qw = n_maps * dk
    sum_col = dv_out if dv_in > dv_out else None
    extra_specs = [pl.BlockSpec(e.shape, lambda bi, h, qi: (0, 0)) for e in extra]
    kern = functools.partial(_attn_kernel, n_maps=n_maps, dk=dk, tq=tq, tk=tk, n_chains=ATTN_CHAINS,
                             sum_col=sum_col, lambda_init=lambda_init)
    return pl.pallas_call(
        kern,
        grid=(b, n_heads, s // tq),
        in_specs=[pl.BlockSpec((1, tq, qw), lambda bi, h, qi: (bi, qi, h)),
                  pl.BlockSpec((1, s, qw), lambda bi, h, qi: (bi, 0, h)),
                  pl.BlockSpec((1, s, dv_in), lambda bi, h, qi: (bi, 0, h))] + extra_specs,
        out_specs=pl.BlockSpec((1, tq, dv_out), lambda bi, h, qi: (bi, qi, h)),
        out_shape=jax.ShapeDtypeStruct((b, s, n_heads * dv_out), BF16),
        scratch_shapes=[pltpu.VMEM((n_maps, tq, LANES), F32), pltpu.VMEM((n_maps, tq, LANES), F32),
                        pltpu.VMEM((n_maps, tq, dv_in), F32)],
        compiler_params=_params(("parallel", "parallel", "arbitrary")),
        name="attention_%dmap" % n_maps,
    )(q, k, v, *extra)


def _conv_kernel(u_ref, w_ref, cb_ref, g_ref, b_ref, o_ref, ext, win_sc):
    ts = u_ref.shape[0]
    si = pl.program_id(1)

    @pl.when(si == 0)
    def _():
        ext[0:CONV_HALO, :] = jnp.zeros((CONV_HALO, CONV_CHANNELS), F32)

    @pl.when(si > 0)
    def _():
        ext[0:CONV_HALO, :] = ext[ts:ts + CONV_HALO, :]

    ext[CONV_HALO:CONV_HALO + ts, :] = u_ref[...]
    first = CONV_HALO - (CONV_WIDTH - 1)
    for c in range(ts // CONV_CHUNK):
        r0 = c * CONV_CHUNK
        acc = jnp.zeros((CONV_CHUNK, CONV_CHANNELS), F32)
        for shift in range(ROW_TILE):
            offs = [first + j - shift for j in range(CONV_WIDTH) if (first + j) % ROW_TILE == shift]
            n_win = max(offs) + CONV_CHUNK
            win = win_sc.at[(c * ROW_TILE + shift) % 2]
            win[0:n_win, :] = ext[r0 + shift:r0 + shift + n_win, :]
            for off in offs:
                j = off + shift - first
                acc = acc + w_ref[j:j + 1, :] * win[off:off + CONV_CHUNK, :]
        y = _layer_norm(acc + cb_ref[...], g_ref[...], b_ref[...])
        o_ref[r0:r0 + CONV_CHUNK, :] = (y * jax.nn.sigmoid(y)).astype(o_ref.dtype)


def _conv_module(u2d, conv_w, conv_b, ln_g, ln_b, batch, seq):
    ts = CONV_TILE
    n_s = seq // ts
    vec = pl.BlockSpec((1, CONV_CHANNELS), lambda bi, si: (0, 0))
    return pl.pallas_call(
        _conv_kernel,
        grid=(batch, n_s),
        in_specs=[pl.BlockSpec((ts, CONV_CHANNELS), lambda bi, si: (bi * n_s + si, 0)),
                  pl.BlockSpec(conv_w.shape, lambda bi, si: (0, 0)), vec, vec, vec],
        out_specs=pl.BlockSpec((ts, CONV_CHANNELS), lambda bi, si: (bi * n_s + si, 0)),
        out_shape=jax.ShapeDtypeStruct((batch * seq, CONV_CHANNELS), BF16),
        scratch_shapes=[pltpu.VMEM((ts + CONV_HALO, CONV_CHANNELS), F32),
                        pltpu.VMEM((2, CONV_CHUNK + CONV_HALO, CONV_CHANNELS), F32)],
        compiler_params=_params(("arbitrary", "arbitrary")),
        name="conv_module",
    )(u2d, conv_w, conv_b, ln_g, ln_b)


def _out_ln_kernel(*refs, n_in):
    a_refs = refs[:n_in]
    w_refs = refs[n_in:2 * n_in]
    x_ref, g_ref, b_ref, o_ref, ot_ref = refs[2 * n_in:]
    mix = jnp.dot(a_refs[0][...], w_refs[0][...], preferred_element_type=F32)
    for a_ref, w_ref in zip(a_refs[1:], w_refs[1:]):
        mix = mix + jnp.dot(a_ref[...], w_ref[...], preferred_element_type=F32)
    y = _layer_norm(DN_ALPHA * x_ref[...] + mix, g_ref[...], b_ref[...])
    o_ref[...] = y
    _store_rows(ot_ref, y)


def _out_ln(acts, weights, x2d, g, b):
    t = x2d.shape[0]
    tm = TOKEN_TILE
    vec = pl.BlockSpec((1, D_MODEL), lambda i: (0, 0))
    return pl.pallas_call(
        functools.partial(_out_ln_kernel, n_in=len(acts)),
        grid=(t // tm,),
        in_specs=([pl.BlockSpec((tm, a.shape[1]), lambda i: (i, 0)) for a in acts]
                  + [pl.BlockSpec(w.shape, lambda i: (0, 0)) for w in weights]
                  + [pl.BlockSpec((tm, D_MODEL), lambda i: (i, 0)), vec, vec]),
        out_specs=[pl.BlockSpec((tm, D_MODEL), lambda i: (i, 0)),
                   pl.BlockSpec((tm * ROW_TILE, LANES), lambda i: (i, 0))],
        out_shape=[jax.ShapeDtypeStruct((t, D_MODEL), F32),
                   jax.ShapeDtypeStruct((t * ROW_TILE, LANES), F32)],
        compiler_params=_params(("parallel",)),
        name="out_proj_ln",
    )(*acts, *weights, x2d, g, b)


def _router_kernel(x_ref, rw_ref, rb_ref, tri_ref, low_ref, gate_ref, lrow_ref, before_ref, tcnt_ref,
                   cnt_ref):
    tm = x_ref.shape[0]
    slot_base = (pl.program_id(0) % 2) * (TOP_K * tm * ROW_TILE)

    @pl.when(pl.program_id(0) == 0)
    def _():
        cnt_ref[...] = jnp.zeros(cnt_ref.shape, F32)

    x = x_ref[...]
    x_top = pltpu.bitcast(pltpu.bitcast(x, jnp.uint32) & jnp.uint32(0xFFFF0000), F32)
    x_hi = x_top.astype(BF16)
    x_lo = (x - x_top).astype(BF16)
    nt = (((1,), (1,)), ((), ()))
    by_hi = lax.dot_general(rw_ref[...], x_hi, nt, preferred_element_type=F32)
    by_lo = lax.dot_general(rw_ref[...], x_lo, nt, preferred_element_type=F32)
    logits = ((by_hi[:N_EXPERTS] + by_hi[N_EXPERTS:]) + (by_lo[:N_EXPERTS] + by_lo[N_EXPERTS:])
              + rb_ref[...])
    e_iota = lax.broadcasted_iota(jnp.int32, (N_EXPERTS, tm), 0)
    vals, sels = [], []
    work = logits
    for k in range(TOP_K):
        top = jnp.max(work, axis=0, keepdims=True)
        idx = jnp.min(jnp.where(work == top, e_iota, N_EXPERTS), axis=0, keepdims=True)
        sel = e_iota == idx
        vals.append(top)
        sels.append(sel)
        work = jnp.where(sel, -jnp.inf, work)
    exps = [jnp.exp(v - vals[0]) for v in vals]
    denom = exps[0] + exps[1] + exps[2] + exps[3]
    for k in range(TOP_K):
        gate_ref[0, k:k + 1, :] = exps[k] / denom
    chosen = jnp.where(sels[0] | sels[1] | sels[2] | sels[3], 1.0, 0.0)
    earlier = jnp.dot(chosen.astype(BF16), tri_ref[...], preferred_element_type=F32)
    before = cnt_ref[...]
    tile_cnt = jnp.sum(chosen, axis=1, keepdims=True)
    cnt16 = jnp.floor(tile_cnt * (1.0 / 16.0))
    cnt_r = tile_cnt - 16.0 * cnt16
    prefix = lambda c: jnp.dot(low_ref[...], jnp.broadcast_to(c, before.shape).astype(BF16),
                               preferred_element_type=F32)
    local = 16.0 * prefix(cnt16) + prefix(cnt_r)
    lbase = earlier + local[:, 0:1]
    for k in range(TOP_K):
        lrow = jnp.sum(jnp.where(sels[k], lbase, 0.0), axis=0, keepdims=True)
        lrow_ref[0, k:k + 1, :] = lrow.astype(jnp.int32) * ROW_TILE + slot_base
    before_ref[0] = before
    tcnt_ref[0] = jnp.broadcast_to(tile_cnt, before.shape)
    cnt_ref[...] = before + tile_cnt


def _router(x2d, rw_t, rb, tri):
    t = x2d.shape[0]
    tm = TOKEN_TILE
    n = t // tm
    kt = pl.BlockSpec((1, TOP_K, tm), lambda i: (i, 0, 0))
    per_tile = pl.BlockSpec((1, N_EXPERTS, LANES), lambda i: (i, 0, 0))
    low = (jnp.arange(N_EXPERTS)[:, None] > jnp.arange(N_EXPERTS)[None, :]).astype(BF16)
    ints = jax.ShapeDtypeStruct((n, TOP_K, tm), jnp.int32)
    tiles = jax.ShapeDtypeStruct((n, N_EXPERTS, LANES), F32)
    return pl.pallas_call(
        _router_kernel,
        grid=(n,),
        in_specs=[pl.BlockSpec((tm, D_MODEL), lambda i: (i, 0)),
                  pl.BlockSpec(rw_t.shape, lambda i: (0, 0)),
                  pl.BlockSpec(rb.shape, lambda i: (0, 0)),
                  pl.BlockSpec(tri.shape, lambda i: (0, 0)),
                  pl.BlockSpec(low.shape, lambda i: (0, 0))],
        out_specs=[kt, kt, per_tile, per_tile, pl.BlockSpec((N_EXPERTS, LANES), lambda i: (0, 0))],
        out_shape=[jax.ShapeDtypeStruct((n, TOP_K, tm), F32), ints, tiles, tiles,
                   jax.ShapeDtypeStruct((N_EXPERTS, LANES), F32)],
        compiler_params=_params(("arbitrary",)),
        name="router",
    )(x2d, rw_t, rb, tri, low)


def _dispatch_kernel(pend_ref, padded_ref, rdst_ref, rsrc_ref, rlen_ref, lrow_ref, xt_ref, xs_ref,
                     zeros, stage, zsem, sems):
    td = lrow_ref.shape[0] // TOP_K
    zrows = zeros.shape[0]
    slot_rows = stage.shape[0] // 2
    i = pl.program_id(0)
    slot = i % 2

    @pl.when(pl.program_id(0) == 0)
    def _():
        zeros[...] = jnp.zeros(zeros.shape, F32)

        def zero_copy(e):
            start = pl.multiple_of(pend_ref[e] * ROW_TILE - zrows, zrows)
            return pltpu.make_async_copy(zeros, xs_ref.at[pl.ds(start, zrows)], zsem)

        def start_zero(e, carry):
            @pl.when(padded_ref[e] > 0)
            def _():
                zero_copy(e).start()
            return carry

        def wait_zero(e, carry):
            @pl.when(padded_ref[e] > 0)
            def _():
                zero_copy(e).wait()
            return carry

        lax.fori_loop(0, N_EXPERTS, start_zero, 0)
        lax.fori_loop(0, N_EXPERTS, wait_zero, 0)

    def start_runs(tile, s):
        def per_expert(e, carry):
            n_rows = rlen_ref[tile * N_EXPERTS + e]
            src = rsrc_ref[tile * N_EXPERTS + e]
            dst = rdst_ref[tile * N_EXPERTS + e]
            for bit in reversed(range(td.bit_length())):
                size = (1 << bit) * ROW_TILE
                has = (n_rows & (1 << bit)) != 0

                @pl.when(has)
                def _(src=src, dst=dst, size=size):
                    pltpu.make_async_copy(
                        stage.at[pl.ds(pl.multiple_of(s * slot_rows + src, ROW_TILE), size)],
                        xs_ref.at[pl.ds(pl.multiple_of(dst, ROW_TILE), size)], sems.at[s]).start()

                step = jnp.where(has, size, 0)
                src = src + step
                dst = dst + step
            return carry

        lax.fori_loop(0, N_EXPERTS, per_expert, 0)

    def wait_tile(s):
        pltpu.make_async_copy(stage.at[pl.ds(pl.multiple_of(s * slot_rows, ROW_TILE), slot_rows)],
                              xs_ref.at[pl.ds(0, slot_rows)], sems.at[s]).wait()

    def compact(t8, carry):
        for u in range(ROW_TILE):
            row = xt_ref[pl.ds(pl.multiple_of((t8 * ROW_TILE + u) * ROW_TILE, ROW_TILE), ROW_TILE), :]
            for k in range(TOP_K):
                dst = pl.multiple_of(lrow_ref[k * td + t8 * ROW_TILE + u], ROW_TILE)
                stage[pl.ds(dst, ROW_TILE), :] = row
        return carry

    lax.fori_loop(0, td // ROW_TILE, compact, 0)
    start_runs(i, slot)

    @pl.when(i > 0)
    def _():
        wait_tile(1 - slot)

    @pl.when(i == pl.num_programs(0) - 1)
    def _():
        wait_tile(slot)


def _dispatch(pend, padded, rdst, rsrc, rlen, lrow8, xt, n_rows):
    t = lrow8.shape[0] // TOP_K
    td = TOKEN_TILE
    smem = lambda i, *_: (i,)
    return pl.pallas_call(
        _dispatch_kernel,
        grid_spec=pltpu.PrefetchScalarGridSpec(
            num_scalar_prefetch=5,
            grid=(t // td,),
            in_specs=[pl.BlockSpec((TOP_K * td,), smem, memory_space=pltpu.SMEM),
                      pl.BlockSpec((td * ROW_TILE, LANES), lambda i, *_: (i, 0))],
            out_specs=pl.BlockSpec(memory_space=pl.ANY),
            scratch_shapes=[pltpu.VMEM((EXPERT_ROWS * ROW_TILE, LANES), F32),
                            pltpu.VMEM((2 * TOP_K * td * ROW_TILE, LANES), F32),
                            pltpu.SemaphoreType.DMA(()), pltpu.SemaphoreType.DMA((2,))]),
        out_shape=jax.ShapeDtypeStruct((n_rows * ROW_TILE, LANES), F32),
        compiler_params=_params(("arbitrary",)),
        name="moe_dispatch",
    )(pend, padded, rdst, rsrc, rlen, lrow8, xt)


def _expert_kernel(blk_e_ref, grp_ref, nxt_ref, valid_ref, n_used_ref, xs_ref, wgu_hbm, bgu_ref, wdn_hbm,
                   bdn_ref, ys_ref, wgu_f32, wdn_f32, wgu_bf, wdn_bf, sems):
    i = pl.program_id(0)
    rb = xs_ref.shape[0] // ROW_TILE
    active = i < n_used_ref[0]
    new_expert = (i == 0) | (blk_e_ref[i] != blk_e_ref[jnp.maximum(i - 1, 0)])

    def weight_copies(e, slot):
        return (pltpu.make_async_copy(wgu_hbm.at[e], wgu_f32.at[slot], sems.at[0, slot]),
                pltpu.make_async_copy(wdn_hbm.at[e], wdn_f32.at[slot], sems.at[1, slot]))

    @pl.when(active & new_expert)
    def _():
        slot = grp_ref[i] % 2
        e = blk_e_ref[i]
        nxt = nxt_ref[i]

        @pl.when(i == 0)
        def _():
            for cp in weight_copies(e, slot):
                cp.start()

        @pl.when(nxt >= 0)
        def _():
            for cp in weight_copies(nxt, 1 - slot):
                cp.start()

        for cp in weight_copies(e, slot):
            cp.wait()

        def cast(c, carry):
            rows = pl.ds(pl.multiple_of(c * CAST_CHUNK, CAST_CHUNK), CAST_CHUNK)
            wgu_bf[rows, :] = wgu_f32[slot, rows, :].astype(BF16)
            wdn_bf[rows, :] = wdn_f32[slot, rows, :].astype(BF16)
            return carry

        lax.fori_loop(0, D_MODEL // CAST_CHUNK, cast, 0)

    def swiglu_rows(n_rows):
        xb = _load_rows(xs_ref, n_rows).astype(BF16)
        h = jnp.dot(xb, wgu_bf[...], preferred_element_type=F32) + bgu_ref[0]
        gate = jnp.minimum(h[:, :D_EXPERT], SWIGLU_LIMIT)
        up = jnp.clip(h[:, D_EXPERT:], -SWIGLU_LIMIT, SWIGLU_LIMIT)
        act = (up + 1.0) * gate * jax.nn.sigmoid(SWIGLU_ALPHA * gate)
        y = jnp.dot(act.astype(BF16), wdn_bf[...], preferred_element_type=F32) + bdn_ref[0]
        _store_rows(ys_ref, y)

    half_only = valid_ref[i] <= rb // 2

    @pl.when(active & jnp.logical_not(half_only))
    def _():
        swiglu_rows(rb)

    @pl.when(active & half_only)
    def _():
        swiglu_rows(rb // 2)
        ys_ref[rb // 2 * ROW_TILE:, :] = jnp.zeros((rb // 2 * ROW_TILE, LANES), F32)


def _experts(blk_e, valid, n_used, xs, w_gu, b_gu, w_dn, b_dn):
    rb = EXPERT_ROWS
    n_blk = xs.shape[0] // (rb * ROW_TILE)
    ids = jnp.arange(n_blk, dtype=jnp.int32)
    change = jnp.concatenate([jnp.ones((1,), bool), blk_e[1:] != blk_e[:-1]])
    grp = (jnp.cumsum(change.astype(jnp.int32)) - 1).astype(jnp.int32)
    later = jnp.where(change, ids, n_blk)
    nxt_pos = jnp.concatenate([lax.cummin(later[::-1])[::-1][1:], jnp.full((1,), n_blk, jnp.int32)])
    nxt = jnp.where(nxt_pos < n_blk, blk_e[jnp.minimum(nxt_pos, n_blk - 1)], -1).astype(jnp.int32)
    rows = pl.BlockSpec((rb * ROW_TILE, LANES), lambda i, be, gr, nx, va, nu: (jnp.minimum(i, nu[0] - 1), 0))
    per_e = lambda a: pl.BlockSpec((1,) + a.shape[1:], lambda i, be, gr, nx, va, nu: (be[i], 0, 0))
    hbm = pl.BlockSpec(memory_space=pl.ANY)
    return pl.pallas_call(
        _expert_kernel,
        grid_spec=pltpu.PrefetchScalarGridSpec(
            num_scalar_prefetch=5,
            grid=(n_blk,),
            in_specs=[rows, hbm, per_e(b_gu), hbm, per_e(b_dn)],
            out_specs=rows,
            scratch_shapes=[pltpu.VMEM((2,) + w_gu.shape[1:], F32), pltpu.VMEM((2,) + w_dn.shape[1:], F32),
                            pltpu.VMEM(w_gu.shape[1:], BF16), pltpu.VMEM(w_dn.shape[1:], BF16),
                            pltpu.SemaphoreType.DMA((2, 2))]),
        out_shape=jax.ShapeDtypeStruct(xs.shape, F32),
        compiler_params=_params(("arbitrary",)),
        name="moe_experts",
    )(blk_e, grp, nxt, valid, n_used, xs, w_gu, b_gu, w_dn, b_dn)


def _combine_kernel(rdst_ref, rsrc_ref, rlen_ref, lrow_ref, gate_ref, x_ref, ys_ref, g_ref, b_ref, o_ref,
                    stage, moe_sc, sems):
    td = x_ref.shape[0]
    slot_rows = stage.shape[0] // 2
    i = pl.program_id(0)
    slot = i % 2

    def start_runs(tile, s):
        def per_expert(e, carry):
            n_rows = rlen_ref[tile * N_EXPERTS + e]
            src = rdst_ref[tile * N_EXPERTS + e]
            dst = rsrc_ref[tile * N_EXPERTS + e]
            for bit in reversed(range(td.bit_length())):
                size = (1 << bit) * ROW_TILE
                has = (n_rows & (1 << bit)) != 0

                @pl.when(has)
                def _(src=src, dst=dst, size=size):
                    pltpu.make_async_copy(
                        ys_ref.at[pl.ds(pl.multiple_of(src, ROW_TILE), size)],
                        stage.at[pl.ds(pl.multiple_of(s * slot_rows + dst, ROW_TILE), size)],
                        sems.at[s]).start()

                step = jnp.where(has, size, 0)
                src = src + step
                dst = dst + step
            return carry

        lax.fori_loop(0, N_EXPERTS, per_expert, 0)

    @pl.when(i == 0)
    def _():
        start_runs(0, 0)

    @pl.when(i + 1 < pl.num_programs(0))
    def _():
        start_runs(i + 1, 1 - slot)

    pltpu.make_async_copy(ys_ref.at[pl.ds(0, slot_rows)],
                          stage.at[pl.ds(pl.multiple_of(slot * slot_rows, ROW_TILE), slot_rows)],
                          sems.at[slot]).wait()

    def reduce_rows(t8, carry):
        for u in range(ROW_TILE):
            tok = t8 * ROW_TILE + u
            acc = None
            for k in range(TOP_K):
                row = stage[pl.ds(pl.multiple_of(lrow_ref[k * td + tok], ROW_TILE), ROW_TILE), :]
                term = gate_ref[k * td + tok] * row
                acc = term if acc is None else acc + term
            moe_sc[pl.ds(pl.multiple_of((t8 * ROW_TILE + u) * ROW_TILE, ROW_TILE), ROW_TILE), :] = acc
        return carry

    lax.fori_loop(0, td // ROW_TILE, reduce_rows, 0)
    moe = _load_rows(moe_sc, td)
    o_ref[...] = _layer_norm(DN_ALPHA * x_ref[...] + moe, g_ref[...], b_ref[...])


def _combine(rdst, rsrc, rlen, lrow8, gates_flat, x2d, ys, g, b):
    t = x2d.shape[0]
    td = TOKEN_TILE
    vec = pl.BlockSpec((1, D_MODEL), lambda i, *_: (0, 0))
    smem = pl.BlockSpec((TOP_K * td,), lambda i, *_: (i,), memory_space=pltpu.SMEM)
    return pl.pallas_call(
        _combine_kernel,
        grid_spec=pltpu.PrefetchScalarGridSpec(
            num_scalar_prefetch=3,
            grid=(t // td,),
            in_specs=[smem, smem, pl.BlockSpec((td, D_MODEL), lambda i, *_: (i, 0)),
                      pl.BlockSpec(memory_space=pl.ANY), vec, vec],
            out_specs=pl.BlockSpec((td, D_MODEL), lambda i, *_: (i, 0)),
            scratch_shapes=[pltpu.VMEM((2 * TOP_K * td * ROW_TILE, LANES), F32),
                            pltpu.VMEM((td * ROW_TILE, LANES), F32), pltpu.SemaphoreType.DMA((2,))]),
        out_shape=jax.ShapeDtypeStruct((t, D_MODEL), F32),
        compiler_params=_params(("arbitrary",)),
        name="moe_combine_ln",
    )(rdst, rsrc, rlen, lrow8, gates_flat, x2d, ys, g, b)


def _moe_ln(x2d, xt, router_w, router_b, w_gu, b_gu, w_dn, b_dn, ln_g, ln_b, tri):
    t = x2d.shape[0]
    rb = EXPERT_ROWS
    rw_t = router_w.T.astype(F32)
    rw_top = lax.bitcast_convert_type(
        lax.bitcast_convert_type(rw_t, jnp.uint32) & jnp.uint32(0xFFFF0000), F32)
    rw_split = jnp.concatenate([rw_top.astype(BF16), (rw_t - rw_top).astype(BF16)], axis=0)
    gates, lrow, before, tile_cnt, cnt = _router(x2d, rw_split, router_b.reshape(N_EXPERTS, 1), tri)
    counts = cnt[:, 0].astype(jnp.int32)
    padded = ((counts + rb - 1) // rb) * rb
    pend = jnp.cumsum(padded).astype(jnp.int32)
    pstart = pend - padded
    n_blk = (t * TOP_K) // rb + N_EXPERTS
    n_used = pend[-1] // rb
    blk = jnp.minimum(jnp.arange(n_blk, dtype=jnp.int32), n_used - 1)
    blk_e = jnp.sum((blk[:, None] * rb >= pend[None, :]).astype(jnp.int32), axis=1)
    blk_e = jnp.minimum(blk_e, N_EXPERTS - 1).astype(jnp.int32)
    valid = jnp.clip((pstart + counts)[blk_e] - blk * rb, 0, rb).astype(jnp.int32)
    lrow8 = lrow.reshape(-1)
    run_len = tile_cnt[:, :, 0].astype(jnp.int32)
    run_src = (jnp.cumsum(run_len, axis=1) - run_len) * ROW_TILE
    run_dst = (pstart[None, :] + before[:, :, 0].astype(jnp.int32)) * ROW_TILE
    flat = lambda a: a.reshape(-1).astype(jnp.int32)

    run_dst, run_src, run_len = flat(run_dst), flat(run_src), flat(run_len)

    xs = _dispatch(pend, padded.astype(jnp.int32), run_dst, run_src, run_len, lrow8, xt, n_blk * rb)
    ys = _experts(blk_e, valid, n_used.reshape(1).astype(jnp.int32), xs,
                  w_gu, b_gu.reshape(N_EXPERTS, 1, -1), w_dn, b_dn.reshape(N_EXPERTS, 1, -1))
    return _combine(run_dst, run_src, run_len, lrow8, gates.reshape(-1), x2d, ys,
                    ln_g.reshape(1, -1), ln_b.reshape(1, -1))


def _spread_rope_cols(w):
    half = QK_ROPE_DIM // 2
    z = jnp.zeros(w.shape[:-1] + (LANES // 2 - half,), w.dtype)
    return jnp.concatenate([w[..., :half], z, w[..., half:], z], axis=-1)


def _rope_tables(seq, dim, spread):
    inv_freq = 1.0 / (ROPE_THETA ** (jnp.arange(0, dim, 2, dtype=F32) / dim))
    ang = jnp.arange(seq, dtype=F32)[:, None] * inv_freq[None, :]
    cos, sin = jnp.cos(ang), jnp.sin(ang)
    if spread:
        z = jnp.zeros((seq, LANES // 2 - dim // 2), F32)
        return (jnp.concatenate([cos, z, cos, z], axis=1), jnp.concatenate([-sin, z, sin, z], axis=1))
    return jnp.concatenate([cos, cos], axis=1), jnp.concatenate([-sin, sin], axis=1)


def kernel(x, l0_w_in, l0_q_norm_g, l0_w_q_up, l0_kv_norm_g, l0_w_kv_up, l0_conv_w, l0_conv_b, l0_conv_ln_g, l0_conv_ln_b, l0_w_o, l0_ln1_g, l0_ln1_b, l0_router_w, l0_router_b, l0_w_gu, l0_b_gu, l0_w_dn, l0_b_dn, l0_ln2_g, l0_ln2_b, l1_w_qkv, l1_lambda_q1, l1_lambda_k1, l1_lambda_q2, l1_lambda_k2, l1_subln_g, l1_w_o, l1_ln1_g, l1_ln1_b, l1_router_w, l1_router_b, l1_w_gu, l1_b_gu, l1_w_dn, l1_b_dn, l1_ln2_g, l1_ln2_b):
    b, s, d = x.shape
    t = b * s
    x2d = x.reshape(t, d)
    row = lambda a: a.reshape(1, -1)
    tri = (jnp.arange(TOKEN_TILE)[:, None] < jnp.arange(TOKEN_TILE)[None, :]).astype(BF16)

    o1 = Q_LORA_RANK
    o2 = o1 + KV_LORA_RANK
    o3 = o2 + QK_ROPE_DIM
    w_in = jnp.concatenate([l0_w_in[:, :o2], _spread_rope_cols(l0_w_in[:, o2:o3]), l0_w_in[:, o3:]],
                           axis=1).astype(BF16)
    wq = l0_w_q_up.reshape(Q_LORA_RANK, MLA_HEADS, QK_NOPE_DIM + QK_ROPE_DIM)
    wq = jnp.concatenate([wq[..., :QK_NOPE_DIM], _spread_rope_cols(wq[..., QK_NOPE_DIM:])], axis=-1)
    wq = wq.reshape(Q_LORA_RANK, MLA_HEADS * 2 * LANES).astype(BF16)
    wkv = l0_w_kv_up.reshape(KV_LORA_RANK, MLA_HEADS, QK_NOPE_DIM + V_HEAD_DIM)
    wkv = jnp.concatenate([wkv[..., :QK_NOPE_DIM].reshape(KV_LORA_RANK, -1),
                           wkv[..., QK_NOPE_DIM:].reshape(KV_LORA_RANK, -1)], axis=1).astype(BF16)
    cos0, sin0 = _rope_tables(s, QK_ROPE_DIM, spread=True)
    q, k, v, u = _l0_proj(x2d, w_in, row(l0_q_norm_g), wq, row(l0_kv_norm_g), wkv, cos0, sin0, s)
    hw = MLA_HEADS * 2 * LANES
    attn = _attention(q.reshape(b, s, hw), k.reshape(b, s, hw), v.reshape(b, s, hw),
                      MLA_HEADS, 1, 2 * LANES, 2 * LANES, V_HEAD_DIM)
    conv_w = jnp.concatenate([l0_conv_w, jnp.zeros((1, CONV_CHANNELS), F32)], axis=0)
    uc = _conv_module(u, conv_w, row(l0_conv_b), row(l0_conv_ln_g), row(l0_conv_ln_b), b, s)
    n_attn = MLA_HEADS * V_HEAD_DIM
    w_o = l0_w_o.astype(BF16)
    x2d, xt = _out_ln([attn.reshape(t, n_attn), uc], [w_o[:n_attn], w_o[n_attn:]], x2d,
                      row(l0_ln1_g), row(l0_ln1_b))
    x2d = _moe_ln(x2d, xt, l0_router_w, l0_router_b, l0_w_gu, l0_b_gu, l0_w_dn, l0_b_dn,
                  l0_ln2_g, l0_ln2_b, tri)

    lambda_init = 0.8 - 0.6 * math.exp(-0.3 * 1)
    cos1, sin1 = _rope_tables(s, DIFF_HEAD_DIM, spread=False)
    q, k, v = _l1_proj(x2d, l1_w_qkv.astype(BF16), cos1, sin1, s)
    lam_in = jnp.stack([l1_lambda_q1, l1_lambda_k1, l1_lambda_q2, l1_lambda_k2]).astype(F32)
    dv = 2 * DIFF_HEAD_DIM
    attn = _attention(q.reshape(b, s, d), k.reshape(b, s, d), v.reshape(b, s, d),
                      DIFF_HEADS, 2, DIFF_HEAD_DIM, dv, dv,
                      extra=(lam_in, row(l1_subln_g)), lambda_init=lambda_init)
    x2d, xt = _out_ln([attn.reshape(t, d)], [l1_w_o.astype(BF16)], x2d, row(l1_ln1_g), row(l1_ln1_b))
    x2d = _moe_ln(x2d, xt, l1_router_w, l1_router_b, l1_w_gu, l1_b_gu, l1_w_dn, l1_b_dn,
                  l1_ln2_g, l1_ln2_b, tri)
    return x2d.reshape(b, s, d)
```

```python
import functools
import math

import jax
import jax.numpy as jnp
from jax import lax
from jax.experimental import pallas as pl
from jax.experimental.pallas import tpu as pltpu

F32 = jnp.float32
BF16 = jnp.bfloat16

D_MODEL = 1024
DEPTH = 2
MLA_HEADS = 4
QK_NOPE_DIM = 128
QK_ROPE_DIM = 64
V_HEAD_DIM = 128
Q_LORA_RANK = 384
KV_LORA_RANK = 256
CONV_CHANNELS = D_MODEL - MLA_HEADS * V_HEAD_DIM
CONV_WIDTH = 31
DIFF_HEAD_DIM = 128
DIFF_HEADS = D_MODEL // (2 * DIFF_HEAD_DIM)
N_EXPERTS = 32
TOP_K = 4
D_EXPERT = D_MODEL
SWIGLU_LIMIT = 7.0
SWIGLU_ALPHA = 1.702
ROPE_THETA = 10000.0
DN_ALPHA = (2 * DEPTH) ** 0.25
LN_EPS = 1e-5
RMS_EPS = 1e-6
MASK_VALUE = -1e30

LANES = 128
ROW_TILE = 8
TOKEN_TILE = 512
ATTN_Q_TILE = 2048
ATTN_K_TILE = 1024
ATTN_CHAINS = 8
CONV_TILE = 512
CONV_HALO = 32
CONV_CHUNK = 64
EXPERT_ROWS = 512
CAST_CHUNK = 128
VMEM_LIMIT = 56 * 1024 * 1024


def _params(sem, vmem=VMEM_LIMIT):
    return pltpu.CompilerParams(dimension_semantics=sem, vmem_limit_bytes=vmem)


def _layer_norm(r, g, b):
    mu = jnp.mean(r, axis=-1, keepdims=True)
    d = r - mu
    var = jnp.mean(d * d, axis=-1, keepdims=True)
    return d * lax.rsqrt(var + LN_EPS) * g + b


def _rms_norm(x, g):
    return x * lax.rsqrt(jnp.mean(x * x, axis=-1, keepdims=True) + RMS_EPS) * g


def _rope(x, cos, sin):
    return x * cos + pltpu.roll(x, 64, 1) * sin


def _repeat_lanes(x, n):
    return x if n == 1 else jnp.concatenate([x] * n, axis=1)


def _load_rows(ref, n_rows):
    return jnp.concatenate([ref[pl.ds(j, n_rows, stride=ROW_TILE), :] for j in range(ROW_TILE)], axis=1)


def _store_rows(ref, val):
    n_rows = val.shape[0]
    for j in range(ROW_TILE):
        ref[pl.ds(j, n_rows, stride=ROW_TILE), :] = val[:, j * LANES:(j + 1) * LANES]


def _l0_proj_kernel(x_ref, win_ref, qg_ref, wq_ref, kvg_ref, wkv_ref, cos_ref, sin_ref,
                    q_ref, k_ref, v_ref, u_ref):
    tm = x_ref.shape[0]
    xb = x_ref[...].astype(BF16)
    proj = jnp.dot(xb, win_ref[...], preferred_element_type=F32)
    o1 = Q_LORA_RANK
    o2 = o1 + KV_LORA_RANK
    o3 = o2 + LANES
    o4 = o3 + CONV_CHANNELS
    u_ref[...] = proj[:, o3:o4] * jax.nn.sigmoid(proj[:, o4:])
    cos = cos_ref[...]
    sin = sin_ref[...]
    scale = (QK_NOPE_DIM + QK_ROPE_DIM) ** -0.5
    qn = _rms_norm(proj[:, :o1], qg_ref[...])
    qup = jnp.dot(qn.astype(BF16), wq_ref[...], preferred_element_type=F32)
    kvn = _rms_norm(proj[:, o1:o2], kvg_ref[...])
    kvup = jnp.dot(kvn.astype(BF16), wkv_ref[...], preferred_element_type=F32)
    k_rope = _rope(proj[:, o2:o3], cos, sin).astype(BF16)
    ones_col = jnp.where(lax.broadcasted_iota(jnp.int32, (tm, LANES), 1) == 0, 1.0, 0.0).astype(BF16)
    for h in range(MLA_HEADS):
        c = 2 * LANES * h
        q_ref[:, c:c + LANES] = (qup[:, c:c + LANES] * scale).astype(BF16)
        q_rope = _rope(qup[:, c + LANES:c + 2 * LANES], cos, sin)
        q_ref[:, c + LANES:c + 2 * LANES] = (q_rope * scale).astype(BF16)
        k_ref[:, c:c + LANES] = kvup[:, LANES * h:LANES * (h + 1)].astype(BF16)
        k_ref[:, c + LANES:c + 2 * LANES] = k_rope
        vh = MLA_HEADS * LANES + LANES * h
        v_ref[:, c:c + LANES] = kvup[:, vh:vh + LANES].astype(BF16)
        v_ref[:, c + LANES:c + 2 * LANES] = ones_col


def _l0_proj(x2d, w_in, qg, wq, kvg, wkv, cos, sin, seq):
    t = x2d.shape[0]
    tm = TOKEN_TILE
    n_pos = seq // tm
    full = lambda a: pl.BlockSpec(a.shape, lambda i: (0,) * a.ndim)
    row = lambda w: pl.BlockSpec((tm, w), lambda i: (i, 0))
    pos = pl.BlockSpec((tm, LANES), lambda i: (i % n_pos, 0))
    hw = MLA_HEADS * 2 * LANES
    wide = jax.ShapeDtypeStruct((t, hw), BF16)
    return pl.pallas_call(
        _l0_proj_kernel,
        grid=(t // tm,),
        in_specs=[row(D_MODEL), full(w_in), full(qg), full(wq), full(kvg), full(wkv), pos, pos],
        out_specs=[row(hw), row(hw), row(hw), row(CONV_CHANNELS)],
        out_shape=[wide, wide, wide, jax.ShapeDtypeStruct((t, CONV_CHANNELS), F32)],
        compiler_params=_params(("parallel",)),
        name="l0_proj",
    )(x2d, w_in, qg, wq, kvg, wkv, cos, sin)


def _l1_proj_kernel(x_ref, w_ref, cos_ref, sin_ref, q_ref, k_ref, v_ref):
    xb = x_ref[...].astype(BF16)
    cos = cos_ref[...]
    sin = sin_ref[...]
    scale = DIFF_HEAD_DIM ** -0.5
    qk_w = DIFF_HEADS * 2 * DIFF_HEAD_DIM
    q = jnp.dot(xb, w_ref[:, :qk_w], preferred_element_type=F32)
    for j in range(qk_w // LANES):
        c = j * LANES
        q_ref[:, c:c + LANES] = (_rope(q[:, c:c + LANES], cos, sin) * scale).astype(BF16)
    k = jnp.dot(xb, w_ref[:, qk_w:2 * qk_w], preferred_element_type=F32)
    for j in range(qk_w // LANES):
        c = j * LANES
        k_ref[:, c:c + LANES] = _rope(k[:, c:c + LANES], cos, sin).astype(BF16)
    v_ref[...] = jnp.dot(xb, w_ref[:, 2 * qk_w:], preferred_element_type=F32).astype(BF16)


def _l1_proj(x2d, w_qkv, cos, sin, seq):
    t = x2d.shape[0]
    tm = TOKEN_TILE
    n_pos = seq // tm
    row = lambda w: pl.BlockSpec((tm, w), lambda i: (i, 0))
    pos = pl.BlockSpec((tm, LANES), lambda i: (i % n_pos, 0))
    out = jax.ShapeDtypeStruct((t, D_MODEL), BF16)
    return pl.pallas_call(
        _l1_proj_kernel,
        grid=(t // tm,),
        in_specs=[row(D_MODEL), pl.BlockSpec(w_qkv.shape, lambda i: (0, 0)), pos, pos],
        out_specs=[row(D_MODEL)] * 3,
        out_shape=[out, out, out],
        compiler_params=_params(("parallel",)),
        name="l1_proj",
    )(x2d, w_qkv, cos, sin)


def _attn_kernel(*refs, n_maps, dk, tq, tk, n_chains, sum_col, lambda_init):
    if n_maps == 2:
        q_ref, k_ref, v_ref, lam_ref, g_ref, o_ref, m_sc, l_sc, acc_sc = refs
    else:
        q_ref, k_ref, v_ref, o_ref, m_sc, l_sc, acc_sc = refs
    qi = pl.program_id(2)
    rs = tq // n_chains
    dv = v_ref.shape[2]
    m_sc[...] = jnp.full(m_sc.shape, MASK_VALUE, F32)
    l_sc[...] = jnp.zeros(l_sc.shape, F32)
    acc_sc[...] = jnp.zeros(acc_sc.shape, F32)

    def chain(m, r, k, v, mask):
        rows = slice(r * rs, (r + 1) * rs)
        q = q_ref[0, rows, m * dk:(m + 1) * dk]
        s = lax.dot_general(q, k, (((1,), (1,)), ((), ())), preferred_element_type=F32)
        if mask is not None:
            s = jnp.where(mask, s, MASK_VALUE)
        m_prev = m_sc[m, rows, :]
        m_new = jnp.maximum(m_prev, jnp.max(s, axis=1, keepdims=True))
        p = jnp.exp(s - _repeat_lanes(m_new, s.shape[1] // LANES))
        a = jnp.exp(m_prev - m_new)
        if sum_col is None:
            l_sc[m, rows, :] = a * l_sc[m, rows, :] + jnp.sum(p, axis=1, keepdims=True)
        acc_sc[m, rows, :] = (_repeat_lanes(a, dv // LANES) * acc_sc[m, rows, :]
                              + jnp.dot(p.astype(BF16), v, preferred_element_type=F32))
        m_sc[m, rows, :] = m_new

    def off_diagonal(j, carry):
        start = pl.multiple_of(j * tk, tk)
        v = v_ref[0, pl.ds(start, tk), :]
        for m in range(n_maps):
            k = k_ref[0, pl.ds(start, tk), m * dk:(m + 1) * dk]
            for r in range(n_chains):
                chain(m, r, k, v, None)
        return carry

    lax.fori_loop(0, qi * (tq // tk), off_diagonal, 0)
    base = pl.multiple_of(qi * tq, tq)
    for r in range(n_chains):
        nk = (r + 1) * rs
        v = v_ref[0, pl.ds(base, nk), :]
        row = lax.broadcasted_iota(jnp.int32, (rs, nk), 0) + r * rs
        col = lax.broadcasted_iota(jnp.int32, (rs, nk), 1)
        mask = col <= row
        for m in range(n_maps):
            k = k_ref[0, pl.ds(base, nk), m * dk:(m + 1) * dk]
            chain(m, r, k, v, mask)

    if n_maps == 1:
        acc = acc_sc[0]
        o_ref[0] = (acc[:, :sum_col] / acc[:, sum_col:sum_col + 1]).astype(o_ref.dtype)
    else:
        lam_in = lam_ref[...]
        lam = (jnp.exp(jnp.sum(lam_in[0:1] * lam_in[1:2], axis=1, keepdims=True))
               - jnp.exp(jnp.sum(lam_in[2:3] * lam_in[3:4], axis=1, keepdims=True)) + lambda_init)
        a = acc_sc[0] / l_sc[0][:, 0:1] - lam * (acc_sc[1] / l_sc[1][:, 0:1])
        o_ref[0] = (_rms_norm(a, g_ref[...]) * (1.0 - lambda_init)).astype(o_ref.dtype)


def _attention(q, k, v, n_heads, n_maps, dk, dv_in, dv_out, extra=(), lambda_init=0.0):
    b, s, _ = q.shape
    tq, tk = ATTN_Q_TILE, ATTN_K_TILE
    qw = n_maps * dk
    sum_col = dv_out if dv_in > dv_out else None
    extra_specs = [pl.BlockSpec(e.shape, lambda bi, h, qi: (0, 0)) for e in extra]
    kern = functools.partial(_attn_kernel, n_maps=n_maps, dk=dk, tq=tq, tk=tk, n_chains=ATTN_CHAINS,
                             sum_col=sum_col, lambda_init=lambda_init)
    return pl.pallas_call(
        kern,
        grid=(b, n_heads, s // tq),
        in_specs=[pl.BlockSpec((1, tq, qw), lambda bi, h, qi: (bi, qi, h)),
                  pl.BlockSpec((1, s, qw), lambda bi, h, qi: (bi, 0, h)),
                  pl.BlockSpec((1, s, dv_in), lambda bi, h, qi: (bi, 0, h))] + extra_specs,
        out_specs=pl.BlockSpec((1, tq, dv_out), lambda bi, h, qi: (bi, qi, h)),
        out_shape=jax.ShapeDtypeStruct((b, s, n_heads * dv_out), BF16),
        scratch_shapes=[pltpu.VMEM((n_maps, tq, LANES), F32), pltpu.VMEM((n_maps, tq, LANES), F32),
                        pltpu.VMEM((n_maps, tq, dv_in), F32)],
        compiler_params=_params(("parallel", "parallel", "arbitrary")),
        name="attention_%dmap" % n_maps,
    )(q, k, v, *extra)


def _conv_kernel(u_ref, w_ref, cb_ref, g_ref, b_ref, o_ref, ext, win_sc):
    ts = u_ref.shape[0]
    si = pl.program_id(1)

    @pl.when(si == 0)
    def _():
        ext[0:CONV_HALO, :] = jnp.zeros((CONV_HALO, CONV_CHANNELS), F32)

    @pl.when(si > 0)
    def _():
        ext[0:CONV_HALO, :] = ext[ts:ts + CONV_HALO, :]

    ext[CONV_HALO:CONV_HALO + ts, :] = u_ref[...]
    first = CONV_HALO - (CONV_WIDTH - 1)
    for c in range(ts // CONV_CHUNK):
        r0 = c * CONV_CHUNK
        acc = jnp.zeros((CONV_CHUNK, CONV_CHANNELS), F32)
        for shift in range(ROW_TILE):
            offs = [first + j - shift for j in range(CONV_WIDTH) if (first + j) % ROW_TILE == shift]
            n_win = max(offs) + CONV_CHUNK
            win = win_sc.at[(c * ROW_TILE + shift) % 2]
            win[0:n_win, :] = ext[r0 + shift:r0 + shift + n_win, :]
            for off in offs:
                j = off + shift - first
                acc = acc + w_ref[j:j + 1, :] * win[off:off + CONV_CHUNK, :]
        y = _layer_norm(acc + cb_ref[...], g_ref[...], b_ref[...])
        o_ref[r0:r0 + CONV_CHUNK, :] = (y * jax.nn.sigmoid(y)).astype(o_ref.dtype)


def _conv_module(u2d, conv_w, conv_b, ln_g, ln_b, batch, seq):
    ts = CONV_TILE
    n_s = seq // ts
    vec = pl.BlockSpec((1, CONV_CHANNELS), lambda bi, si: (0, 0))
    return pl.pallas_call(
        _conv_kernel,
        grid=(batch, n_s),
        in_specs=[pl.BlockSpec((ts, CONV_CHANNELS), lambda bi, si: (bi * n_s + si, 0)),
                  pl.BlockSpec(conv_w.shape, lambda bi, si: (0, 0)), vec, vec, vec],
        out_specs=pl.BlockSpec((ts, CONV_CHANNELS), lambda bi, si: (bi * n_s + si, 0)),
        out_shape=jax.ShapeDtypeStruct((batch * seq, CONV_CHANNELS), BF16),
        scratch_shapes=[pltpu.VMEM((ts + CONV_HALO, CONV_CHANNELS), F32),
                        pltpu.VMEM((2, CONV_CHUNK + CONV_HALO, CONV_CHANNELS), F32)],
        compiler_params=_params(("arbitrary", "arbitrary")),
        name="conv_module",
    )(u2d, conv_w, conv_b, ln_g, ln_b)


def _out_ln_kernel(*refs, n_in):
    a_refs = refs[:n_in]
    w_refs = refs[n_in:2 * n_in]
    x_ref, g_ref, b_ref, o_ref, ot_ref = refs[2 * n_in:]
    mix = jnp.dot(a_refs[0][...], w_refs[0][...], preferred_element_type=F32)
    for a_ref, w_ref in zip(a_refs[1:], w_refs[1:]):
        mix = mix + jnp.dot(a_ref[...], w_ref[...], preferred_element_type=F32)
    y = _layer_norm(DN_ALPHA * x_ref[...] + mix, g_ref[...], b_ref[...])
    o_ref[...] = y
    _store_rows(ot_ref, y)


def _out_ln(acts, weights, x2d, g, b):
    t = x2d.shape[0]
    tm = TOKEN_TILE
    vec = pl.BlockSpec((1, D_MODEL), lambda i: (0, 0))
    return pl.pallas_call(
        functools.partial(_out_ln_kernel, n_in=len(acts)),
        grid=(t // tm,),
        in_specs=([pl.BlockSpec((tm, a.shape[1]), lambda i: (i, 0)) for a in acts]
                  + [pl.BlockSpec(w.shape, lambda i: (0, 0)) for w in weights]
                  + [pl.BlockSpec((tm, D_MODEL), lambda i: (i, 0)), vec, vec]),
        out_specs=[pl.BlockSpec((tm, D_MODEL), lambda i: (i, 0)),
                   pl.BlockSpec((tm * ROW_TILE, LANES), lambda i: (i, 0))],
        out_shape=[jax.ShapeDtypeStruct((t, D_MODEL), F32),
                   jax.ShapeDtypeStruct((t * ROW_TILE, LANES), F32)],
        compiler_params=_params(("parallel",)),
        name="out_proj_ln",
    )(*acts, *weights, x2d, g, b)


def _router_kernel(x_ref, rw_ref, rb_ref, tri_ref, low_ref, gate_ref, lrow_ref, before_ref, tcnt_ref,
                   cnt_ref):
    tm = x_ref.shape[0]
    slot_base = (pl.program_id(0) % 2) * (TOP_K * tm * ROW_TILE)

    @pl.when(pl.program_id(0) == 0)
    def _():
        cnt_ref[...] = jnp.zeros(cnt_ref.shape, F32)

    x = x_ref[...]
    x_top = pltpu.bitcast(pltpu.bitcast(x, jnp.uint32) & jnp.uint32(0xFFFF0000), F32)
    x_hi = x_top.astype(BF16)
    x_lo = (x - x_top).astype(BF16)
    nt = (((1,), (1,)), ((), ()))
    by_hi = lax.dot_general(rw_ref[...], x_hi, nt, preferred_element_type=F32)
    by_lo = lax.dot_general(rw_ref[...], x_lo, nt, preferred_element_type=F32)
    logits = ((by_hi[:N_EXPERTS] + by_hi[N_EXPERTS:]) + (by_lo[:N_EXPERTS] + by_lo[N_EXPERTS:])
              + rb_ref[...])
    e_iota = lax.broadcasted_iota(jnp.int32, (N_EXPERTS, tm), 0)
    vals, sels = [], []
    work = logits
    for k in range(TOP_K):
        top = jnp.max(work, axis=0, keepdims=True)
        idx = jnp.min(jnp.where(work == top, e_iota, N_EXPERTS), axis=0, keepdims=True)
        sel = e_iota == idx
        vals.append(top)
        sels.append(sel)
        work = jnp.where(sel, -jnp.inf, work)
    exps = [jnp.exp(v - vals[0]) for v in vals]
    denom = exps[0] + exps[1] + exps[2] + exps[3]
    for k in range(TOP_K):
        gate_ref[0, k:k + 1, :] = exps[k] / denom
    chosen = jnp.where(sels[0] | sels[1] | sels[2] | sels[3], 1.0, 0.0)
    earlier = jnp.dot(chosen.astype(BF16), tri_ref[...], preferred_element_type=F32)
    before = cnt_ref[...]
    tile_cnt = jnp.sum(chosen, axis=1, keepdims=True)
    cnt16 = jnp.floor(tile_cnt * (1.0 / 16.0))
    cnt_r = tile_cnt - 16.0 * cnt16
    prefix = lambda c: jnp.dot(low_ref[...], jnp.broadcast_to(c, before.shape).astype(BF16),
                               preferred_element_type=F32)
    local = 16.0 * prefix(cnt16) + prefix(cnt_r)
    lbase = earlier + local[:, 0:1]
    for k in range(TOP_K):
        lrow = jnp.sum(jnp.where(sels[k], lbase, 0.0), axis=0, keepdims=True)
        lrow_ref[0, k:k + 1, :] = lrow.astype(jnp.int32) * ROW_TILE + slot_base
    before_ref[0] = before
    tcnt_ref[0] = jnp.broadcast_to(tile_cnt, before.shape)
    cnt_ref[...] = before + tile_cnt


def _router(x2d, rw_t, rb, tri):
    t = x2d.shape[0]
    tm = TOKEN_TILE
    n = t // tm
    kt = pl.BlockSpec((1, TOP_K, tm), lambda i: (i, 0, 0))
    per_tile = pl.BlockSpec((1, N_EXPERTS, LANES), lambda i: (i, 0, 0))
    low = (jnp.arange(N_EXPERTS)[:, None] > jnp.arange(N_EXPERTS)[None, :]).astype(BF16)
    ints = jax.ShapeDtypeStruct((n, TOP_K, tm), jnp.int32)
    tiles = jax.ShapeDtypeStruct((n, N_EXPERTS, LANES), F32)
    return pl.pallas_call(
        _router_kernel,
        grid=(n,),
        in_specs=[pl.BlockSpec((tm, D_MODEL), lambda i: (i, 0)),
                  pl.BlockSpec(rw_t.shape, lambda i: (0, 0)),
                  pl.BlockSpec(rb.shape, lambda i: (0, 0)),
                  pl.BlockSpec(tri.shape, lambda i: (0, 0)),
                  pl.BlockSpec(low.shape, lambda i: (0, 0))],
        out_specs=[kt, kt, per_tile, per_tile, pl.BlockSpec((N_EXPERTS, LANES), lambda i: (0, 0))],
        out_shape=[jax.ShapeDtypeStruct((n, TOP_K, tm), F32), ints, tiles, tiles,
                   jax.ShapeDtypeStruct((N_EXPERTS, LANES), F32)],
        compiler_params=_params(("arbitrary",)),
        name="router",
    )(x2d, rw_t, rb, tri, low)


def _dispatch_kernel(pend_ref, padded_ref, rdst_ref, rsrc_ref, rlen_ref, lrow_ref, xt_ref, xs_ref,
                     zeros, stage, zsem, sems):
    td = lrow_ref.shape[0] // TOP_K
    zrows = zeros.shape[0]
    slot_rows = stage.shape[0] // 2
    i = pl.program_id(0)
    slot = i % 2

    @pl.when(pl.program_id(0) == 0)
    def _():
        zeros[...] = jnp.zeros(zeros.shape, F32)

        def zero_block(start):
            return pltpu.make_async_copy(zeros, xs_ref.at[pl.ds(pl.multiple_of(start, zrows), zrows)], zsem)

        def zero_tail(fn):
            def body(e, carry):
                @pl.when(padded_ref[e] > 0)
                def _():
                    fn(zero_block(pend_ref[e] * ROW_TILE - zrows))
                return carry

            lax.fori_loop(0, N_EXPERTS, body, 0)

        def zero_unused(fn):
            def body(blk, carry):
                fn(zero_block(blk * zrows))
                return carry

            lax.fori_loop(pend_ref[N_EXPERTS - 1] * ROW_TILE // zrows, xs_ref.shape[0] // zrows, body, 0)

        zero_tail(lambda cp: cp.start())
        zero_unused(lambda cp: cp.start())
        zero_tail(lambda cp: cp.wait())
        zero_unused(lambda cp: cp.wait())

    def start_runs(tile, s):
        def per_expert(e, carry):
            n_rows = rlen_ref[tile * N_EXPERTS + e]
            src = rsrc_ref[tile * N_EXPERTS + e]
            dst = rdst_ref[tile * N_EXPERTS + e]
            for bit in reversed(range(td.bit_length())):
                size = (1 << bit) * ROW_TILE
                has = (n_rows & (1 << bit)) != 0

                @pl.when(has)
                def _(src=src, dst=dst, size=size):
                    pltpu.make_async_copy(
                        stage.at[pl.ds(pl.multiple_of(s * slot_rows + src, ROW_TILE), size)],
                        xs_ref.at[pl.ds(pl.multiple_of(dst, ROW_TILE), size)], sems.at[s]).start()

                step = jnp.where(has, size, 0)
                src = src + step
                dst = dst + step
            return carry

        lax.fori_loop(0, N_EXPERTS, per_expert, 0)

    def wait_tile(s):
        pltpu.make_async_copy(stage.at[pl.ds(pl.multiple_of(s * slot_rows, ROW_TILE), slot_rows)],
                              xs_ref.at[pl.ds(0, slot_rows)], sems.at[s]).wait()

    def compact(t8, carry):
        for u in range(ROW_TILE):
            row = xt_ref[pl.ds(pl.multiple_of((t8 * ROW_TILE + u) * ROW_TILE, ROW_TILE), ROW_TILE), :]
            for k in range(TOP_K):
                dst = pl.multiple_of(lrow_ref[k * td + t8 * ROW_TILE + u], ROW_TILE)
                stage[pl.ds(dst, ROW_TILE), :] = row
        return carry

    lax.fori_loop(0, td // ROW_TILE, compact, 0)
    start_runs(i, slot)

    @pl.when(i > 0)
    def _():
        wait_tile(1 - slot)

    @pl.when(i == pl.num_programs(0) - 1)
    def _():
        wait_tile(slot)


def _dispatch(pend, padded, rdst, rsrc, rlen, lrow8, xt, n_rows):
    t = lrow8.shape[0] // TOP_K
    td = TOKEN_TILE
    smem = lambda i, *_: (i,)
    return pl.pallas_call(
        _dispatch_kernel,
        grid_spec=pltpu.PrefetchScalarGridSpec(
            num_scalar_prefetch=5,
            grid=(t // td,),
            in_specs=[pl.BlockSpec((TOP_K * td,), smem, memory_space=pltpu.SMEM),
                      pl.BlockSpec((td * ROW_TILE, LANES), lambda i, *_: (i, 0))],
            out_specs=pl.BlockSpec(memory_space=pl.ANY),
            scratch_shapes=[pltpu.VMEM((EXPERT_ROWS * ROW_TILE, LANES), F32),
                            pltpu.VMEM((2 * TOP_K * td * ROW_TILE, LANES), F32),
                            pltpu.SemaphoreType.DMA(()), pltpu.SemaphoreType.DMA((2,))]),
        out_shape=jax.ShapeDtypeStruct((n_rows * ROW_TILE, LANES), F32),
        compiler_params=_params(("arbitrary",)),
        name="moe_dispatch",
    )(pend, padded, rdst, rsrc, rlen, lrow8, xt)


def _expert_kernel(blk_e_ref, grp_ref, nxt_ref, valid_ref, n_used_ref, xs_ref, wgu_hbm, bgu_ref, wdn_hbm,
                   bdn_ref, ys_ref, wgu_f32, wdn_f32, wgu_bf, wdn_bf, sems):
    i = pl.program_id(0)
    rb = xs_ref.shape[0] // ROW_TILE
    active = i < n_used_ref[0]
    new_expert = (i == 0) | (blk_e_ref[i] != blk_e_ref[jnp.maximum(i - 1, 0)])

    def weight_copies(e, slot):
        return (pltpu.make_async_copy(wgu_hbm.at[e], wgu_f32.at[slot], sems.at[0, slot]),
                pltpu.make_async_copy(wdn_hbm.at[e], wdn_f32.at[slot], sems.at[1, slot]))

    @pl.when(active & new_expert)
    def _():
        slot = grp_ref[i] % 2
        e = blk_e_ref[i]
        nxt = nxt_ref[i]

        @pl.when(i == 0)
        def _():
            for cp in weight_copies(e, slot):
                cp.start()

        @pl.when(nxt >= 0)
        def _():
            for cp in weight_copies(nxt, 1 - slot):
                cp.start()

        for cp in weight_copies(e, slot):
            cp.wait()

        def cast(c, carry):
            rows = pl.ds(pl.multiple_of(c * CAST_CHUNK, CAST_CHUNK), CAST_CHUNK)
            wgu_bf[rows, :] = wgu_f32[slot, rows, :].astype(BF16)
            wdn_bf[rows, :] = wdn_f32[slot, rows, :].astype(BF16)
            return carry

        lax.fori_loop(0, D_MODEL // CAST_CHUNK, cast, 0)

    def swiglu_rows(n_rows):
        xb = _load_rows(xs_ref, n_rows).astype(BF16)
        h = jnp.dot(xb, wgu_bf[...], preferred_element_type=F32) + bgu_ref[0]
        gate = jnp.minimum(h[:, :D_EXPERT], SWIGLU_LIMIT)
        up = jnp.clip(h[:, D_EXPERT:], -SWIGLU_LIMIT, SWIGLU_LIMIT)
        act = (up + 1.0) * gate * jax.nn.sigmoid(SWIGLU_ALPHA * gate)
        y = jnp.dot(act.astype(BF16), wdn_bf[...], preferred_element_type=F32) + bdn_ref[0]
        _store_rows(ys_ref, y)

    half_only = valid_ref[i] <= rb // 2

    @pl.when(active & jnp.logical_not(half_only))
    def _():
        swiglu_rows(rb)

    @pl.when(active & half_only)
    def _():
        swiglu_rows(rb // 2)
        ys_ref[rb // 2 * ROW_TILE:, :] = jnp.zeros((rb // 2 * ROW_TILE, LANES), F32)

    @pl.when(jnp.logical_not(active))
    def _():
        ys_ref[...] = jnp.zeros(ys_ref.shape, F32)


def _experts(blk_e, valid, n_used, xs, w_gu, b_gu, w_dn, b_dn):
    rb = EXPERT_ROWS
    n_blk = xs.shape[0] // (rb * ROW_TILE)
    ids = jnp.arange(n_blk, dtype=jnp.int32)
    change = jnp.concatenate([jnp.ones((1,), bool), blk_e[1:] != blk_e[:-1]])
    grp = (jnp.cumsum(change.astype(jnp.int32)) - 1).astype(jnp.int32)
    later = jnp.where(change, ids, n_blk)
    nxt_pos = jnp.concatenate([lax.cummin(later[::-1])[::-1][1:], jnp.full((1,), n_blk, jnp.int32)])
    nxt = jnp.where(nxt_pos < n_blk, blk_e[jnp.minimum(nxt_pos, n_blk - 1)], -1).astype(jnp.int32)
    rows = pl.BlockSpec((rb * ROW_TILE, LANES), lambda i, be, gr, nx, va, nu: (jnp.minimum(i, nu[0] - 1), 0))
    rows_out = pl.BlockSpec((rb * ROW_TILE, LANES), lambda i, be, gr, nx, va, nu: (i, 0))
    per_e = lambda a: pl.BlockSpec((1,) + a.shape[1:], lambda i, be, gr, nx, va, nu: (be[i], 0, 0))
    hbm = pl.BlockSpec(memory_space=pl.ANY)
    return pl.pallas_call(
        _expert_kernel,
        grid_spec=pltpu.PrefetchScalarGridSpec(
            num_scalar_prefetch=5,
            grid=(n_blk,),
            in_specs=[rows, hbm, per_e(b_gu), hbm, per_e(b_dn)],
            out_specs=rows_out,
            scratch_shapes=[pltpu.VMEM((2,) + w_gu.shape[1:], F32), pltpu.VMEM((2,) + w_dn.shape[1:], F32),
                            pltpu.VMEM(w_gu.shape[1:], BF16), pltpu.VMEM(w_dn.shape[1:], BF16),
                            pltpu.SemaphoreType.DMA((2, 2))]),
        out_shape=jax.ShapeDtypeStruct(xs.shape, F32),
        compiler_params=_params(("arbitrary",)),
        name="moe_experts",
    )(blk_e, grp, nxt, valid, n_used, xs, w_gu, b_gu, w_dn, b_dn)


def _combine_kernel(rdst_ref, rsrc_ref, rlen_ref, lrow_ref, gate_ref, x_ref, ys_ref, g_ref, b_ref, o_ref,
                    stage, moe_sc, sems):
    td = x_ref.shape[0]
    slot_rows = stage.shape[0] // 2
    i = pl.program_id(0)
    slot = i % 2

    def start_runs(tile, s):
        def per_expert(e, carry):
            n_rows = rlen_ref[tile * N_EXPERTS + e]
            src = rdst_ref[tile * N_EXPERTS + e]
            dst = rsrc_ref[tile * N_EXPERTS + e]
            for bit in reversed(range(td.bit_length())):
                size = (1 << bit) * ROW_TILE
                has = (n_rows & (1 << bit)) != 0

                @pl.when(has)
                def _(src=src, dst=dst, size=size):
                    pltpu.make_async_copy(
                        ys_ref.at[pl.ds(pl.multiple_of(src, ROW_TILE), size)],
                        stage.at[pl.ds(pl.multiple_of(s * slot_rows + dst, ROW_TILE), size)],
                        sems.at[s]).start()

                step = jnp.where(has, size, 0)
                src = src + step
                dst = dst + step
            return carry

        lax.fori_loop(0, N_EXPERTS, per_expert, 0)

    @pl.when(i == 0)
    def _():
        start_runs(0, 0)

    @pl.when(i + 1 < pl.num_programs(0))
    def _():
        start_runs(i + 1, 1 - slot)

    pltpu.make_async_copy(ys_ref.at[pl.ds(0, slot_rows)],
                          stage.at[pl.ds(pl.multiple_of(slot * slot_rows, ROW_TILE), slot_rows)],
                          sems.at[slot]).wait()

    def reduce_rows(t8, carry):
        for u in range(ROW_TILE):
            tok = t8 * ROW_TILE + u
            acc = None
            for k in range(TOP_K):
                row = stage[pl.ds(pl.multiple_of(lrow_ref[k * td + tok], ROW_TILE), ROW_TILE), :]
                term = gate_ref[k * td + tok] * row
                acc = term if acc is None else acc + term
            moe_sc[pl.ds(pl.multiple_of((t8 * ROW_TILE + u) * ROW_TILE, ROW_TILE), ROW_TILE), :] = acc
        return carry

    lax.fori_loop(0, td // ROW_TILE, reduce_rows, 0)
    moe = _load_rows(moe_sc, td)
    o_ref[...] = _layer_norm(DN_ALPHA * x_ref[...] + moe, g_ref[...], b_ref[...])


def _combine(rdst, rsrc, rlen, lrow8, gates_flat, x2d, ys, g, b):
    t = x2d.shape[0]
    td = TOKEN_TILE
    vec = pl.BlockSpec((1, D_MODEL), lambda i, *_: (0, 0))
    smem = pl.BlockSpec((TOP_K * td,), lambda i, *_: (i,), memory_space=pltpu.SMEM)
    return pl.pallas_call(
        _combine_kernel,
        grid_spec=pltpu.PrefetchScalarGridSpec(
            num_scalar_prefetch=3,
            grid=(t // td,),
            in_specs=[smem, smem, pl.BlockSpec((td, D_MODEL), lambda i, *_: (i, 0)),
                      pl.BlockSpec(memory_space=pl.ANY), vec, vec],
            out_specs=pl.BlockSpec((td, D_MODEL), lambda i, *_: (i, 0)),
            scratch_shapes=[pltpu.VMEM((2 * TOP_K * td * ROW_TILE, LANES), F32),
                            pltpu.VMEM((td * ROW_TILE, LANES), F32), pltpu.SemaphoreType.DMA((2,))]),
        out_shape=jax.ShapeDtypeStruct((t, D_MODEL), F32),
        compiler_params=_params(("arbitrary",)),
        name="moe_combine_ln",
    )(rdst, rsrc, rlen, lrow8, gates_flat, x2d, ys, g, b)


def _moe_ln(x2d, xt, router_w, router_b, w_gu, b_gu, w_dn, b_dn, ln_g, ln_b, tri):
    t = x2d.shape[0]
    rb = EXPERT_ROWS
    rw_t = router_w.T.astype(F32)
    rw_top = lax.bitcast_convert_type(
        lax.bitcast_convert_type(rw_t, jnp.uint32) & jnp.uint32(0xFFFF0000), F32)
    rw_split = jnp.concatenate([rw_top.astype(BF16), (rw_t - rw_top).astype(BF16)], axis=0)
    gates, lrow, before, tile_cnt, cnt = _router(x2d, rw_split, router_b.reshape(N_EXPERTS, 1), tri)
    counts = cnt[:, 0].astype(jnp.int32)
    padded = ((counts + rb - 1) // rb) * rb
    pend = jnp.cumsum(padded).astype(jnp.int32)
    pstart = pend - padded
    n_blk = (t * TOP_K) // rb + N_EXPERTS
    n_used = pend[-1] // rb
    blk = jnp.minimum(jnp.arange(n_blk, dtype=jnp.int32), n_used - 1)
    blk_e = jnp.sum((blk[:, None] * rb >= pend[None, :]).astype(jnp.int32), axis=1)
    blk_e = jnp.minimum(blk_e, N_EXPERTS - 1).astype(jnp.int32)
    valid = jnp.clip((pstart + counts)[blk_e] - blk * rb, 0, rb).astype(jnp.int32)
    lrow8 = lrow.reshape(-1)
    run_len = tile_cnt[:, :, 0].astype(jnp.int32)
    run_src = (jnp.cumsum(run_len, axis=1) - run_len) * ROW_TILE
    run_dst = (pstart[None, :] + before[:, :, 0].astype(jnp.int32)) * ROW_TILE
    flat = lambda a: a.reshape(-1).astype(jnp.int32)

    run_dst, run_src, run_len = flat(run_dst), flat(run_src), flat(run_len)

    xs = _dispatch(pend, padded.astype(jnp.int32), run_dst, run_src, run_len, lrow8, xt, n_blk * rb)
    ys = _experts(blk_e, valid, n_used.reshape(1).astype(jnp.int32), xs,
                  w_gu, b_gu.reshape(N_EXPERTS, 1, -1), w_dn, b_dn.reshape(N_EXPERTS, 1, -1))
    return _combine(run_dst, run_src, run_len, lrow8, gates.reshape(-1), x2d, ys,
                    ln_g.reshape(1, -1), ln_b.reshape(1, -1))


def _spread_rope_cols(w):
    half = QK_ROPE_DIM // 2
    z = jnp.zeros(w.shape[:-1] + (LANES // 2 - half,), w.dtype)
    return jnp.concatenate([w[..., :half], z, w[..., half:], z], axis=-1)


def _rope_tables(seq, dim, spread):
    inv_freq = 1.0 / (ROPE_THETA ** (jnp.arange(0, dim, 2, dtype=F32) / dim))
    ang = jnp.arange(seq, dtype=F32)[:, None] * inv_freq[None, :]
    cos, sin = jnp.cos(ang), jnp.sin(ang)
    if spread:
        z = jnp.zeros((seq, LANES // 2 - dim // 2), F32)
        return (jnp.concatenate([cos, z, cos, z], axis=1), jnp.concatenate([-sin, z, sin, z], axis=1))
    return jnp.concatenate([cos, cos], axis=1), jnp.concatenate([-sin, sin], axis=1)


def kernel(x, l0_w_in, l0_q_norm_g, l0_w_q_up, l0_kv_norm_g, l0_w_kv_up, l0_conv_w, l0_conv_b, l0_conv_ln_g, l0_conv_ln_b, l0_w_o, l0_ln1_g, l0_ln1_b, l0_router_w, l0_router_b, l0_w_gu, l0_b_gu, l0_w_dn, l0_b_dn, l0_ln2_g, l0_ln2_b, l1_w_qkv, l1_lambda_q1, l1_lambda_k1, l1_lambda_q2, l1_lambda_k2, l1_subln_g, l1_w_o, l1_ln1_g, l1_ln1_b, l1_router_w, l1_router_b, l1_w_gu, l1_b_gu, l1_w_dn, l1_b_dn, l1_ln2_g, l1_ln2_b):
    b, s, d = x.shape
    t = b * s
    x2d = x.reshape(t, d)
    row = lambda a: a.reshape(1, -1)
    tri = (jnp.arange(TOKEN_TILE)[:, None] < jnp.arange(TOKEN_TILE)[None, :]).astype(BF16)

    o1 = Q_LORA_RANK
    o2 = o1 + KV_LORA_RANK
    o3 = o2 + QK_ROPE_DIM
    w_in = jnp.concatenate([l0_w_in[:, :o2], _spread_rope_cols(l0_w_in[:, o2:o3]), l0_w_in[:, o3:]],
                           axis=1).astype(BF16)
    wq = l0_w_q_up.reshape(Q_LORA_RANK, MLA_HEADS, QK_NOPE_DIM + QK_ROPE_DIM)
    wq = jnp.concatenate([wq[..., :QK_NOPE_DIM], _spread_rope_cols(wq[..., QK_NOPE_DIM:])], axis=-1)
    wq = wq.reshape(Q_LORA_RANK, MLA_HEADS * 2 * LANES).astype(BF16)
    wkv = l0_w_kv_up.reshape(KV_LORA_RANK, MLA_HEADS, QK_NOPE_DIM + V_HEAD_DIM)
    wkv = jnp.concatenate([wkv[..., :QK_NOPE_DIM].reshape(KV_LORA_RANK, -1),
                           wkv[..., QK_NOPE_DIM:].reshape(KV_LORA_RANK, -1)], axis=1).astype(BF16)
    cos0, sin0 = _rope_tables(s, QK_ROPE_DIM, spread=True)
    q, k, v, u = _l0_proj(x2d, w_in, row(l0_q_norm_g), wq, row(l0_kv_norm_g), wkv, cos0, sin0, s)
    hw = MLA_HEADS * 2 * LANES
    attn = _attention(q.reshape(b, s, hw), k.reshape(b, s, hw), v.reshape(b, s, hw),
                      MLA_HEADS, 1, 2 * LANES, 2 * LANES, V_HEAD_DIM)
    conv_w = jnp.concatenate([l0_conv_w, jnp.zeros((1, CONV_CHANNELS), F32)], axis=0)
    uc = _conv_module(u, conv_w, row(l0_conv_b), row(l0_conv_ln_g), row(l0_conv_ln_b), b, s)
    n_attn = MLA_HEADS * V_HEAD_DIM
    w_o = l0_w_o.astype(BF16)
    x2d, xt = _out_ln([attn.reshape(t, n_attn), uc], [w_o[:n_attn], w_o[n_attn:]], x2d,
                      row(l0_ln1_g), row(l0_ln1_b))
    x2d = _moe_ln(x2d, xt, l0_router_w, l0_router_b, l0_w_gu, l0_b_gu, l0_w_dn, l0_b_dn,
                  l0_ln2_g, l0_ln2_b, tri)

    lambda_init = 0.8 - 0.6 * math.exp(-0.3 * 1)
    cos1, sin1 = _rope_tables(s, DIFF_HEAD_DIM, spread=False)
    q, k, v = _l1_proj(x2d, l1_w_qkv.astype(BF16), cos1, sin1, s)
    lam_in = jnp.stack([l1_lambda_q1, l1_lambda_k1, l1_lambda_q2, l1_lambda_k2]).astype(F32)
    dv = 2 * DIFF_HEAD_DIM
    attn = _attention(q.reshape(b, s, d), k.reshape(b, s, d), v.reshape(b, s, d),
                      DIFF_HEADS, 2, DIFF_HEAD_DIM, dv, dv,
                      extra=(lam_in, row(l1_subln_g)), lambda_init=lambda_init)
    x2d, xt = _out_ln([attn.reshape(t, d)], [l1_w_o.astype(BF16)], x2d, row(l1_ln1_g), row(l1_ln1_b))
    x2d = _moe_ln(x2d, xt, l1_router_w, l1_router_b, l1_w_gu, l1_b_gu, l1_w_dn, l1_b_dn,
                  l1_ln2_g, l1_ln2_b, tri)
    return x2d.reshape(b, s, d)
```

```python
import functools
import math

import jax
import jax.numpy as jnp
from jax import lax
from jax.experimental import pallas as pl
from jax.experimental.pallas import tpu as pltpu

F32 = jnp.float32
BF16 = jnp.bfloat16

D_MODEL = 1024
DEPTH = 2
MLA_HEADS = 4
QK_NOPE_DIM = 128
QK_ROPE_DIM = 64
V_HEAD_DIM = 128
Q_LORA_RANK = 384
KV_LORA_RANK = 256
CONV_CHANNELS = D_MODEL - MLA_HEADS * V_HEAD_DIM
CONV_WIDTH = 31
DIFF_HEAD_DIM = 128
DIFF_HEADS = D_MODEL // (2 * DIFF_HEAD_DIM)
N_EXPERTS = 32
TOP_K = 4
D_EXPERT = D_MODEL
SWIGLU_LIMIT = 7.0
SWIGLU_ALPHA = 1.702
ROPE_THETA = 10000.0
DN_ALPHA = (2 * DEPTH) ** 0.25
LN_EPS = 1e-5
RMS_EPS = 1e-6
MASK_VALUE = -1e30

LANES = 128
ROW_TILE = 8
TOKEN_TILE = 512
ATTN_Q_TILE = 2048
ATTN_K_TILE = 1024
ATTN_CHAINS = 8
CONV_TILE = 512
CONV_HALO = 32
CONV_CHUNK = 64
EXPERT_ROWS = 512
CAST_CHUNK = 128
VMEM_LIMIT = 56 * 1024 * 1024


def _params(sem, vmem=VMEM_LIMIT):
    return pltpu.CompilerParams(dimension_semantics=sem, vmem_limit_bytes=vmem)


def _layer_norm(r, g, b):
    mu = jnp.mean(r, axis=-1, keepdims=True)
    d = r - mu
    var = jnp.mean(d * d, axis=-1, keepdims=True)
    return d * lax.rsqrt(var + LN_EPS) * g + b


def _rms_norm(x, g):
    return x * lax.rsqrt(jnp.mean(x * x, axis=-1, keepdims=True) + RMS_EPS) * g


def _rope(x, cos, sin):
    return x * cos + pltpu.roll(x, 64, 1) * sin


def _repeat_lanes(x, n):
    return x if n == 1 else jnp.concatenate([x] * n, axis=1)


def _load_rows(ref, n_rows):
    return jnp.concatenate([ref[pl.ds(j, n_rows, stride=ROW_TILE), :] for j in range(ROW_TILE)], axis=1)


def _store_rows(ref, val):
    n_rows = val.shape[0]
    for j in range(ROW_TILE):
        ref[pl.ds(j, n_rows, stride=ROW_TILE), :] = val[:, j * LANES:(j + 1) * LANES]


def _l0_proj_kernel(x_ref, win_ref, qg_ref, wq_ref, kvg_ref, wkv_ref, cos_ref, sin_ref,
                    q_ref, k_ref, v_ref, u_ref):
    tm = x_ref.shape[0]
    xb = x_ref[...].astype(BF16)
    proj = jnp.dot(xb, win_ref[...], preferred_element_type=F32)
    o1 = Q_LORA_RANK
    o2 = o1 + KV_LORA_RANK
    o3 = o2 + LANES
    o4 = o3 + CONV_CHANNELS
    u_ref[...] = proj[:, o3:o4] * jax.nn.sigmoid(proj[:, o4:])
    cos = cos_ref[...]
    sin = sin_ref[...]
    scale = (QK_NOPE_DIM + QK_ROPE_DIM) ** -0.5
    qn = _rms_norm(proj[:, :o1], qg_ref[...])
    qup = jnp.dot(qn.astype(BF16), wq_ref[...], preferred_element_type=F32)
    kvn = _rms_norm(proj[:, o1:o2], kvg_ref[...])
    kvup = jnp.dot(kvn.astype(BF16), wkv_ref[...], preferred_element_type=F32)
    k_rope = _rope(proj[:, o2:o3], cos, sin).astype(BF16)
    ones_col = jnp.where(lax.broadcasted_iota(jnp.int32, (tm, LANES), 1) == 0, 1.0, 0.0).astype(BF16)
    for h in range(MLA_HEADS):
        c = 2 * LANES * h
        q_ref[:, c:c + LANES] = (qup[:, c:c + LANES] * scale).astype(BF16)
        q_rope = _rope(qup[:, c + LANES:c + 2 * LANES], cos, sin)
        q_ref[:, c + LANES:c + 2 * LANES] = (q_rope * scale).astype(BF16)
        k_ref[:, c:c + LANES] = kvup[:, LANES * h:LANES * (h + 1)].astype(BF16)
        k_ref[:, c + LANES:c + 2 * LANES] = k_rope
        vh = MLA_HEADS * LANES + LANES * h
        v_ref[:, c:c + LANES] = kvup[:, vh:vh + LANES].astype(BF16)
        v_ref[:, c + LANES:c + 2 * LANES] = ones_col


def _l0_proj(x2d, w_in, qg, wq, kvg, wkv, cos, sin, seq):
    t = x2d.shape[0]
    tm = TOKEN_TILE
    n_pos = seq // tm
    full = lambda a: pl.BlockSpec(a.shape, lambda i: (0,) * a.ndim)
    row = lambda w: pl.BlockSpec((tm, w), lambda i: (i, 0))
    pos = pl.BlockSpec((tm, LANES), lambda i: (i % n_pos, 0))
    hw = MLA_HEADS * 2 * LANES
    wide = jax.ShapeDtypeStruct((t, hw), BF16)
    return pl.pallas_call(
        _l0_proj_kernel,
        grid=(t // tm,),
        in_specs=[row(D_MODEL), full(w_in), full(qg), full(wq), full(kvg), full(wkv), pos, pos],
        out_specs=[row(hw), row(hw), row(hw), row(CONV_CHANNELS)],
        out_shape=[wide, wide, wide, jax.ShapeDtypeStruct((t, CONV_CHANNELS), F32)],
        compiler_params=_params(("parallel",)),
        name="l0_proj",
    )(x2d, w_in, qg, wq, kvg, wkv, cos, sin)


def _l1_proj_kernel(x_ref, w_ref, cos_ref, sin_ref, q_ref, k_ref, v_ref):
    xb = x_ref[...].astype(BF16)
    cos = cos_ref[...]
    sin = sin_ref[...]
    scale = DIFF_HEAD_DIM ** -0.5
    qk_w = DIFF_HEADS * 2 * DIFF_HEAD_DIM
    q = jnp.dot(xb, w_ref[:, :qk_w], preferred_element_type=F32)
    for j in range(qk_w // LANES):
        c = j * LANES
        q_ref[:, c:c + LANES] = (_rope(q[:, c:c + LANES], cos, sin) * scale).astype(BF16)
    k = jnp.dot(xb, w_ref[:, qk_w:2 * qk_w], preferred_element_type=F32)
    for j in range(qk_w // LANES):
        c = j * LANES
        k_ref[:, c:c + LANES] = _rope(k[:, c:c + LANES], cos, sin).astype(BF16)
    v_ref[...] = jnp.dot(xb, w_ref[:, 2 * qk_w:], preferred_element_type=F32).astype(BF16)


def _l1_proj(x2d, w_qkv, cos, sin, seq):
    t = x2d.shape[0]
    tm = TOKEN_TILE
    n_pos = seq // tm
    row = lambda w: pl.BlockSpec((tm, w), lambda i: (i, 0))
    pos = pl.BlockSpec((tm, LANES), lambda i: (i % n_pos, 0))
    out = jax.ShapeDtypeStruct((t, D_MODEL), BF16)
    return pl.pallas_call(
        _l1_proj_kernel,
        grid=(t // tm,),
        in_specs=[row(D_MODEL), pl.BlockSpec(w_qkv.shape, lambda i: (0, 0)), pos, pos],
        out_specs=[row(D_MODEL)] * 3,
        out_shape=[out, out, out],
        compiler_params=_params(("parallel",)),
        name="l1_proj",
    )(x2d, w_qkv, cos, sin)


def _attn_kernel(*refs, n_maps, dk, tq, tk, n_chains, sum_col, lambda_init):
    if n_maps == 2:
        q_ref, k_ref, v_ref, lam_ref, g_ref, o_ref, m_sc, l_sc, acc_sc = refs
    else:
        q_ref, k_ref, v_ref, o_ref, m_sc, l_sc, acc_sc = refs
    qi = pl.program_id(2)
    rs = tq // n_chains
    dv = v_ref.shape[2]
    m_sc[...] = jnp.full(m_sc.shape, MASK_VALUE, F32)
    l_sc[...] = jnp.zeros(l_sc.shape, F32)
    acc_sc[...] = jnp.zeros(acc_sc.shape, F32)

    def chain(m, r, k, v, mask):
        rows = slice(r * rs, (r + 1) * rs)
        q = q_ref[0, rows, m * dk:(m + 1) * dk]
        s = lax.dot_general(q, k, (((1,), (1,)), ((), ())), preferred_element_type=F32)
        if mask is not None:
            s = jnp.where(mask, s, MASK_VALUE)
        m_prev = m_sc[m, rows, :]
        m_new = jnp.maximum(m_prev, jnp.max(s, axis=1, keepdims=True))
        p = jnp.exp(s - _repeat_lanes(m_new, s.shape[1] // LANES))
        a = jnp.exp(m_prev - m_new)
        if sum_col is None:
            l_sc[m, rows, :] = a * l_sc[m, rows, :] + jnp.sum(p, axis=1, keepdims=True)
        acc_sc[m, rows, :] = (_repeat_lanes(a, dv // LANES) * acc_sc[m, rows, :]
                              + jnp.dot(p.astype(BF16), v, preferred_element_type=F32))
        m_sc[m, rows, :] = m_new

    def off_diagonal(j, carry):
        start = pl.multiple_of(j * tk, tk)
        v = v_ref[0, pl.ds(start, tk), :]
        for m in range(n_maps):
            k = k_ref[0, pl.ds(start, tk), m * dk:(m + 1) * dk]
            for r in range(n_chains):
                chain(m, r, k, v, None)
        return carry

    lax.fori_loop(0, qi * (tq // tk), off_diagonal, 0)
    base = pl.multiple_of(qi * tq, tq)
    for r in range(n_chains):
        nk = (r + 1) * rs
        v = v_ref[0, pl.ds(base, nk), :]
        row = lax.broadcasted_iota(jnp.int32, (rs, nk), 0) + r * rs
        col = lax.broadcasted_iota(jnp.int32, (rs, nk), 1)
        mask = col <= row
        for m in range(n_maps):
            k = k_ref[0, pl.ds(base, nk), m * dk:(m + 1) * dk]
            chain(m, r, k, v, mask)

    if n_maps == 1:
        acc = acc_sc[0]
        o_ref[0] = (acc[:, :sum_col] / acc[:, sum_col:sum_col + 1]).astype(o_ref.dtype)
    else:
        lam_in = lam_ref[...]
        lam = (jnp.exp(jnp.sum(lam_in[0:1] * lam_in[1:2], axis=1, keepdims=True))
               - jnp.exp(jnp.sum(lam_in[2:3] * lam_in[3:4], axis=1, keepdims=True)) + lambda_init)
        a = acc_sc[0] / l_sc[0][:, 0:1] - lam * (acc_sc[1] / l_sc[1][:, 0:1])
        o_ref[0] = (_rms_norm(a, g_ref[...]) * (1.0 - lambda_init)).astype(o_ref.dtype)


def _attention(q, k, v, n_heads, n_maps, dk, dv_in, dv_out, extra=(), lambda_init=0.0):
    b, s, _ = q.shape
    tq, tk = ATTN_Q_TILE, ATTN_K_TILE
    qw = n_maps * dk
    sum_col = dv_out if dv_in > dv_out else None
    extra_specs = [pl.BlockSpec(e.shape, lambda bi, h, qi: (0, 0)) for e in extra]
    kern = functools.partial(_attn_kernel, n_maps=n_maps, dk=dk, tq=tq, tk=tk, n_chains=ATTN_CHAINS,
                             sum_col=sum_col, lambda_init=lambda_init)
    return pl.pallas_call(
        kern,
        grid=(b, n_heads, s // tq),
        in_specs=[pl.BlockSpec((1, tq, qw), lambda bi, h, qi: (bi, qi, h)),
                  pl.BlockSpec((1, s, qw), lambda bi, h, qi: (bi, 0, h)),
                  pl.BlockSpec((1, s, dv_in), lambda bi, h, qi: (bi, 0, h))] + extra_specs,
        out_specs=pl.BlockSpec((1, tq, dv_out), lambda bi, h, qi: (bi, qi, h)),
        out_shape=jax.ShapeDtypeStruct((b, s, n_heads * dv_out), BF16),
        scratch_shapes=[pltpu.VMEM((n_maps, tq, LANES), F32), pltpu.VMEM((n_maps, tq, LANES), F32),
                        pltpu.VMEM((n_maps, tq, dv_in), F32)],
        compiler_params=_params(("parallel", "parallel", "arbitrary")),
        name="attention_%dmap" % n_maps,
    )(q, k, v, *extra)


def _conv_kernel(u_ref, w_ref, cb_ref, g_ref, b_ref, o_ref, ext, win_sc):
    ts = u_ref.shape[0]
    si = pl.program_id(1)

    @pl.when(si == 0)
    def _():
        ext[0:CONV_HALO, :] = jnp.zeros((CONV_HALO, CONV_CHANNELS), F32)

    @pl.when(si > 0)
    def _():
        ext[0:CONV_HALO, :] = ext[ts:ts + CONV_HALO, :]

    ext[CONV_HALO:CONV_HALO + ts, :] = u_ref[...]
    first = CONV_HALO - (CONV_WIDTH - 1)
    for c in range(ts // CONV_CHUNK):
        r0 = c * CONV_CHUNK
        acc = jnp.zeros((CONV_CHUNK, CONV_CHANNELS), F32)
        for shift in range(ROW_TILE):
            offs = [first + j - shift for j in range(CONV_WIDTH) if (first + j) % ROW_TILE == shift]
            n_win = max(offs) + CONV_CHUNK
            win = win_sc.at[(c * ROW_TILE + shift) % 2]
            win[0:n_win, :] = ext[r0 + shift:r0 + shift + n_win, :]
            for off in offs:
                j = off + shift - first
                acc = acc + w_ref[j:j + 1, :] * win[off:off + CONV_CHUNK, :]
        y = _layer_norm(acc + cb_ref[...], g_ref[...], b_ref[...])
        o_ref[r0:r0 + CONV_CHUNK, :] = (y * jax.nn.sigmoid(y)).astype(o_ref.dtype)


def _conv_module(u2d, conv_w, conv_b, ln_g, ln_b, batch, seq):
    ts = CONV_TILE
    n_s = seq // ts
    vec = pl.BlockSpec((1, CONV_CHANNELS), lambda bi, si: (0, 0))
    return pl.pallas_call(
        _conv_kernel,
        grid=(batch, n_s),
        in_specs=[pl.BlockSpec((ts, CONV_CHANNELS), lambda bi, si: (bi * n_s + si, 0)),
                  pl.BlockSpec(conv_w.shape, lambda bi, si: (0, 0)), vec, vec, vec],
        out_specs=pl.BlockSpec((ts, CONV_CHANNELS), lambda bi, si: (bi * n_s + si, 0)),
        out_shape=jax.ShapeDtypeStruct((batch * seq, CONV_CHANNELS), BF16),
        scratch_shapes=[pltpu.VMEM((ts + CONV_HALO, CONV_CHANNELS), F32),
                        pltpu.VMEM((2, CONV_CHUNK + CONV_HALO, CONV_CHANNELS), F32)],
        compiler_params=_params(("arbitrary", "arbitrary")),
        name="conv_module",
    )(u2d, conv_w, conv_b, ln_g, ln_b)


def _out_ln_kernel(*refs, n_in):
    a_refs = refs[:n_in]
    w_refs = refs[n_in:2 * n_in]
    x_ref, g_ref, b_ref, ot_ref = refs[2 * n_in:]
    mix = jnp.dot(a_refs[0][...], w_refs[0][...], preferred_element_type=F32)
    for a_ref, w_ref in zip(a_refs[1:], w_refs[1:]):
        mix = mix + jnp.dot(a_ref[...], w_ref[...], preferred_element_type=F32)
    _store_rows(ot_ref, _layer_norm(DN_ALPHA * x_ref[...] + mix, g_ref[...], b_ref[...]))


def _out_ln(acts, weights, x2d, g, b):
    t = x2d.shape[0]
    tm = TOKEN_TILE
    vec = pl.BlockSpec((1, D_MODEL), lambda i: (0, 0))
    return pl.pallas_call(
        functools.partial(_out_ln_kernel, n_in=len(acts)),
        grid=(t // tm,),
        in_specs=([pl.BlockSpec((tm, a.shape[1]), lambda i: (i, 0)) for a in acts]
                  + [pl.BlockSpec(w.shape, lambda i: (0, 0)) for w in weights]
                  + [pl.BlockSpec((tm, D_MODEL), lambda i: (i, 0)), vec, vec]),
        out_specs=pl.BlockSpec((tm * ROW_TILE, LANES), lambda i: (i, 0)),
        out_shape=jax.ShapeDtypeStruct((t * ROW_TILE, LANES), F32),
        compiler_params=_params(("parallel",)),
        name="out_proj_ln",
    )(*acts, *weights, x2d, g, b)


def _router_kernel(x_ref, rw_ref, rb_ref, tri_ref, low_ref, gate_ref, lrow_ref, before_ref, tcnt_ref,
                   cnt_ref):
    tm = x_ref.shape[0] // ROW_TILE
    slot_base = (pl.program_id(0) % 2) * (TOP_K * tm * ROW_TILE)

    @pl.when(pl.program_id(0) == 0)
    def _():
        cnt_ref[...] = jnp.zeros(cnt_ref.shape, F32)

    x = _load_rows(x_ref, tm)
    x_top = pltpu.bitcast(pltpu.bitcast(x, jnp.uint32) & jnp.uint32(0xFFFF0000), F32)
    x_hi = x_top.astype(BF16)
    x_lo = (x - x_top).astype(BF16)
    nt = (((1,), (1,)), ((), ()))
    by_hi = lax.dot_general(rw_ref[...], x_hi, nt, preferred_element_type=F32)
    by_lo = lax.dot_general(rw_ref[...], x_lo, nt, preferred_element_type=F32)
    logits = ((by_hi[:N_EXPERTS] + by_hi[N_EXPERTS:]) + (by_lo[:N_EXPERTS] + by_lo[N_EXPERTS:])
              + rb_ref[...])
    e_iota = lax.broadcasted_iota(jnp.int32, (N_EXPERTS, tm), 0)
    vals, sels = [], []
    work = logits
    for k in range(TOP_K):
        top = jnp.max(work, axis=0, keepdims=True)
        idx = jnp.min(jnp.where(work == top, e_iota, N_EXPERTS), axis=0, keepdims=True)
        sel = e_iota == idx
        vals.append(top)
        sels.append(sel)
        work = jnp.where(sel, -jnp.inf, work)
    exps = [jnp.exp(v - vals[0]) for v in vals]
    denom = exps[0] + exps[1] + exps[2] + exps[3]
    for k in range(TOP_K):
        gate_ref[0, k:k + 1, :] = exps[k] / denom
    chosen = jnp.where(sels[0] | sels[1] | sels[2] | sels[3], 1.0, 0.0)
    earlier = jnp.dot(chosen.astype(BF16), tri_ref[...], preferred_element_type=F32)
    before = cnt_ref[...]
    tile_cnt = jnp.sum(chosen, axis=1, keepdims=True)
    cnt16 = jnp.floor(tile_cnt * (1.0 / 16.0))
    cnt_r = tile_cnt - 16.0 * cnt16
    prefix = lambda c: jnp.dot(low_ref[...], jnp.broadcast_to(c, before.shape).astype(BF16),
                               preferred_element_type=F32)
    local = 16.0 * prefix(cnt16) + prefix(cnt_r)
    lbase = earlier + local[:, 0:1]
    for k in range(TOP_K):
        lrow = jnp.sum(jnp.where(sels[k], lbase, 0.0), axis=0, keepdims=True)
        lrow_ref[0, k:k + 1, :] = lrow.astype(jnp.int32) * ROW_TILE + slot_base
    before_ref[0] = before
    tcnt_ref[0] = jnp.broadcast_to(tile_cnt, before.shape)
    cnt_ref[...] = before + tile_cnt


def _router(xt, rw_t, rb, tri):
    t = xt.shape[0] // ROW_TILE
    tm = TOKEN_TILE
    n = t // tm
    kt = pl.BlockSpec((1, TOP_K, tm), lambda i: (i, 0, 0))
    per_tile = pl.BlockSpec((1, N_EXPERTS, LANES), lambda i: (i, 0, 0))
    low = (jnp.arange(N_EXPERTS)[:, None] > jnp.arange(N_EXPERTS)[None, :]).astype(BF16)
    ints = jax.ShapeDtypeStruct((n, TOP_K, tm), jnp.int32)
    tiles = jax.ShapeDtypeStruct((n, N_EXPERTS, LANES), F32)
    return pl.pallas_call(
        _router_kernel,
        grid=(n,),
        in_specs=[pl.BlockSpec((tm * ROW_TILE, LANES), lambda i: (i, 0)),
                  pl.BlockSpec(rw_t.shape, lambda i: (0, 0)),
                  pl.BlockSpec(rb.shape, lambda i: (0, 0)),
                  pl.BlockSpec(tri.shape, lambda i: (0, 0)),
                  pl.BlockSpec(low.shape, lambda i: (0, 0))],
        out_specs=[kt, kt, per_tile, per_tile, pl.BlockSpec((N_EXPERTS, LANES), lambda i: (0, 0))],
        out_shape=[jax.ShapeDtypeStruct((n, TOP_K, tm), F32), ints, tiles, tiles,
                   jax.ShapeDtypeStruct((N_EXPERTS, LANES), F32)],
        compiler_params=_params(("arbitrary",)),
        name="router",
    )(xt, rw_t, rb, tri, low)


def _dispatch_kernel(pend_ref, padded_ref, rdst_ref, rsrc_ref, rlen_ref, lrow_ref, xt_ref, xs_ref,
                     zeros, stage, zsem, sems):
    td = lrow_ref.shape[0] // TOP_K
    zrows = zeros.shape[0]
    slot_rows = stage.shape[0] // 2
    i = pl.program_id(0)
    slot = i % 2

    @pl.when(pl.program_id(0) == 0)
    def _():
        zeros[...] = jnp.zeros(zeros.shape, F32)

        def zero_block(start):
            return pltpu.make_async_copy(zeros, xs_ref.at[pl.ds(pl.multiple_of(start, zrows), zrows)], zsem)

        def zero_tail(fn):
            def body(e, carry):
                @pl.when(padded_ref[e] > 0)
                def _():
                    fn(zero_block(pend_ref[e] * ROW_TILE - zrows))
                return carry

            lax.fori_loop(0, N_EXPERTS, body, 0)

        def zero_unused(fn):
            def body(blk, carry):
                fn(zero_block(blk * zrows))
                return carry

            lax.fori_loop(pend_ref[N_EXPERTS - 1] * ROW_TILE // zrows, xs_ref.shape[0] // zrows, body, 0)

        zero_tail(lambda cp: cp.start())
        zero_unused(lambda cp: cp.start())
        zero_tail(lambda cp: cp.wait())
        zero_unused(lambda cp: cp.wait())

    def start_runs(tile, s):
        def per_expert(e, carry):
            n_rows = rlen_ref[tile * N_EXPERTS + e]
            src = rsrc_ref[tile * N_EXPERTS + e]
            dst = rdst_ref[tile * N_EXPERTS + e]
            for bit in reversed(range(td.bit_length())):
                size = (1 << bit) * ROW_TILE
                has = (n_rows & (1 << bit)) != 0

                @pl.when(has)
                def _(src=src, dst=dst, size=size):
                    pltpu.make_async_copy(
                        stage.at[pl.ds(pl.multiple_of(s * slot_rows + src, ROW_TILE), size)],
                        xs_ref.at[pl.ds(pl.multiple_of(dst, ROW_TILE), size)], sems.at[s]).start()

                step = jnp.where(has, size, 0)
                src = src + step
                dst = dst + step
            return carry

        lax.fori_loop(0, N_EXPERTS, per_expert, 0)

    def wait_tile(s):
        pltpu.make_async_copy(stage.at[pl.ds(pl.multiple_of(s * slot_rows, ROW_TILE), slot_rows)],
                              xs_ref.at[pl.ds(0, slot_rows)], sems.at[s]).wait()

    def compact(t8, carry):
        for u in range(ROW_TILE):
            row = xt_ref[pl.ds(pl.multiple_of((t8 * ROW_TILE + u) * ROW_TILE, ROW_TILE), ROW_TILE), :]
            for k in range(TOP_K):
                dst = pl.multiple_of(lrow_ref[k * td + t8 * ROW_TILE + u], ROW_TILE)
                stage[pl.ds(dst, ROW_TILE), :] = row
        return carry

    lax.fori_loop(0, td // ROW_TILE, compact, 0)
    start_runs(i, slot)

    @pl.when(i > 0)
    def _():
        wait_tile(1 - slot)

    @pl.when(i == pl.num_programs(0) - 1)
    def _():
        wait_tile(slot)


def _dispatch(pend, padded, rdst, rsrc, rlen, lrow8, xt, n_rows):
    t = lrow8.shape[0] // TOP_K
    td = TOKEN_TILE
    smem = lambda i, *_: (i,)
    return pl.pallas_call(
        _dispatch_kernel,
        grid_spec=pltpu.PrefetchScalarGridSpec(
            num_scalar_prefetch=5,
            grid=(t // td,),
            in_specs=[pl.BlockSpec((TOP_K * td,), smem, memory_space=pltpu.SMEM),
                      pl.BlockSpec((td * ROW_TILE, LANES), lambda i, *_: (i, 0))],
            out_specs=pl.BlockSpec(memory_space=pl.ANY),
            scratch_shapes=[pltpu.VMEM((EXPERT_ROWS * ROW_TILE, LANES), F32),
                            pltpu.VMEM((2 * TOP_K * td * ROW_TILE, LANES), F32),
                            pltpu.SemaphoreType.DMA(()), pltpu.SemaphoreType.DMA((2,))]),
        out_shape=jax.ShapeDtypeStruct((n_rows * ROW_TILE, LANES), F32),
        compiler_params=_params(("arbitrary",)),
        name="moe_dispatch",
    )(pend, padded, rdst, rsrc, rlen, lrow8, xt)


def _expert_kernel(blk_e_ref, grp_ref, nxt_ref, valid_ref, n_used_ref, xs_ref, wgu_hbm, bgu_ref, wdn_hbm,
                   bdn_ref, ys_ref, wgu_f32, wdn_f32, wgu_bf, wdn_bf, sems):
    i = pl.program_id(0)
    rb = xs_ref.shape[0] // ROW_TILE
    active = i < n_used_ref[0]
    new_expert = (i == 0) | (blk_e_ref[i] != blk_e_ref[jnp.maximum(i - 1, 0)])

    def weight_copies(e, slot):
        return (pltpu.make_async_copy(wgu_hbm.at[e], wgu_f32.at[slot], sems.at[0, slot]),
                pltpu.make_async_copy(wdn_hbm.at[e], wdn_f32.at[slot], sems.at[1, slot]))

    @pl.when(active & new_expert)
    def _():
        slot = grp_ref[i] % 2
        e = blk_e_ref[i]
        nxt = nxt_ref[i]

        @pl.when(i == 0)
        def _():
            for cp in weight_copies(e, slot):
                cp.start()

        @pl.when(nxt >= 0)
        def _():
            for cp in weight_copies(nxt, 1 - slot):
                cp.start()

        for cp in weight_copies(e, slot):
            cp.wait()

        def cast(c, carry):
            rows = pl.ds(pl.multiple_of(c * CAST_CHUNK, CAST_CHUNK), CAST_CHUNK)
            wgu_bf[rows, :] = wgu_f32[slot, rows, :].astype(BF16)
            wdn_bf[rows, :] = wdn_f32[slot, rows, :].astype(BF16)
            return carry

        lax.fori_loop(0, D_MODEL // CAST_CHUNK, cast, 0)

    def swiglu_rows(n_rows):
        xb = _load_rows(xs_ref, n_rows).astype(BF16)
        h = jnp.dot(xb, wgu_bf[...], preferred_element_type=F32) + bgu_ref[0]
        gate = jnp.minimum(h[:, :D_EXPERT], SWIGLU_LIMIT)
        up = jnp.clip(h[:, D_EXPERT:], -SWIGLU_LIMIT, SWIGLU_LIMIT)
        act = (up + 1.0) * gate * jax.nn.sigmoid(SWIGLU_ALPHA * gate)
        y = jnp.dot(act.astype(BF16), wdn_bf[...], preferred_element_type=F32) + bdn_ref[0]
        _store_rows(ys_ref, y)

    half_only = valid_ref[i] <= rb // 2

    @pl.when(active & jnp.logical_not(half_only))
    def _():
        swiglu_rows(rb)

    @pl.when(active & half_only)
    def _():
        swiglu_rows(rb // 2)
        ys_ref[rb // 2 * ROW_TILE:, :] = jnp.zeros((rb // 2 * ROW_TILE, LANES), F32)

    @pl.when(jnp.logical_not(active))
    def _():
        ys_ref[...] = jnp.zeros(ys_ref.shape, F32)


def _experts(blk_e, valid, n_used, xs, w_gu, b_gu, w_dn, b_dn):
    rb = EXPERT_ROWS
    n_blk = xs.shape[0] // (rb * ROW_TILE)
    ids = jnp.arange(n_blk, dtype=jnp.int32)
    change = jnp.concatenate([jnp.ones((1,), bool), blk_e[1:] != blk_e[:-1]])
    grp = (jnp.cumsum(change.astype(jnp.int32)) - 1).astype(jnp.int32)
    later = jnp.where(change, ids, n_blk)
    nxt_pos = jnp.concatenate([lax.cummin(later[::-1])[::-1][1:], jnp.full((1,), n_blk, jnp.int32)])
    nxt = jnp.where(nxt_pos < n_blk, blk_e[jnp.minimum(nxt_pos, n_blk - 1)], -1).astype(jnp.int32)
    rows = pl.BlockSpec((rb * ROW_TILE, LANES), lambda i, be, gr, nx, va, nu: (jnp.minimum(i, nu[0] - 1), 0))
    rows_out = pl.BlockSpec((rb * ROW_TILE, LANES), lambda i, be, gr, nx, va, nu: (i, 0))
    per_e = lambda a: pl.BlockSpec((1,) + a.shape[1:], lambda i, be, gr, nx, va, nu: (be[i], 0, 0))
    hbm = pl.BlockSpec(memory_space=pl.ANY)
    return pl.pallas_call(
        _expert_kernel,
        grid_spec=pltpu.PrefetchScalarGridSpec(
            num_scalar_prefetch=5,
            grid=(n_blk,),
            in_specs=[rows, hbm, per_e(b_gu), hbm, per_e(b_dn)],
            out_specs=rows_out,
            scratch_shapes=[pltpu.VMEM((2,) + w_gu.shape[1:], F32), pltpu.VMEM((2,) + w_dn.shape[1:], F32),
                            pltpu.VMEM(w_gu.shape[1:], BF16), pltpu.VMEM(w_dn.shape[1:], BF16),
                            pltpu.SemaphoreType.DMA((2, 2))]),
        out_shape=jax.ShapeDtypeStruct(xs.shape, F32),
        compiler_params=_params(("arbitrary",)),
        name="moe_experts",
    )(blk_e, grp, nxt, valid, n_used, xs, w_gu, b_gu, w_dn, b_dn)


def _combine_kernel(rdst_ref, rsrc_ref, rlen_ref, lrow_ref, gate_ref, x_ref, ys_ref, g_ref, b_ref, o_ref,
                    stage, moe_sc, sems):
    td = o_ref.shape[0]
    slot_rows = stage.shape[0] // 2
    i = pl.program_id(0)
    slot = i % 2

    def start_runs(tile, s):
        def per_expert(e, carry):
            n_rows = rlen_ref[tile * N_EXPERTS + e]
            src = rdst_ref[tile * N_EXPERTS + e]
            dst = rsrc_ref[tile * N_EXPERTS + e]
            for bit in reversed(range(td.bit_length())):
                size = (1 << bit) * ROW_TILE
                has = (n_rows & (1 << bit)) != 0

                @pl.when(has)
                def _(src=src, dst=dst, size=size):
                    pltpu.make_async_copy(
                        ys_ref.at[pl.ds(pl.multiple_of(src, ROW_TILE), size)],
                        stage.at[pl.ds(pl.multiple_of(s * slot_rows + dst, ROW_TILE), size)],
                        sems.at[s]).start()

                step = jnp.where(has, size, 0)
                src = src + step
                dst = dst + step
            return carry

        lax.fori_loop(0, N_EXPERTS, per_expert, 0)

    @pl.when(i == 0)
    def _():
        start_runs(0, 0)

    @pl.when(i + 1 < pl.num_programs(0))
    def _():
        start_runs(i + 1, 1 - slot)

    pltpu.make_async_copy(ys_ref.at[pl.ds(0, slot_rows)],
                          stage.at[pl.ds(pl.multiple_of(slot * slot_rows, ROW_TILE), slot_rows)],
                          sems.at[slot]).wait()

    def reduce_rows(t8, carry):
        for u in range(ROW_TILE):
            tok = t8 * ROW_TILE + u
            acc = None
            for k in range(TOP_K):
                row = stage[pl.ds(pl.multiple_of(lrow_ref[k * td + tok], ROW_TILE), ROW_TILE), :]
                term = gate_ref[k * td + tok] * row
                acc = term if acc is None else acc + term
            moe_sc[pl.ds(pl.multiple_of((t8 * ROW_TILE + u) * ROW_TILE, ROW_TILE), ROW_TILE), :] = acc
        return carry

    lax.fori_loop(0, td // ROW_TILE, reduce_rows, 0)
    moe = _load_rows(moe_sc, td)
    o_ref[...] = _layer_norm(DN_ALPHA * _load_rows(x_ref, td) + moe, g_ref[...], b_ref[...])


def _combine(rdst, rsrc, rlen, lrow8, gates_flat, xt, ys, g, b):
    t = xt.shape[0] // ROW_TILE
    td = TOKEN_TILE
    vec = pl.BlockSpec((1, D_MODEL), lambda i, *_: (0, 0))
    smem = pl.BlockSpec((TOP_K * td,), lambda i, *_: (i,), memory_space=pltpu.SMEM)
    return pl.pallas_call(
        _combine_kernel,
        grid_spec=pltpu.PrefetchScalarGridSpec(
            num_scalar_prefetch=3,
            grid=(t // td,),
            in_specs=[smem, smem, pl.BlockSpec((td * ROW_TILE, LANES), lambda i, *_: (i, 0)),
                      pl.BlockSpec(memory_space=pl.ANY), vec, vec],
            out_specs=pl.BlockSpec((td, D_MODEL), lambda i, *_: (i, 0)),
            scratch_shapes=[pltpu.VMEM((2 * TOP_K * td * ROW_TILE, LANES), F32),
                            pltpu.VMEM((td * ROW_TILE, LANES), F32), pltpu.SemaphoreType.DMA((2,))]),
        out_shape=jax.ShapeDtypeStruct((t, D_MODEL), F32),
        compiler_params=_params(("arbitrary",)),
        name="moe_combine_ln",
    )(rdst, rsrc, rlen, lrow8, gates_flat, xt, ys, g, b)


def _moe_ln(xt, router_w, router_b, w_gu, b_gu, w_dn, b_dn, ln_g, ln_b, tri):
    t = xt.shape[0] // ROW_TILE
    rb = EXPERT_ROWS
    rw_t = router_w.T.astype(F32)
    rw_top = lax.bitcast_convert_type(
        lax.bitcast_convert_type(rw_t, jnp.uint32) & jnp.uint32(0xFFFF0000), F32)
    rw_split = jnp.concatenate([rw_top.astype(BF16), (rw_t - rw_top).astype(BF16)], axis=0)
    gates, lrow, before, tile_cnt, cnt = _router(xt, rw_split, router_b.reshape(N_EXPERTS, 1), tri)
    counts = cnt[:, 0].astype(jnp.int32)
    padded = ((counts + rb - 1) // rb) * rb
    pend = jnp.cumsum(padded).astype(jnp.int32)
    pstart = pend - padded
    n_blk = (t * TOP_K) // rb + N_EXPERTS
    n_used = pend[-1] // rb
    blk = jnp.minimum(jnp.arange(n_blk, dtype=jnp.int32), n_used - 1)
    blk_e = jnp.sum((blk[:, None] * rb >= pend[None, :]).astype(jnp.int32), axis=1)
    blk_e = jnp.minimum(blk_e, N_EXPERTS - 1).astype(jnp.int32)
    valid = jnp.clip((pstart + counts)[blk_e] - blk * rb, 0, rb).astype(jnp.int32)
    lrow8 = lrow.reshape(-1)
    run_len = tile_cnt[:, :, 0].astype(jnp.int32)
    run_src = (jnp.cumsum(run_len, axis=1) - run_len) * ROW_TILE
    run_dst = (pstart[None, :] + before[:, :, 0].astype(jnp.int32)) * ROW_TILE
    flat = lambda a: a.reshape(-1).astype(jnp.int32)

    run_dst, run_src, run_len = flat(run_dst), flat(run_src), flat(run_len)

    xs = _dispatch(pend, padded.astype(jnp.int32), run_dst, run_src, run_len, lrow8, xt, n_blk * rb)
    ys = _experts(blk_e, valid, n_used.reshape(1).astype(jnp.int32), xs,
                  w_gu, b_gu.reshape(N_EXPERTS, 1, -1), w_dn, b_dn.reshape(N_EXPERTS, 1, -1))
    return _combine(run_dst, run_src, run_len, lrow8, gates.reshape(-1), xt, ys,
                    ln_g.reshape(1, -1), ln_b.reshape(1, -1))


def _spread_rope_cols(w):
    half = QK_ROPE_DIM // 2
    z = jnp.zeros(w.shape[:-1] + (LANES // 2 - half,), w.dtype)
    return jnp.concatenate([w[..., :half], z, w[..., half:], z], axis=-1)


def _rope_tables(seq, dim, spread):
    inv_freq = 1.0 / (ROPE_THETA ** (jnp.arange(0, dim, 2, dtype=F32) / dim))
    ang = jnp.arange(seq, dtype=F32)[:, None] * inv_freq[None, :]
    cos, sin = jnp.cos(ang), jnp.sin(ang)
    if spread:
        z = jnp.zeros((seq, LANES // 2 - dim // 2), F32)
        return (jnp.concatenate([cos, z, cos, z], axis=1), jnp.concatenate([-sin, z, sin, z], axis=1))
    return jnp.concatenate([cos, cos], axis=1), jnp.concatenate([-sin, sin], axis=1)


def kernel(x, l0_w_in, l0_q_norm_g, l0_w_q_up, l0_kv_norm_g, l0_w_kv_up, l0_conv_w, l0_conv_b, l0_conv_ln_g, l0_conv_ln_b, l0_w_o, l0_ln1_g, l0_ln1_b, l0_router_w, l0_router_b, l0_w_gu, l0_b_gu, l0_w_dn, l0_b_dn, l0_ln2_g, l0_ln2_b, l1_w_qkv, l1_lambda_q1, l1_lambda_k1, l1_lambda_q2, l1_lambda_k2, l1_subln_g, l1_w_o, l1_ln1_g, l1_ln1_b, l1_router_w, l1_router_b, l1_w_gu, l1_b_gu, l1_w_dn, l1_b_dn, l1_ln2_g, l1_ln2_b):
    b, s, d = x.shape
    t = b * s
    x2d = x.reshape(t, d)
    row = lambda a: a.reshape(1, -1)
    tri = (jnp.arange(TOKEN_TILE)[:, None] < jnp.arange(TOKEN_TILE)[None, :]).astype(BF16)

    o1 = Q_LORA_RANK
    o2 = o1 + KV_LORA_RANK
    o3 = o2 + QK_ROPE_DIM
    w_in = jnp.concatenate([l0_w_in[:, :o2], _spread_rope_cols(l0_w_in[:, o2:o3]), l0_w_in[:, o3:]],
                           axis=1).astype(BF16)
    wq = l0_w_q_up.reshape(Q_LORA_RANK, MLA_HEADS, QK_NOPE_DIM + QK_ROPE_DIM)
    wq = jnp.concatenate([wq[..., :QK_NOPE_DIM], _spread_rope_cols(wq[..., QK_NOPE_DIM:])], axis=-1)
    wq = wq.reshape(Q_LORA_RANK, MLA_HEADS * 2 * LANES).astype(BF16)
    wkv = l0_w_kv_up.reshape(KV_LORA_RANK, MLA_HEADS, QK_NOPE_DIM + V_HEAD_DIM)
    wkv = jnp.concatenate([wkv[..., :QK_NOPE_DIM].reshape(KV_LORA_RANK, -1),
                           wkv[..., QK_NOPE_DIM:].reshape(KV_LORA_RANK, -1)], axis=1).astype(BF16)
    cos0, sin0 = _rope_tables(s, QK_ROPE_DIM, spread=True)
    q, k, v, u = _l0_proj(x2d, w_in, row(l0_q_norm_g), wq, row(l0_kv_norm_g), wkv, cos0, sin0, s)
    hw = MLA_HEADS * 2 * LANES
    attn = _attention(q.reshape(b, s, hw), k.reshape(b, s, hw), v.reshape(b, s, hw),
                      MLA_HEADS, 1, 2 * LANES, 2 * LANES, V_HEAD_DIM)
    conv_w = jnp.concatenate([l0_conv_w, jnp.zeros((1, CONV_CHANNELS), F32)], axis=0)
    uc = _conv_module(u, conv_w, row(l0_conv_b), row(l0_conv_ln_g), row(l0_conv_ln_b), b, s)
    n_attn = MLA_HEADS * V_HEAD_DIM
    w_o = l0_w_o.astype(BF16)
    xt = _out_ln([attn.reshape(t, n_attn), uc], [w_o[:n_attn], w_o[n_attn:]], x2d,
                 row(l0_ln1_g), row(l0_ln1_b))
    x2d = _moe_ln(xt, l0_router_w, l0_router_b, l0_w_gu, l0_b_gu, l0_w_dn, l0_b_dn,
                  l0_ln2_g, l0_ln2_b, tri)

    lambda_init = 0.8 - 0.6 * math.exp(-0.3 * 1)
    cos1, sin1 = _rope_tables(s, DIFF_HEAD_DIM, spread=False)
    q, k, v = _l1_proj(x2d, l1_w_qkv.astype(BF16), cos1, sin1, s)
    lam_in = jnp.stack([l1_lambda_q1, l1_lambda_k1, l1_lambda_q2, l1_lambda_k2]).astype(F32)
    dv = 2 * DIFF_HEAD_DIM
    attn = _attention(q.reshape(b, s, d), k.reshape(b, s, d), v.reshape(b, s, d),
                      DIFF_HEADS, 2, DIFF_HEAD_DIM, dv, dv,
                      extra=(lam_in, row(l1_subln_g)), lambda_init=lambda_init)
    xt = _out_ln([attn.reshape(t, d)], [l1_w_o.astype(BF16)], x2d, row(l1_ln1_g), row(l1_ln1_b))
    x2d = _moe_ln(xt, l1_router_w, l1_router_b, l1_w_gu, l1_b_gu, l1_w_dn, l1_b_dn,
                  l1_ln2_g, l1_ln2_b, tri)
    return x2d.reshape(b, s, d)
```
